```python
import math
import jax
import jax.numpy as jnp
from jax import lax
import numpy as np

D_MODEL = 2048
BATCH = 4
SEQ = 2048
DEPTH = 1
DEC_BATCH = 128
DEC_SEQ = 8
PAST_LEN = 8192
PAGE_SIZE = 128

HEAD_DIM = 128
MIX_WIDTH = D_MODEL
SWA_Q_HEADS = MIX_WIDTH // 2 // HEAD_DIM
SWA_KV_HEADS = 2
SWA_GROUP = SWA_Q_HEADS // SWA_KV_HEADS
SWA_WIDTH = SWA_Q_HEADS * HEAD_DIM
SWA_KV_WIDTH = SWA_KV_HEADS * HEAD_DIM
X_HEADS = 4
X_WIDTH = X_HEADS * HEAD_DIM
CONV_WIDTH = MIX_WIDTH - SWA_WIDTH - X_WIDTH
CONV_K = 3
WINDOW = 128
ATTN_BLOCK = WINDOW
NUM_BUCKETS = 32
MAX_DISTANCE = WINDOW
MEM_TOKENS = 256
D_FF = 4 * D_MODEL
EPS = 1e-6
NEG = -1e30
IN_SPLITS = (CONV_WIDTH, 2 * CONV_WIDTH, 3 * CONV_WIDTH, 3 * CONV_WIDTH + SWA_WIDTH,
             3 * CONV_WIDTH + SWA_WIDTH + SWA_KV_WIDTH, 3 * CONV_WIDTH + SWA_WIDTH + 2 * SWA_KV_WIDTH)
IN_WIDTH = IN_SPLITS[-1] + X_WIDTH

kernel_name = 'hybrid_conv_swa_memory_decode_step'


def rmsnorm(x, g):
    xf = x.astype(jnp.float32)
    y = xf * lax.rsqrt(jnp.mean(xf * xf, axis=-1, keepdims=True) + EPS)
    return (y * g.astype(jnp.float32)).astype(x.dtype)


def rel_bucket(dist):
    n = jnp.maximum(dist, 0)
    max_exact = NUM_BUCKETS // 2
    nf = jnp.maximum(n, 1).astype(jnp.float32)
    large = max_exact + (jnp.log(nf / max_exact) / math.log(MAX_DISTANCE / max_exact)
                         * (NUM_BUCKETS - max_exact)).astype(jnp.int32)
    large = jnp.minimum(large, NUM_BUCKETS - 1)
    return jnp.where(n < max_exact, n, large)


def rel_bias(dist, table):
    b = jnp.moveaxis(table[rel_bucket(dist)], -1, 0)
    return b.reshape(SWA_KV_HEADS, SWA_GROUP, dist.shape[0], dist.shape[1])


def sink_attention(q, k, v, bias, valid, sink):
    s = jnp.einsum('...qhgd,...khd->...hgqk', q, k).astype(jnp.float32) * (HEAD_DIM ** -0.5)
    s = jnp.where(valid, s + bias.astype(jnp.float32), NEG)
    sink_l = sink.astype(jnp.float32).reshape(SWA_KV_HEADS, SWA_GROUP, 1, 1)
    m = jnp.maximum(jnp.max(s, axis=-1, keepdims=True), sink_l)
    p = jnp.exp(s - m)
    w = p / (jnp.sum(p, axis=-1, keepdims=True) + jnp.exp(sink_l - m))
    return jnp.einsum('...hgqk,...khd->...qhgd', w.astype(v.dtype), v)


def mixer_projections(x, g_mix, w_in, g_q_swa, g_k_swa, g_q_x):
    bsz, length = x.shape[0], x.shape[1]
    z = rmsnorm(x, g_mix) @ w_in
    b, c, hc, q, k, v, qx = jnp.split(z, IN_SPLITS, axis=-1)
    q = rmsnorm(q.reshape(bsz, length, SWA_Q_HEADS, HEAD_DIM), g_q_swa)
    k = rmsnorm(k.reshape(bsz, length, SWA_KV_HEADS, HEAD_DIM), g_k_swa)
    v = v.reshape(bsz, length, SWA_KV_HEADS, HEAD_DIM)
    qx = rmsnorm(qx.reshape(bsz, length, X_HEADS, HEAD_DIM), g_q_x)
    return b, c * hc, q, k, v, qx


def short_conv(u_ext, w):
    length = u_ext.shape[1] - (CONV_K - 1)
    out = w[0] * u_ext[:, 0:length]
    for i in range(1, CONV_K):
        out = out + w[i] * u_ext[:, i:i + length]
    return out


def swa_prompt(q, k, v, sink, table):
    bsz, seq = q.shape[0], q.shape[1]
    nb = seq // ATTN_BLOCK
    qb = q.reshape(bsz, nb, ATTN_BLOCK, SWA_KV_HEADS, SWA_GROUP, HEAD_DIM)

    def with_prev(t):
        tb = t.reshape(bsz, nb, ATTN_BLOCK, SWA_KV_HEADS, HEAD_DIM)
        prev = jnp.pad(tb[:, :-1], ((0, 0), (1, 0), (0, 0), (0, 0), (0, 0)))
        return jnp.concatenate([prev, tb], axis=2)

    qi = jnp.arange(ATTN_BLOCK)[:, None]
    ki = jnp.arange(2 * ATTN_BLOCK)[None, :]
    dist = ATTN_BLOCK + qi - ki
    band = (dist >= 0) & (dist < WINDOW)
    exists = (jnp.arange(nb)[:, None, None] > 0) | (ki[None] >= ATTN_BLOCK)
    valid = (band[None] & exists)[:, None, None]
    o = sink_attention(qb, with_prev(k), with_prev(v), rel_bias(dist, table), valid, sink)
    return o.reshape(bsz, seq, SWA_WIDTH)


def swa_sample(q, k_new, v_new, k_buf, v_buf, sink, table):
    bsz, t_len = q.shape[0], q.shape[1]
    k_all = jnp.concatenate([k_buf, k_new], axis=1)
    v_all = jnp.concatenate([v_buf, v_new], axis=1)
    dist = jnp.arange(t_len)[:, None] + WINDOW - jnp.arange(WINDOW + t_len)[None, :]
    valid = (dist >= 0) & (dist < WINDOW)
    qg = q.reshape(bsz, t_len, SWA_KV_HEADS, SWA_GROUP, HEAD_DIM)
    o = sink_attention(qg, k_all, v_all, rel_bias(dist, table), valid, sink)
    return o.reshape(bsz, t_len, SWA_WIDTH), k_all[:, -WINDOW:], v_all[:, -WINDOW:]


def memory_kv(mem, g_mem, w_mem_k, w_mem_v, g_k_x):
    bsz, m = mem.shape[0], mem.shape[1]
    h = rmsnorm(mem, g_mem)
    mk = rmsnorm((h @ w_mem_k).reshape(bsz, m, X_HEADS, HEAD_DIM), g_k_x)
    mv = (h @ w_mem_v).reshape(bsz, m, X_HEADS, HEAD_DIM)
    return mk, mv


def cross_attention(qx, mk, mv):
    s = jnp.einsum('bqhd,bkhd->bhqk', qx, mk).astype(jnp.float32) * (HEAD_DIM ** -0.5)
    w = jax.nn.softmax(s, axis=-1)
    o = jnp.einsum('bhqk,bkhd->bqhd', w.astype(mv.dtype), mv)
    return o.reshape(qx.shape[0], qx.shape[1], X_WIDTH)


def finish(x, y_conv, y_swa, y_x, w_out, g_mlp, w_up, w_down):
    x = x + jnp.concatenate([y_conv, y_swa, y_x], axis=-1) @ w_out
    a = jnp.square(jax.nn.relu(rmsnorm(x, g_mlp) @ w_up))
    return x + a @ w_down


def setup_inputs(seed: int = 0) -> dict:
    key = jax.random.key(seed)
    ks = jax.random.split(key, 32)

    def nrm(k, shape, scale=1.0):
        return scale * jax.random.normal(k, shape, jnp.float32)

    def gain(k, shape):
        return 1.0 + 0.01 * jax.random.normal(k, shape, jnp.float32)

    return {
        'x_prompt': nrm(ks[0], (BATCH, SEQ, D_MODEL)),
        'x_sample': nrm(ks[1], (DEC_BATCH, DEC_SEQ, D_MODEL)),
        'mem_prompt': nrm(ks[2], (BATCH, MEM_TOKENS, D_MODEL)),
        'cache_conv': nrm(ks[3], (DEPTH, DEC_BATCH, CONV_K - 1, CONV_WIDTH)),
        'cache_swa_k': nrm(ks[4], (DEPTH, DEC_BATCH, WINDOW, SWA_KV_HEADS, HEAD_DIM)),
        'cache_swa_v': nrm(ks[5], (DEPTH, DEC_BATCH, WINDOW, SWA_KV_HEADS, HEAD_DIM)),
        'cache_mem_k': nrm(ks[6], (DEPTH, DEC_BATCH, MEM_TOKENS, X_HEADS, HEAD_DIM)),
        'cache_mem_v': nrm(ks[7], (DEPTH, DEC_BATCH, MEM_TOKENS, X_HEADS, HEAD_DIM)),
        'rel_bias_table': nrm(ks[8], (NUM_BUCKETS, SWA_Q_HEADS), 0.5),
        'g_mix': gain(ks[9], (DEPTH, D_MODEL)),
        'w_in': nrm(ks[10], (DEPTH, D_MODEL, IN_WIDTH), D_MODEL ** -0.5),
        'conv_w': nrm(ks[11], (DEPTH, CONV_K, CONV_WIDTH), CONV_K ** -0.5),
        'g_q_swa': gain(ks[12], (DEPTH, HEAD_DIM)),
        'g_k_swa': gain(ks[13], (DEPTH, HEAD_DIM)),
        'sinks': nrm(ks[14], (DEPTH, SWA_Q_HEADS), 0.5),
        'g_q_x': gain(ks[15], (DEPTH, HEAD_DIM)),
        'g_k_x': gain(ks[16], (DEPTH, HEAD_DIM)),
        'g_mem': gain(ks[17], (DEPTH, D_MODEL)),
        'w_mem_k': nrm(ks[18], (DEPTH, D_MODEL, X_WIDTH), D_MODEL ** -0.5),
        'w_mem_v': nrm(ks[19], (DEPTH, D_MODEL, X_WIDTH), D_MODEL ** -0.5),
        'w_out': nrm(ks[20], (DEPTH, MIX_WIDTH, D_MODEL), MIX_WIDTH ** -0.5),
        'g_mlp': gain(ks[21], (DEPTH, D_MODEL)),
        'w_up': nrm(ks[22], (DEPTH, D_MODEL, D_FF), D_MODEL ** -0.5),
        'w_down': nrm(ks[23], (DEPTH, D_FF, D_MODEL), D_FF ** -0.5),
    }


def reference(x_prompt, x_sample, mem_prompt, cache_conv, cache_swa_k, cache_swa_v, cache_mem_k, cache_mem_v,
              rel_bias_table, g_mix, w_in, conv_w, g_q_swa, g_k_swa, sinks, g_q_x, g_k_x, g_mem,
              w_mem_k, w_mem_v, w_out, g_mlp, w_up, w_down):
    xp, xs = x_prompt, x_sample
    p_conv, p_k, p_v, p_mk, p_mv = [], [], [], [], []
    s_conv, s_k, s_v = [], [], []
    for l in range(DEPTH):
        b, u, q, k, v, qx = mixer_projections(xp, g_mix[l], w_in[l], g_q_swa[l], g_k_swa[l], g_q_x[l])
        u_ext = jnp.pad(u, ((0, 0), (CONV_K - 1, 0), (0, 0)))
        y_conv = b * short_conv(u_ext, conv_w[l])
        y_swa = swa_prompt(q, k, v, sinks[l], rel_bias_table)
        mk, mv = memory_kv(mem_prompt, g_mem[l], w_mem_k[l], w_mem_v[l], g_k_x[l])
        y_x = cross_attention(qx, mk, mv)
        p_conv.append(u_ext[:, -(CONV_K - 1):])
        p_k.append(k[:, -WINDOW:])
        p_v.append(v[:, -WINDOW:])
        p_mk.append(mk)
        p_mv.append(mv)
        xp = finish(xp, y_conv, y_swa, y_x, w_out[l], g_mlp[l], w_up[l], w_down[l])

        b, u, q, k, v, qx = mixer_projections(xs, g_mix[l], w_in[l], g_q_swa[l], g_k_swa[l], g_q_x[l])
        u_ext = jnp.concatenate([cache_conv[l], u], axis=1)
        y_conv = b * short_conv(u_ext, conv_w[l])
        y_swa, k_buf, v_buf = swa_sample(q, k, v, cache_swa_k[l], cache_swa_v[l], sinks[l], rel_bias_table)
        y_x = cross_attention(qx, cache_mem_k[l], cache_mem_v[l])
        s_conv.append(u_ext[:, -(CONV_K - 1):])
        s_k.append(k_buf)
        s_v.append(v_buf)
        xs = finish(xs, y_conv, y_swa, y_x, w_out[l], g_mlp[l], w_up[l], w_down[l])

    return (xp, xs, jnp.stack(p_conv), jnp.stack(p_k), jnp.stack(p_v), jnp.stack(p_mk), jnp.stack(p_mv),
            jnp.stack(s_conv), jnp.stack(s_k), jnp.stack(s_v))
```

```python
import functools
import math

import numpy as np
import jax
import jax.numpy as jnp
from jax import lax
from jax.experimental import pallas as pl
from jax.experimental.pallas import tpu as pltpu

D_MODEL = 2048
HEAD_DIM = 128
SWA_Q_HEADS = 8
SWA_KV_HEADS = 2
SWA_GROUP = SWA_Q_HEADS // SWA_KV_HEADS
SWA_WIDTH = SWA_Q_HEADS * HEAD_DIM
SWA_KV_WIDTH = SWA_KV_HEADS * HEAD_DIM
X_HEADS = 4
X_WIDTH = X_HEADS * HEAD_DIM
CONV_WIDTH = D_MODEL - SWA_WIDTH - X_WIDTH
CONV_K = 3
WINDOW = 128
NUM_BUCKETS = 32
MAX_DISTANCE = WINDOW
MEM_TOKENS = 256
D_FF = 4 * D_MODEL
EPS = 1e-6
NEG = -1e30
SCALE = HEAD_DIM ** -0.5

OFF_B = 0
OFF_C = CONV_WIDTH
OFF_H = 2 * CONV_WIDTH
OFF_Q = 3 * CONV_WIDTH
OFF_K = OFF_Q + SWA_WIDTH
OFF_V = OFF_K + SWA_KV_WIDTH
OFF_QX = OFF_V + SWA_KV_WIDTH
IN_WIDTH = OFF_QX + X_WIDTH

VMEM_LIMIT_V7X = 56 * 1024 * 1024
SUBLANES = 8

ROW_TILE = 512
FF_TILE = 1024
SAMPLE_BATCH_TILE = 8

BF16 = jnp.bfloat16
F32 = jnp.float32
NT_DIMS = (((1,), (1,)), ((), ()))


def _params(*sem):
    return pltpu.CompilerParams(dimension_semantics=sem, vmem_limit_bytes=VMEM_LIMIT_V7X)


def _resident(shape):
    nd = len(shape)
    return pl.BlockSpec(shape, lambda *_: (0,) * nd, pipeline_mode=pl.Buffered(1))


def _rms(x, g):
    ms = jnp.mean(x * x, axis=-1, keepdims=True)
    return x * lax.rsqrt(ms + EPS) * g


def _rel_bucket_np(dist):
    n = np.maximum(dist, 0)
    max_exact = NUM_BUCKETS // 2
    nf = np.maximum(n, 1).astype(np.float32)
    large = max_exact + (np.log(nf / np.float32(max_exact)) / np.float32(math.log(MAX_DISTANCE / max_exact))
                         * np.float32(NUM_BUCKETS - max_exact)).astype(np.int32)
    large = np.minimum(large, NUM_BUCKETS - 1)
    return np.where(n < max_exact, n, large).astype(np.int32)


def _bias_kernel(tab_ref, bp_ref, bs_ref, op_ref, os_ref):
    bp = bp_ref[...]
    bs = bs_ref[...]
    for hh in range(SWA_Q_HEADS):
        accp = jnp.zeros(bp.shape, F32)
        accs = jnp.zeros(bs.shape, F32)
        for k in range(NUM_BUCKETS):
            t = tab_ref[k * SWA_Q_HEADS + hh]
            accp = jnp.where(bp == k, t, accp)
            accs = jnp.where(bs == k, t, accs)
        h, g = divmod(hh, SWA_GROUP)
        op_ref[h, g * WINDOW:(g + 1) * WINDOW, :] = accp
        os_ref[hh * SUBLANES:(hh + 1) * SUBLANES, :] = accs


def _bias_tables(table, t_len):
    qi = np.arange(WINDOW)[:, None]
    kj = np.arange(2 * WINDOW)[None, :]
    bkt_p = _rel_bucket_np(WINDOW + qi - kj)
    bkt_s = _rel_bucket_np(np.arange(t_len)[:, None] + WINDOW - kj)
    return pl.pallas_call(
        _bias_kernel,
        out_shape=(jax.ShapeDtypeStruct((SWA_KV_HEADS, SWA_GROUP * WINDOW, 2 * WINDOW), F32),
                   jax.ShapeDtypeStruct((SWA_Q_HEADS * t_len, 2 * WINDOW), F32)),
        in_specs=[pl.BlockSpec(memory_space=pltpu.SMEM),
                  pl.BlockSpec(memory_space=pltpu.VMEM),
                  pl.BlockSpec(memory_space=pltpu.VMEM)],
        out_specs=(pl.BlockSpec(memory_space=pltpu.VMEM), pl.BlockSpec(memory_space=pltpu.VMEM)),
        name="bias",
    )(table.reshape(-1), jnp.asarray(bkt_p), jnp.asarray(bkt_s))


def _memkv_kernel(m_ref, g_ref, wk_ref, wv_ref, gk_ref, mk_ref, mv_ref, mkb_ref, mvb_ref):
    h = _rms(m_ref[...], g_ref[...]).astype(BF16)
    zk = jnp.dot(h, wk_ref[...], preferred_element_type=F32)
    zv = jnp.dot(h, wv_ref[...], preferred_element_type=F32)
    gk = gk_ref[...]
    for hx in range(X_HEADS):
        sl = slice(hx * HEAD_DIM, (hx + 1) * HEAD_DIM)
        mk = _rms(zk[:, sl], gk)
        mk_ref[:, sl] = mk
        mkb_ref[:, sl] = mk.astype(BF16)
    mv_ref[...] = zv
    mvb_ref[...] = zv.astype(BF16)


def _memory_kv(mem2d, g_mem, wk, wv, g_k_x):
    n = mem2d.shape[0]
    tile = MEM_TOKENS
    row = lambda i: (i, 0)
    out = lambda dt: jax.ShapeDtypeStruct((n, X_WIDTH), dt)
    return pl.pallas_call(
        _memkv_kernel,
        grid=(n // tile,),
        in_specs=[pl.BlockSpec((tile, D_MODEL), row), _resident((1, D_MODEL)),
                  _resident((D_MODEL, X_WIDTH)), _resident((D_MODEL, X_WIDTH)), _resident((1, HEAD_DIM))],
        out_specs=tuple(pl.BlockSpec((tile, X_WIDTH), row) for _ in range(4)),
        out_shape=(out(F32), out(F32), out(BF16), out(BF16)),
        compiler_params=_params("arbitrary"),
        name="memkv",
    )(mem2d, g_mem, wk, wv, g_k_x)


def _proj_kernel(x_ref, g_ref, w_ref, gq_ref, gk_ref, gx_ref,
                 b_ref, u_ref, q_ref, k_ref, v_ref, kb_ref, vb_ref, qx_ref):
    h = _rms(x_ref[...], g_ref[...]).astype(BF16)

    def seg(lo, width):
        return jnp.dot(h, w_ref[:, lo:lo + width], preferred_element_type=F32)

    b_ref[...] = seg(OFF_B, CONV_WIDTH)
    u_ref[...] = seg(OFF_C, CONV_WIDTH) * seg(OFF_H, CONV_WIDTH)

    def head_norm(z, g, n_heads, outs):
        for hh in range(n_heads):
            sl = slice(hh * HEAD_DIM, (hh + 1) * HEAD_DIM)
            y = _rms(z[:, sl], g)
            for o in outs:
                o[:, sl] = y.astype(o.dtype)

    head_norm(seg(OFF_Q, SWA_WIDTH), gq_ref[...], SWA_Q_HEADS, (q_ref,))
    head_norm(seg(OFF_K, SWA_KV_WIDTH), gk_ref[...], SWA_KV_HEADS, (k_ref, kb_ref))
    zv = seg(OFF_V, SWA_KV_WIDTH)
    v_ref[...] = zv
    vb_ref[...] = zv.astype(BF16)
    head_norm(seg(OFF_QX, X_WIDTH), gx_ref[...], X_HEADS, (qx_ref,))


def _projections(x2d, g_mix, w_in, g_q, g_k, g_qx, q_dtype):
    n = x2d.shape[0]
    tile = min(ROW_TILE, n)
    row = lambda i: (i, 0)
    widths = (CONV_WIDTH, CONV_WIDTH, SWA_WIDTH, SWA_KV_WIDTH, SWA_KV_WIDTH, SWA_KV_WIDTH, SWA_KV_WIDTH, X_WIDTH)
    dtypes = (F32, F32, q_dtype, F32, F32, BF16, BF16, q_dtype)
    return pl.pallas_call(
        _proj_kernel,
        grid=(n // tile,),
        in_specs=[pl.BlockSpec((tile, D_MODEL), row), _resident((1, D_MODEL)), _resident((D_MODEL, IN_WIDTH)),
                  _resident((1, HEAD_DIM)), _resident((1, HEAD_DIM)), _resident((1, HEAD_DIM))],
        out_specs=tuple(pl.BlockSpec((tile, w), row) for w in widths),
        out_shape=tuple(jax.ShapeDtypeStruct((n, w), dt) for w, dt in zip(widths, dtypes)),
        compiler_params=_params("arbitrary"),
        name="proj",
    )(x2d, g_mix, w_in, g_q, g_k, g_qx)


def _sink_softmax(s, sink_col):
    m = jnp.maximum(jnp.max(s, axis=-1, keepdims=True), sink_col)
    p = jnp.exp(s - m)
    den = jnp.sum(p, axis=-1, keepdims=True) + jnp.exp(sink_col - m)
    return p * (1.0 / den)


def _softmax(s):
    m = jnp.max(s, axis=-1, keepdims=True)
    p = jnp.exp(s - m)
    return p * (1.0 / jnp.sum(p, axis=-1, keepdims=True))


def _mix_prompt_kernel(sink_ref, q_ref, kc_ref, kp_ref, vc_ref, vp_ref, qx_ref, mk_ref, mv_ref,
                       b_ref, uc_ref, up_ref, cw_ref, bias_ref, o_ref):
    blk = pl.program_id(1)
    has_prev = blk > 0

    u = uc_ref[...]
    prev = jnp.where(has_prev, up_ref[...], 0.0)
    ext = jnp.concatenate([prev, u], axis=0)
    cw = cw_ref[...]
    conv = cw[0:1] * ext[SUBLANES - 2:SUBLANES - 2 + WINDOW]
    conv = conv + cw[1:2] * ext[SUBLANES - 1:SUBLANES - 1 + WINDOW]
    conv = conv + cw[2:3] * u
    o_ref[:, 0:CONV_WIDTH] = (b_ref[...] * conv).astype(o_ref.dtype)

    rows = SWA_GROUP * WINDOW
    qi = lax.broadcasted_iota(jnp.int32, (rows, 2 * WINDOW), 0) & (WINDOW - 1)
    kj = lax.broadcasted_iota(jnp.int32, (rows, 2 * WINDOW), 1)
    dist = WINDOW + qi - kj
    first_key = jnp.where(has_prev, 0, WINDOW)
    valid = (dist >= 0) & (dist < WINDOW) & (kj >= first_key)
    for h in range(SWA_KV_HEADS):
        ksl = slice(h * HEAD_DIM, (h + 1) * HEAD_DIM)
        qh = jnp.concatenate(
            [q_ref[:, (h * SWA_GROUP + g) * HEAD_DIM:(h * SWA_GROUP + g + 1) * HEAD_DIM] for g in range(SWA_GROUP)],
            axis=0)
        k_all = jnp.concatenate([kp_ref[:, ksl], kc_ref[:, ksl]], axis=0)
        v_all = jnp.concatenate([vp_ref[:, ksl], vc_ref[:, ksl]], axis=0)
        s = lax.dot_general(qh, k_all, NT_DIMS, preferred_element_type=F32) * SCALE
        s = jnp.where(valid, s + bias_ref[h], NEG)
        sink_col = jnp.concatenate(
            [jnp.full((WINDOW, 1), sink_ref[h * SWA_GROUP + g], F32) for g in range(SWA_GROUP)], axis=0)
        w = _sink_softmax(s, sink_col).astype(BF16)
        o = jnp.dot(w, v_all, preferred_element_type=F32)
        for g in range(SWA_GROUP):
            col = CONV_WIDTH + (h * SWA_GROUP + g) * HEAD_DIM
            o_ref[:, col:col + HEAD_DIM] = o[g * WINDOW:(g + 1) * WINDOW].astype(o_ref.dtype)

    for hx in range(X_HEADS):
        sl = slice(hx * HEAD_DIM, (hx + 1) * HEAD_DIM)
        s = lax.dot_general(qx_ref[:, sl], mk_ref[:, sl], NT_DIMS, preferred_element_type=F32) * SCALE
        w = _softmax(s).astype(BF16)
        col = CONV_WIDTH + SWA_WIDTH + hx * HEAD_DIM
        o_ref[:, col:col + HEAD_DIM] = jnp.dot(w, mv_ref[:, sl], preferred_element_type=F32).astype(o_ref.dtype)


def _mix_prompt(sinks, q, kb, vb, qx, mkb, mvb, b, u, conv_w, bias_p, bsz, seq):
    nb = seq // WINDOW
    cur = lambda bi, i: (bi * nb + i, 0)
    prv = lambda bi, i: (bi * nb + jnp.maximum(i - 1, 0), 0)
    prv8 = lambda bi, i: (jnp.maximum((bi * nb + i) * (WINDOW // SUBLANES) - 1, 0), 0)
    per_b = lambda bi, i: (bi, 0)
    return pl.pallas_call(
        _mix_prompt_kernel,
        grid=(bsz, nb),
        in_specs=[pl.BlockSpec(memory_space=pltpu.SMEM),
                  pl.BlockSpec((WINDOW, SWA_WIDTH), cur),
                  pl.BlockSpec((WINDOW, SWA_KV_WIDTH), cur), pl.BlockSpec((WINDOW, SWA_KV_WIDTH), prv),
                  pl.BlockSpec((WINDOW, SWA_KV_WIDTH), cur), pl.BlockSpec((WINDOW, SWA_KV_WIDTH), prv),
                  pl.BlockSpec((WINDOW, X_WIDTH), cur),
                  pl.BlockSpec((MEM_TOKENS, X_WIDTH), per_b), pl.BlockSpec((MEM_TOKENS, X_WIDTH), per_b),
                  pl.BlockSpec((WINDOW, CONV_WIDTH), cur), pl.BlockSpec((WINDOW, CONV_WIDTH), cur),
                  pl.BlockSpec((SUBLANES, CONV_WIDTH), prv8),
                  _resident((CONV_K, CONV_WIDTH)),
                  _resident((SWA_KV_HEADS, SWA_GROUP * WINDOW, 2 * WINDOW))],
        out_specs=pl.BlockSpec((WINDOW, D_MODEL), cur),
        out_shape=jax.ShapeDtypeStruct((bsz * seq, D_MODEL), BF16),
        compiler_params=_params("arbitrary", "arbitrary"),
        name="mix_prompt",
    )(sinks, q, kb, kb, vb, vb, qx, mkb, mvb, b, u, u, conv_w, bias_p)


def _mix_sample_kernel(t_len, sink_ref, q_ref, kn_ref, vn_ref, ck_ref, cv_ref, qx_ref, cmk_ref, cmv_ref,
                       b_ref, u_ref, cc_ref, cw_ref, bias_ref, o_ref, sk_ref, sv_ref):
    nb = ck_ref.shape[0]
    n_keys = 2 * WINDOW
    pad_rows = n_keys - WINDOW - t_len

    sk_ref[:, 0:WINDOW - t_len, :] = ck_ref[:, t_len:WINDOW, :]
    sv_ref[:, 0:WINDOW - t_len, :] = cv_ref[:, t_len:WINDOW, :]
    sk_ref[:, WINDOW - t_len:WINDOW, :] = kn_ref[...].reshape(nb, t_len, SWA_KV_WIDTH)
    sv_ref[:, WINDOW - t_len:WINDOW, :] = vn_ref[...].reshape(nb, t_len, SWA_KV_WIDTH)

    rows = SWA_Q_HEADS * t_len
    tq = lax.broadcasted_iota(jnp.int32, (rows, n_keys), 0) & (t_len - 1)
    kj = lax.broadcasted_iota(jnp.int32, (rows, n_keys), 1)
    dist = tq + WINDOW - kj
    valid = (dist >= 0) & (dist < WINDOW)
    bias = bias_ref[...]
    sink_col = jnp.concatenate([jnp.full((t_len, 1), sink_ref[hh], F32) for hh in range(SWA_Q_HEADS)], axis=0)
    cw = cw_ref[...]
    trow = lax.broadcasted_iota(jnp.int32, (t_len, CONV_WIDTH), 0)
    xrow_head = lax.broadcasted_iota(jnp.int32, (X_HEADS * t_len, X_WIDTH), 0) >> int(math.log2(t_len))
    xcol_head = lax.broadcasted_iota(jnp.int32, (X_HEADS * t_len, X_WIDTH), 1) >> int(math.log2(HEAD_DIM))
    x_diag = xrow_head == xcol_head
    zeros_kv = jnp.zeros((pad_rows, SWA_KV_WIDTH), F32)
    zeros_hd = jnp.zeros((t_len, HEAD_DIM), F32)

    def body(bi, carry):
        r0 = pl.multiple_of(bi * t_len, t_len)
        rsl = pl.ds(r0, t_len)

        u = u_ref[rsl, :]
        cc = cc_ref[bi]
        c1 = jnp.broadcast_to(cc[1:2], u.shape)
        c0 = jnp.broadcast_to(cc[0:1], u.shape)
        u_m1 = jnp.where(trow >= 1, pltpu.roll(u, 1, 0), c1)
        u_m2 = jnp.where(trow >= 2, pltpu.roll(u, 2, 0), jnp.where(trow == 1, c1, c0))
        conv = cw[0:1] * u_m2
        conv = conv + cw[1:2] * u_m1
        conv = conv + cw[2:3] * u
        o_ref[rsl, 0:CONV_WIDTH] = (b_ref[rsl, :] * conv).astype(o_ref.dtype)

        qb = q_ref[rsl, :]
        q_rows = []
        for hh in range(SWA_Q_HEADS):
            piece = qb[:, hh * HEAD_DIM:(hh + 1) * HEAD_DIM]
            parts = [zeros_hd] * SWA_KV_HEADS
            parts[hh // SWA_GROUP] = piece
            q_rows.append(jnp.concatenate(parts, axis=1))
        q_bd = jnp.concatenate(q_rows, axis=0).astype(BF16)
        k_all = jnp.concatenate([ck_ref[bi], kn_ref[rsl, :], zeros_kv], axis=0).astype(BF16)
        v_all = jnp.concatenate([cv_ref[bi], vn_ref[rsl, :], zeros_kv], axis=0).astype(BF16)
        s = lax.dot_general(q_bd, k_all, NT_DIMS, preferred_element_type=F32) * SCALE
        s = jnp.where(valid, s + bias, NEG)
        w = _sink_softmax(s, sink_col).astype(BF16)
        o = jnp.dot(w, v_all, preferred_element_type=F32)
        for hh in range(SWA_Q_HEADS):
            h = hh // SWA_GROUP
            col = CONV_WIDTH + hh * HEAD_DIM
            o_ref[rsl, col:col + HEAD_DIM] = (
                o[hh * t_len:(hh + 1) * t_len, h * HEAD_DIM:(h + 1) * HEAD_DIM].astype(o_ref.dtype))

        qxb = qx_ref[rsl, :]
        qx_bd = jnp.where(x_diag, jnp.concatenate([qxb] * X_HEADS, axis=0), 0.0).astype(BF16)
        sx = lax.dot_general(qx_bd, cmk_ref[bi].astype(BF16), NT_DIMS, preferred_element_type=F32) * SCALE
        wx = _softmax(sx).astype(BF16)
        ox = jnp.dot(wx, cmv_ref[bi].astype(BF16), preferred_element_type=F32)
        for hx in range(X_HEADS):
            col = CONV_WIDTH + SWA_WIDTH + hx * HEAD_DIM
            o_ref[rsl, col:col + HEAD_DIM] = (
                ox[hx * t_len:(hx + 1) * t_len, hx * HEAD_DIM:(hx + 1) * HEAD_DIM].astype(o_ref.dtype))
        return carry

    lax.fori_loop(0, nb, body, 0)


def _mix_sample(sinks, q, k, v, cache_k, cache_v, qx, cache_mk, cache_mv, b, u, cache_conv, conv_w, bias_s,
                bsz, t_len):
    assert t_len == SUBLANES and bsz % SAMPLE_BATCH_TILE == 0
    nb = SAMPLE_BATCH_TILE
    rows = nb * t_len
    row = lambda i: (i, 0)
    b3 = lambda i: (i, 0, 0)
    return pl.pallas_call(
        functools.partial(_mix_sample_kernel, t_len),
        grid=(bsz // nb,),
        in_specs=[pl.BlockSpec(memory_space=pltpu.SMEM),
                  pl.BlockSpec((rows, SWA_WIDTH), row),
                  pl.BlockSpec((rows, SWA_KV_WIDTH), row), pl.BlockSpec((rows, SWA_KV_WIDTH), row),
                  pl.BlockSpec((nb, WINDOW, SWA_KV_WIDTH), b3), pl.BlockSpec((nb, WINDOW, SWA_KV_WIDTH), b3),
                  pl.BlockSpec((rows, X_WIDTH), row),
                  pl.BlockSpec((nb, MEM_TOKENS, X_WIDTH), b3), pl.BlockSpec((nb, MEM_TOKENS, X_WIDTH), b3),
                  pl.BlockSpec((rows, CONV_WIDTH), row), pl.BlockSpec((rows, CONV_WIDTH), row),
                  pl.BlockSpec((nb, CONV_K - 1, CONV_WIDTH), b3),
                  _resident((CONV_K, CONV_WIDTH)),
                  _resident((SWA_Q_HEADS * t_len, 2 * WINDOW))],
        out_specs=(pl.BlockSpec((rows, D_MODEL), row),
                   pl.BlockSpec((nb, WINDOW, SWA_KV_WIDTH), b3), pl.BlockSpec((nb, WINDOW, SWA_KV_WIDTH), b3)),
        out_shape=(jax.ShapeDtypeStruct((bsz * t_len, D_MODEL), F32),
                   jax.ShapeDtypeStruct((bsz, WINDOW, SWA_KV_WIDTH), F32),
                   jax.ShapeDtypeStruct((bsz, WINDOW, SWA_KV_WIDTH), F32)),
        compiler_params=_params("arbitrary"),
        name="mix_sample",
    )(sinks, q, k, v, cache_k, cache_v, qx, cache_mk, cache_mv, b, u, cache_conv, conv_w, bias_s)


def _outproj_kernel(x_ref, y_ref, w_ref, g_ref, x1_ref, h_ref):
    x1 = x_ref[...] + jnp.dot(y_ref[...].astype(BF16), w_ref[...], preferred_element_type=F32)
    x1_ref[...] = x1
    h_ref[...] = _rms(x1, g_ref[...]).astype(BF16)


def _out_projection(x2d, y, w_out, g_mlp):
    n = x2d.shape[0]
    tile = min(ROW_TILE, n)
    row = lambda i: (i, 0)
    return pl.pallas_call(
        _outproj_kernel,
        grid=(n // tile,),
        in_specs=[pl.BlockSpec((tile, D_MODEL), row), pl.BlockSpec((tile, D_MODEL), row),
                  _resident((D_MODEL, D_MODEL)), _resident((1, D_MODEL))],
        out_specs=(pl.BlockSpec((tile, D_MODEL), row), pl.BlockSpec((tile, D_MODEL), row)),
        out_shape=(jax.ShapeDtypeStruct((n, D_MODEL), F32), jax.ShapeDtypeStruct((n, D_MODEL), BF16)),
        compiler_params=_params("arbitrary"),
        name="outproj",
    )(x2d, y, w_out, g_mlp)


def _mlp_kernel(x1_ref, h_ref, wu_ref, wd_ref, o_ref):
    @pl.when(pl.program_id(1) == 0)
    def _():
        o_ref[...] = x1_ref[...]

    a = jnp.maximum(jnp.dot(h_ref[...], wu_ref[...], preferred_element_type=F32), 0.0)
    o_ref[...] += jnp.dot((a * a).astype(BF16), wd_ref[...], preferred_element_type=F32)


def _mlp(x1, h, w_up, w_down):
    n = x1.shape[0]
    tile = min(ROW_TILE, n)
    return pl.pallas_call(
        _mlp_kernel,
        grid=(n // tile, D_FF // FF_TILE),
        in_specs=[pl.BlockSpec((tile, D_MODEL), lambda i, j: (i, 0)),
                  pl.BlockSpec((tile, D_MODEL), lambda i, j: (i, 0)),
                  pl.BlockSpec((D_MODEL, FF_TILE), lambda i, j: (0, j)),
                  pl.BlockSpec((FF_TILE, D_MODEL), lambda i, j: (j, 0))],
        out_specs=pl.BlockSpec((tile, D_MODEL), lambda i, j: (i, 0)),
        out_shape=jax.ShapeDtypeStruct((n, D_MODEL), F32),
        compiler_params=_params("arbitrary", "arbitrary"),
        name="mlp",
    )(x1, h, w_up, w_down)


def kernel(x_prompt, x_sample, mem_prompt, cache_conv, cache_swa_k, cache_swa_v, cache_mem_k, cache_mem_v,
           rel_bias_table, g_mix, w_in, conv_w, g_q_swa, g_k_swa, sinks, g_q_x, g_k_x, g_mem,
           w_mem_k, w_mem_v, w_out, g_mlp, w_up, w_down):
    depth = w_in.shape[0]
    bsz, seq, _ = x_prompt.shape
    dbsz, t_len, _ = x_sample.shape
    xp = x_prompt.reshape(bsz * seq, D_MODEL)
    xs = x_sample.reshape(dbsz * t_len, D_MODEL)
    mem2d = mem_prompt.reshape(bsz * MEM_TOKENS, D_MODEL)
    bias_p, bias_s = _bias_tables(rel_bias_table, t_len)

    outs = [[] for _ in range(8)]
    for l in range(depth):
        vec = lambda a: a[l].reshape(1, -1)
        wi, wo = w_in[l].astype(BF16), w_out[l].astype(BF16)
        wu, wd = w_up[l].astype(BF16), w_down[l].astype(BF16)
        wk, wv = w_mem_k[l].astype(BF16), w_mem_v[l].astype(BF16)
        proj = functools.partial(_projections, g_mix=vec(g_mix), w_in=wi, g_q=vec(g_q_swa), g_k=vec(g_k_swa),
                                 g_qx=vec(g_q_x))

        mk, mv, mkb, mvb = _memory_kv(mem2d, vec(g_mem), wk, wv, vec(g_k_x))
        b, u, q, k, v, kb, vb, qx = proj(xp, q_dtype=BF16)
        y = _mix_prompt(sinks[l], q, kb, vb, qx, mkb, mvb, b, u, conv_w[l], bias_p, bsz, seq)
        x1, h = _out_projection(xp, y, wo, vec(g_mlp))
        xp = _mlp(x1, h, wu, wd)
        outs[0].append(u.reshape(bsz, seq, CONV_WIDTH)[:, seq - (CONV_K - 1):])
        outs[1].append(k.reshape(bsz, seq, SWA_KV_HEADS, HEAD_DIM)[:, seq - WINDOW:])
        outs[2].append(v.reshape(bsz, seq, SWA_KV_HEADS, HEAD_DIM)[:, seq - WINDOW:])
        outs[3].append(mk.reshape(bsz, MEM_TOKENS, X_HEADS, HEAD_DIM))
        outs[4].append(mv.reshape(bsz, MEM_TOKENS, X_HEADS, HEAD_DIM))

        b, u, q, k, v, _, _, qx = proj(xs, q_dtype=F32)
        y, sk, sv = _mix_sample(
            sinks[l], q, k, v,
            cache_swa_k[l].reshape(dbsz, WINDOW, SWA_KV_WIDTH), cache_swa_v[l].reshape(dbsz, WINDOW, SWA_KV_WIDTH),
            qx, cache_mem_k[l].reshape(dbsz, MEM_TOKENS, X_WIDTH), cache_mem_v[l].reshape(dbsz, MEM_TOKENS, X_WIDTH),
            b, u, cache_conv[l], conv_w[l], bias_s, dbsz, t_len)
        x1, h = _out_projection(xs, y, wo, vec(g_mlp))
        xs = _mlp(x1, h, wu, wd)
        outs[5].append(u.reshape(dbsz, t_len, CONV_WIDTH)[:, t_len - (CONV_K - 1):])
        outs[6].append(sk.reshape(dbsz, WINDOW, SWA_KV_HEADS, HEAD_DIM))
        outs[7].append(sv.reshape(dbsz, WINDOW, SWA_KV_HEADS, HEAD_DIM))

    return (xp.reshape(bsz, seq, D_MODEL), xs.reshape(dbsz, t_len, D_MODEL)) + tuple(jnp.stack(o) for o in outs)
```

```python
import functools
import math

import numpy as np
import jax
import jax.numpy as jnp
from jax import lax
from jax.experimental import pallas as pl
from jax.experimental.pallas import tpu as pltpu

D_MODEL = 2048
HEAD_DIM = 128
SWA_Q_HEADS = 8
SWA_KV_HEADS = 2
SWA_GROUP = SWA_Q_HEADS // SWA_KV_HEADS
SWA_WIDTH = SWA_Q_HEADS * HEAD_DIM
SWA_KV_WIDTH = SWA_KV_HEADS * HEAD_DIM
X_HEADS = 4
X_WIDTH = X_HEADS * HEAD_DIM
CONV_WIDTH = D_MODEL - SWA_WIDTH - X_WIDTH
CONV_K = 3
WINDOW = 128
NUM_BUCKETS = 32
MAX_DISTANCE = WINDOW
MEM_TOKENS = 256
D_FF = 4 * D_MODEL
EPS = 1e-6
NEG = -1e30
SCALE = HEAD_DIM ** -0.5

OFF_B = 0
OFF_C = CONV_WIDTH
OFF_H = 2 * CONV_WIDTH
OFF_Q = 3 * CONV_WIDTH
OFF_K = OFF_Q + SWA_WIDTH
OFF_V = OFF_K + SWA_KV_WIDTH
OFF_QX = OFF_V + SWA_KV_WIDTH
IN_WIDTH = OFF_QX + X_WIDTH

VMEM_LIMIT_V7X = 56 * 1024 * 1024
SUBLANES = 8

ROW_TILE = 512
FF_TILE = 1024
SAMPLE_BATCH_TILE = 8

BF16 = jnp.bfloat16
F32 = jnp.float32
NT_DIMS = (((1,), (1,)), ((), ()))


def _params(*sem):
    return pltpu.CompilerParams(dimension_semantics=sem, vmem_limit_bytes=VMEM_LIMIT_V7X)


def _resident(shape):
    nd = len(shape)
    return pl.BlockSpec(shape, lambda *_: (0,) * nd, pipeline_mode=pl.Buffered(1))


def _rms(x, g):
    ms = jnp.mean(x * x, axis=-1, keepdims=True)
    return x * lax.rsqrt(ms + EPS) * g


def _rel_bucket_np(dist):
    n = np.maximum(dist, 0)
    max_exact = NUM_BUCKETS // 2
    nf = np.maximum(n, 1).astype(np.float32)
    large = max_exact + (np.log(nf / np.float32(max_exact)) / np.float32(math.log(MAX_DISTANCE / max_exact))
                         * np.float32(NUM_BUCKETS - max_exact)).astype(np.int32)
    large = np.minimum(large, NUM_BUCKETS - 1)
    return np.where(n < max_exact, n, large).astype(np.int32)


SAMPLE_KEY_ROWS = 3 * WINDOW


def _sample_key_index(t_len):
    c = np.arange(SAMPLE_KEY_ROWS)
    n_cache = SWA_KV_HEADS * WINDOW
    assert n_cache + SWA_KV_HEADS * t_len <= SAMPLE_KEY_ROWS
    pos = np.where(c < n_cache, c // SWA_KV_HEADS, WINDOW + (c - n_cache) // SWA_KV_HEADS)
    return pos, c % SWA_KV_HEADS


def _bias_kernel(tab_ref, bp_ref, bs_ref, op_ref, os_ref):
    bp = bp_ref[...]
    bs = bs_ref[...]
    for hh in range(SWA_Q_HEADS):
        accp = jnp.zeros(bp.shape, F32)
        accs = jnp.zeros(bs.shape, F32)
        for k in range(NUM_BUCKETS):
            t = tab_ref[k * SWA_Q_HEADS + hh]
            accp = jnp.where(bp == k, t, accp)
            accs = jnp.where(bs == k, t, accs)
        h, g = divmod(hh, SWA_GROUP)
        op_ref[h, g * WINDOW:(g + 1) * WINDOW, :] = accp
        os_ref[hh * SUBLANES:(hh + 1) * SUBLANES, :] = accs


def _bias_tables(table, t_len):
    qi = np.arange(WINDOW)[:, None]
    kj = np.arange(2 * WINDOW)[None, :]
    bkt_p = _rel_bucket_np(WINDOW + qi - kj)
    key_pos, _ = _sample_key_index(t_len)
    bkt_s = _rel_bucket_np(np.arange(t_len)[:, None] + WINDOW - key_pos[None, :])
    return pl.pallas_call(
        _bias_kernel,
        out_shape=(jax.ShapeDtypeStruct((SWA_KV_HEADS, SWA_GROUP * WINDOW, 2 * WINDOW), F32),
                   jax.ShapeDtypeStruct((SWA_Q_HEADS * t_len, SAMPLE_KEY_ROWS), F32)),
        in_specs=[pl.BlockSpec(memory_space=pltpu.SMEM),
                  pl.BlockSpec(memory_space=pltpu.VMEM),
                  pl.BlockSpec(memory_space=pltpu.VMEM)],
        out_specs=(pl.BlockSpec(memory_space=pltpu.VMEM), pl.BlockSpec(memory_space=pltpu.VMEM)),
        name="bias",
    )(table.reshape(-1), jnp.asarray(bkt_p), jnp.asarray(bkt_s))


def _memkv_kernel(m_ref, g_ref, wk_ref, wv_ref, gk_ref, mk_ref, mv_ref, mkb_ref, mvb_ref):
    h = _rms(m_ref[...], g_ref[...]).astype(BF16)
    zk = jnp.dot(h, wk_ref[...], preferred_element_type=F32)
    zv = jnp.dot(h, wv_ref[...], preferred_element_type=F32)
    gk = gk_ref[...]
    for hx in range(X_HEADS):
        sl = slice(hx * HEAD_DIM, (hx + 1) * HEAD_DIM)
        mk = _rms(zk[:, sl], gk)
        mk_ref[:, sl] = mk
        mkb_ref[:, sl] = mk.astype(BF16)
    mv_ref[...] = zv
    mvb_ref[...] = zv.astype(BF16)


def _memory_kv(mem2d, g_mem, wk, wv, g_k_x):
    n = mem2d.shape[0]
    tile = MEM_TOKENS
    row = lambda i: (i, 0)
    out = lambda dt: jax.ShapeDtypeStruct((n, X_WIDTH), dt)
    return pl.pallas_call(
        _memkv_kernel,
        grid=(n // tile,),
        in_specs=[pl.BlockSpec((tile, D_MODEL), row), _resident((1, D_MODEL)),
                  _resident((D_MODEL, X_WIDTH)), _resident((D_MODEL, X_WIDTH)), _resident((1, HEAD_DIM))],
        out_specs=tuple(pl.BlockSpec((tile, X_WIDTH), row) for _ in range(4)),
        out_shape=(out(F32), out(F32), out(BF16), out(BF16)),
        compiler_params=_params("arbitrary"),
        name="memkv",
    )(mem2d, g_mem, wk, wv, g_k_x)


def _proj_kernel(x_ref, g_ref, w_ref, gq_ref, gk_ref, gx_ref,
                 b_ref, u_ref, q_ref, k_ref, v_ref, kb_ref, vb_ref, qx_ref):
    h = _rms(x_ref[...], g_ref[...]).astype(BF16)

    def seg(lo, width):
        return jnp.dot(h, w_ref[:, lo:lo + width], preferred_element_type=F32)

    b_ref[...] = seg(OFF_B, CONV_WIDTH)
    u_ref[...] = seg(OFF_C, CONV_WIDTH) * seg(OFF_H, CONV_WIDTH)

    def head_norm(z, g, n_heads, outs):
        for hh in range(n_heads):
            sl = slice(hh * HEAD_DIM, (hh + 1) * HEAD_DIM)
            y = _rms(z[:, sl], g)
            for o in outs:
                o[:, sl] = y.astype(o.dtype)

    head_norm(seg(OFF_Q, SWA_WIDTH), gq_ref[...], SWA_Q_HEADS, (q_ref,))
    head_norm(seg(OFF_K, SWA_KV_WIDTH), gk_ref[...], SWA_KV_HEADS, (k_ref, kb_ref))
    zv = seg(OFF_V, SWA_KV_WIDTH)
    v_ref[...] = zv
    vb_ref[...] = zv.astype(BF16)
    head_norm(seg(OFF_QX, X_WIDTH), gx_ref[...], X_HEADS, (qx_ref,))


def _projections(x2d, g_mix, w_in, g_q, g_k, g_qx, q_dtype):
    n = x2d.shape[0]
    tile = min(ROW_TILE, n)
    row = lambda i: (i, 0)
    widths = (CONV_WIDTH, CONV_WIDTH, SWA_WIDTH, SWA_KV_WIDTH, SWA_KV_WIDTH, SWA_KV_WIDTH, SWA_KV_WIDTH, X_WIDTH)
    dtypes = (F32, F32, q_dtype, F32, F32, BF16, BF16, q_dtype)
    return pl.pallas_call(
        _proj_kernel,
        grid=(n // tile,),
        in_specs=[pl.BlockSpec((tile, D_MODEL), row), _resident((1, D_MODEL)), _resident((D_MODEL, IN_WIDTH)),
                  _resident((1, HEAD_DIM)), _resident((1, HEAD_DIM)), _resident((1, HEAD_DIM))],
        out_specs=tuple(pl.BlockSpec((tile, w), row) for w in widths),
        out_shape=tuple(jax.ShapeDtypeStruct((n, w), dt) for w, dt in zip(widths, dtypes)),
        compiler_params=_params("arbitrary"),
        name="proj",
    )(x2d, g_mix, w_in, g_q, g_k, g_qx)


def _sink_softmax(s, sink_col):
    m = jnp.maximum(jnp.max(s, axis=-1, keepdims=True), sink_col)
    p = jnp.exp(s - m)
    den = jnp.sum(p, axis=-1, keepdims=True) + jnp.exp(sink_col - m)
    return p * (1.0 / den)


def _softmax(s):
    m = jnp.max(s, axis=-1, keepdims=True)
    p = jnp.exp(s - m)
    return p * (1.0 / jnp.sum(p, axis=-1, keepdims=True))


def _mix_prompt_kernel(sink_ref, q_ref, kc_ref, kp_ref, vc_ref, vp_ref, qx_ref, mk_ref, mv_ref,
                       b_ref, uc_ref, up_ref, cw_ref, bias_ref, o_ref):
    blk = pl.program_id(1)
    has_prev = blk > 0

    u = uc_ref[...]
    prev = jnp.where(has_prev, up_ref[...], 0.0)
    ext = jnp.concatenate([prev, u], axis=0)
    cw = cw_ref[...]
    conv = cw[0:1] * ext[SUBLANES - 2:SUBLANES - 2 + WINDOW]
    conv = conv + cw[1:2] * ext[SUBLANES - 1:SUBLANES - 1 + WINDOW]
    conv = conv + cw[2:3] * u
    o_ref[:, 0:CONV_WIDTH] = (b_ref[...] * conv).astype(o_ref.dtype)

    rows = SWA_GROUP * WINDOW
    qi = lax.broadcasted_iota(jnp.int32, (rows, 2 * WINDOW), 0) & (WINDOW - 1)
    kj = lax.broadcasted_iota(jnp.int32, (rows, 2 * WINDOW), 1)
    dist = WINDOW + qi - kj
    first_key = jnp.where(has_prev, 0, WINDOW)
    valid = (dist >= 0) & (dist < WINDOW) & (kj >= first_key)
    for h in range(SWA_KV_HEADS):
        ksl = slice(h * HEAD_DIM, (h + 1) * HEAD_DIM)
        qh = jnp.concatenate(
            [q_ref[:, (h * SWA_GROUP + g) * HEAD_DIM:(h * SWA_GROUP + g + 1) * HEAD_DIM] for g in range(SWA_GROUP)],
            axis=0)
        k_all = jnp.concatenate([kp_ref[:, ksl], kc_ref[:, ksl]], axis=0)
        v_all = jnp.concatenate([vp_ref[:, ksl], vc_ref[:, ksl]], axis=0)
        s = lax.dot_general(qh, k_all, NT_DIMS, preferred_element_type=F32) * SCALE
        s = jnp.where(valid, s + bias_ref[h], NEG)
        sink_col = jnp.concatenate(
            [jnp.full((WINDOW, 1), sink_ref[h * SWA_GROUP + g], F32) for g in range(SWA_GROUP)], axis=0)
        w = _sink_softmax(s, sink_col).astype(BF16)
        o = jnp.dot(w, v_all, preferred_element_type=F32)
        for g in range(SWA_GROUP):
            col = CONV_WIDTH + (h * SWA_GROUP + g) * HEAD_DIM
            o_ref[:, col:col + HEAD_DIM] = o[g * WINDOW:(g + 1) * WINDOW].astype(o_ref.dtype)

    for hx in range(X_HEADS):
        sl = slice(hx * HEAD_DIM, (hx + 1) * HEAD_DIM)
        s = lax.dot_general(qx_ref[:, sl], mk_ref[:, sl], NT_DIMS, preferred_element_type=F32) * SCALE
        w = _softmax(s).astype(BF16)
        col = CONV_WIDTH + SWA_WIDTH + hx * HEAD_DIM
        o_ref[:, col:col + HEAD_DIM] = jnp.dot(w, mv_ref[:, sl], preferred_element_type=F32).astype(o_ref.dtype)


def _mix_prompt(sinks, q, kb, vb, qx, mkb, mvb, b, u, conv_w, bias_p, bsz, seq):
    nb = seq // WINDOW
    cur = lambda bi, i: (bi * nb + i, 0)
    prv = lambda bi, i: (bi * nb + jnp.maximum(i - 1, 0), 0)
    prv8 = lambda bi, i: (jnp.maximum((bi * nb + i) * (WINDOW // SUBLANES) - 1, 0), 0)
    per_b = lambda bi, i: (bi, 0)
    return pl.pallas_call(
        _mix_prompt_kernel,
        grid=(bsz, nb),
        in_specs=[pl.BlockSpec(memory_space=pltpu.SMEM),
                  pl.BlockSpec((WINDOW, SWA_WIDTH), cur),
                  pl.BlockSpec((WINDOW, SWA_KV_WIDTH), cur), pl.BlockSpec((WINDOW, SWA_KV_WIDTH), prv),
                  pl.BlockSpec((WINDOW, SWA_KV_WIDTH), cur), pl.BlockSpec((WINDOW, SWA_KV_WIDTH), prv),
                  pl.BlockSpec((WINDOW, X_WIDTH), cur),
                  pl.BlockSpec((MEM_TOKENS, X_WIDTH), per_b), pl.BlockSpec((MEM_TOKENS, X_WIDTH), per_b),
                  pl.BlockSpec((WINDOW, CONV_WIDTH), cur), pl.BlockSpec((WINDOW, CONV_WIDTH), cur),
                  pl.BlockSpec((SUBLANES, CONV_WIDTH), prv8),
                  _resident((CONV_K, CONV_WIDTH)),
                  _resident((SWA_KV_HEADS, SWA_GROUP * WINDOW, 2 * WINDOW))],
        out_specs=pl.BlockSpec((WINDOW, D_MODEL), cur),
        out_shape=jax.ShapeDtypeStruct((bsz * seq, D_MODEL), BF16),
        compiler_params=_params("arbitrary", "arbitrary"),
        name="mix_prompt",
    )(sinks, q, kb, kb, vb, vb, qx, mkb, mvb, b, u, u, conv_w, bias_p)


def _mix_sample_kernel(t_len, sink_ref, q_ref, kn_ref, vn_ref, ck_ref, cv_ref, qx_ref, cmk_ref, cmv_ref,
                       b_ref, u_ref, cc_ref, cw_ref, bias_ref, o_ref, sk_ref, sv_ref):
    nb = cc_ref.shape[0]
    n_cache = SWA_KV_HEADS * WINDOW
    n_new = SWA_KV_HEADS * t_len
    n_mem = X_HEADS * MEM_TOKENS
    log_t = int(math.log2(t_len))

    rows = SWA_Q_HEADS * t_len
    r = lax.broadcasted_iota(jnp.int32, (rows, SAMPLE_KEY_ROWS), 0)
    c = lax.broadcasted_iota(jnp.int32, (rows, SAMPLE_KEY_ROWS), 1)
    key_pos = jnp.where(c < n_cache, c >> 1, WINDOW + ((c - n_cache) >> 1))
    dist = (r & (t_len - 1)) + WINDOW - key_pos
    valid = (dist >= 0) & (dist < WINDOW) & ((c & (SWA_KV_HEADS - 1)) == (r >> int(math.log2(SWA_GROUP * t_len))))
    bias = bias_ref[...]
    sink_col = jnp.concatenate([jnp.full((t_len, 1), sink_ref[hh], F32) for hh in range(SWA_Q_HEADS)], axis=0)
    xr = lax.broadcasted_iota(jnp.int32, (X_HEADS * t_len, n_mem), 0)
    xc = lax.broadcasted_iota(jnp.int32, (X_HEADS * t_len, n_mem), 1)
    x_valid = (xc & (X_HEADS - 1)) == (xr >> log_t)
    cw = cw_ref[...]
    trow = lax.broadcasted_iota(jnp.int32, (t_len, CONV_WIDTH), 0)
    zeros_pad = jnp.zeros((SAMPLE_KEY_ROWS - n_cache - n_new, HEAD_DIM), F32)

    def body(bi, carry):
        rsl = pl.ds(pl.multiple_of(bi * t_len, t_len), t_len)
        c0 = pl.multiple_of(bi * n_cache, n_cache)
        n0 = pl.multiple_of(bi * n_new, n_new)
        m0 = pl.multiple_of(bi * n_mem, n_mem)

        for dst, cache, new in ((sk_ref, ck_ref, kn_ref), (sv_ref, cv_ref, vn_ref)):
            dst[pl.ds(c0, n_cache - n_new), :] = cache[pl.ds(c0 + n_new, n_cache - n_new), :]
            dst[pl.ds(c0 + (n_cache - n_new), n_new), :] = new[pl.ds(n0, n_new), :]

        u = u_ref[rsl, :]
        cc = cc_ref[bi]
        cc1 = jnp.broadcast_to(cc[1:2], u.shape)
        cc0 = jnp.broadcast_to(cc[0:1], u.shape)
        u_m1 = jnp.where(trow >= 1, pltpu.roll(u, 1, 0), cc1)
        u_m2 = jnp.where(trow >= 2, pltpu.roll(u, 2, 0), jnp.where(trow == 1, cc1, cc0))
        conv = cw[0:1] * u_m2
        conv = conv + cw[1:2] * u_m1
        conv = conv + cw[2:3] * u
        o_ref[rsl, 0:CONV_WIDTH] = (b_ref[rsl, :] * conv).astype(o_ref.dtype)

        qb = q_ref[rsl, :]
        q_rows = jnp.concatenate([qb[:, hh * HEAD_DIM:(hh + 1) * HEAD_DIM] for hh in range(SWA_Q_HEADS)], axis=0)
        k_all = jnp.concatenate([ck_ref[pl.ds(c0, n_cache), :], kn_ref[pl.ds(n0, n_new), :], zeros_pad], axis=0)
        v_all = jnp.concatenate([cv_ref[pl.ds(c0, n_cache), :], vn_ref[pl.ds(n0, n_new), :], zeros_pad], axis=0)
        s = lax.dot_general(q_rows.astype(BF16), k_all.astype(BF16), NT_DIMS, preferred_element_type=F32) * SCALE
        s = jnp.where(valid, s + bias, NEG)
        w = _sink_softmax(s, sink_col).astype(BF16)
        o = jnp.dot(w, v_all.astype(BF16), preferred_element_type=F32)
        for hh in range(SWA_Q_HEADS):
            col = CONV_WIDTH + hh * HEAD_DIM
            o_ref[rsl, col:col + HEAD_DIM] = o[hh * t_len:(hh + 1) * t_len].astype(o_ref.dtype)

        qxb = qx_ref[rsl, :]
        qx_rows = jnp.concatenate([qxb[:, hx * HEAD_DIM:(hx + 1) * HEAD_DIM] for hx in range(X_HEADS)], axis=0)
        mk = cmk_ref[pl.ds(m0, n_mem), :].astype(BF16)
        sx = lax.dot_general(qx_rows.astype(BF16), mk, NT_DIMS, preferred_element_type=F32) * SCALE
        wx = _softmax(jnp.where(x_valid, sx, NEG)).astype(BF16)
        ox = jnp.dot(wx, cmv_ref[pl.ds(m0, n_mem), :].astype(BF16), preferred_element_type=F32)
        for hx in range(X_HEADS):
            col = CONV_WIDTH + SWA_WIDTH + hx * HEAD_DIM
            o_ref[rsl, col:col + HEAD_DIM] = ox[hx * t_len:(hx + 1) * t_len].astype(o_ref.dtype)
        return carry

    lax.fori_loop(0, nb, body, 0)


def _mix_sample(sinks, q, k_rows, v_rows, cache_k, cache_v, qx, cache_mk, cache_mv, b, u, cache_conv, conv_w,
                bias_s, bsz, t_len):
    assert t_len == SUBLANES and bsz % SAMPLE_BATCH_TILE == 0 and SWA_KV_HEADS == 2
    nb = SAMPLE_BATCH_TILE
    n_cache, n_new, n_mem = SWA_KV_HEADS * WINDOW, SWA_KV_HEADS * t_len, X_HEADS * MEM_TOKENS
    row = lambda i: (i, 0)
    rows_of = lambda n, w: pl.BlockSpec((nb * n, w), row)
    return pl.pallas_call(
        functools.partial(_mix_sample_kernel, t_len),
        grid=(bsz // nb,),
        in_specs=[pl.BlockSpec(memory_space=pltpu.SMEM),
                  rows_of(t_len, SWA_WIDTH),
                  rows_of(n_new, HEAD_DIM), rows_of(n_new, HEAD_DIM),
                  rows_of(n_cache, HEAD_DIM), rows_of(n_cache, HEAD_DIM),
                  rows_of(t_len, X_WIDTH),
                  rows_of(n_mem, HEAD_DIM), rows_of(n_mem, HEAD_DIM),
                  rows_of(t_len, CONV_WIDTH), rows_of(t_len, CONV_WIDTH),
                  pl.BlockSpec((nb, CONV_K - 1, CONV_WIDTH), lambda i: (i, 0, 0)),
                  _resident((CONV_K, CONV_WIDTH)),
                  _resident((SWA_Q_HEADS * t_len, SAMPLE_KEY_ROWS))],
        out_specs=(rows_of(t_len, D_MODEL), rows_of(n_cache, HEAD_DIM), rows_of(n_cache, HEAD_DIM)),
        out_shape=(jax.ShapeDtypeStruct((bsz * t_len, D_MODEL), F32),
                   jax.ShapeDtypeStruct((bsz * n_cache, HEAD_DIM), F32),
                   jax.ShapeDtypeStruct((bsz * n_cache, HEAD_DIM), F32)),
        compiler_params=_params("arbitrary"),
        name="mix_sample",
    )(sinks, q, k_rows, v_rows, cache_k, cache_v, qx, cache_mk, cache_mv, b, u, cache_conv, conv_w, bias_s)


def _outproj_kernel(x_ref, y_ref, w_ref, g_ref, x1_ref, h_ref):
    x1 = x_ref[...] + jnp.dot(y_ref[...].astype(BF16), w_ref[...], preferred_element_type=F32)
    x1_ref[...] = x1
    h_ref[...] = _rms(x1, g_ref[...]).astype(BF16)


def _out_projection(x2d, y, w_out, g_mlp):
    n = x2d.shape[0]
    tile = min(ROW_TILE, n)
    row = lambda i: (i, 0)
    return pl.pallas_call(
        _outproj_kernel,
        grid=(n // tile,),
        in_specs=[pl.BlockSpec((tile, D_MODEL), row), pl.BlockSpec((tile, D_MODEL), row),
                  _resident((D_MODEL, D_MODEL)), _resident((1, D_MODEL))],
        out_specs=(pl.BlockSpec((tile, D_MODEL), row), pl.BlockSpec((tile, D_MODEL), row)),
        out_shape=(jax.ShapeDtypeStruct((n, D_MODEL), F32), jax.ShapeDtypeStruct((n, D_MODEL), BF16)),
        compiler_params=_params("arbitrary"),
        name="outproj",
    )(x2d, y, w_out, g_mlp)


def _mlp_kernel(x1_ref, h_ref, wu_ref, wd_ref, o_ref):
    @pl.when(pl.program_id(1) == 0)
    def _():
        o_ref[...] = x1_ref[...]

    a = jnp.maximum(jnp.dot(h_ref[...], wu_ref[...], preferred_element_type=F32), 0.0)
    o_ref[...] += jnp.dot((a * a).astype(BF16), wd_ref[...], preferred_element_type=F32)


def _mlp(x1, h, w_up, w_down):
    n = x1.shape[0]
    tile = min(ROW_TILE, n)
    return pl.pallas_call(
        _mlp_kernel,
        grid=(n // tile, D_FF // FF_TILE),
        in_specs=[pl.BlockSpec((tile, D_MODEL), lambda i, j: (i, 0)),
                  pl.BlockSpec((tile, D_MODEL), lambda i, j: (i, 0)),
                  pl.BlockSpec((D_MODEL, FF_TILE), lambda i, j: (0, j)),
                  pl.BlockSpec((FF_TILE, D_MODEL), lambda i, j: (j, 0))],
        out_specs=pl.BlockSpec((tile, D_MODEL), lambda i, j: (i, 0)),
        out_shape=jax.ShapeDtypeStruct((n, D_MODEL), F32),
        compiler_params=_params("arbitrary", "arbitrary"),
        name="mlp",
    )(x1, h, w_up, w_down)


def kernel(x_prompt, x_sample, mem_prompt, cache_conv, cache_swa_k, cache_swa_v, cache_mem_k, cache_mem_v,
           rel_bias_table, g_mix, w_in, conv_w, g_q_swa, g_k_swa, sinks, g_q_x, g_k_x, g_mem,
           w_mem_k, w_mem_v, w_out, g_mlp, w_up, w_down):
    depth = w_in.shape[0]
    bsz, seq, _ = x_prompt.shape
    dbsz, t_len, _ = x_sample.shape
    xp = x_prompt.reshape(bsz * seq, D_MODEL)
    xs = x_sample.reshape(dbsz * t_len, D_MODEL)
    mem2d = mem_prompt.reshape(bsz * MEM_TOKENS, D_MODEL)
    bias_p, bias_s = _bias_tables(rel_bias_table, t_len)

    outs = [[] for _ in range(8)]
    for l in range(depth):
        vec = lambda a: a[l].reshape(1, -1)
        wi, wo = w_in[l].astype(BF16), w_out[l].astype(BF16)
        wu, wd = w_up[l].astype(BF16), w_down[l].astype(BF16)
        wk, wv = w_mem_k[l].astype(BF16), w_mem_v[l].astype(BF16)
        proj = functools.partial(_projections, g_mix=vec(g_mix), w_in=wi, g_q=vec(g_q_swa), g_k=vec(g_k_swa),
                                 g_qx=vec(g_q_x))

        mk, mv, mkb, mvb = _memory_kv(mem2d, vec(g_mem), wk, wv, vec(g_k_x))
        b, u, q, k, v, kb, vb, qx = proj(xp, q_dtype=BF16)
        y = _mix_prompt(sinks[l], q, kb, vb, qx, mkb, mvb, b, u, conv_w[l], bias_p, bsz, seq)
        x1, h = _out_projection(xp, y, wo, vec(g_mlp))
        xp = _mlp(x1, h, wu, wd)
        outs[0].append(u.reshape(bsz, seq, CONV_WIDTH)[:, seq - (CONV_K - 1):])
        last_window = lambda a: a.reshape(bsz, seq, SWA_KV_WIDTH)[:, seq - WINDOW:].reshape(
            bsz, WINDOW, SWA_KV_HEADS, HEAD_DIM)
        outs[1].append(last_window(k))
        outs[2].append(last_window(v))
        outs[3].append(mk.reshape(bsz, MEM_TOKENS, X_HEADS, HEAD_DIM))
        outs[4].append(mv.reshape(bsz, MEM_TOKENS, X_HEADS, HEAD_DIM))

        head_rows = lambda a: a.reshape(-1, HEAD_DIM)
        b, u, q, k, v, _, _, qx = proj(xs, q_dtype=F32)
        y, sk, sv = _mix_sample(
            sinks[l], q, head_rows(k), head_rows(v), head_rows(cache_swa_k[l]), head_rows(cache_swa_v[l]),
            qx, head_rows(cache_mem_k[l]), head_rows(cache_mem_v[l]),
            b, u, cache_conv[l], conv_w[l], bias_s, dbsz, t_len)
        x1, h = _out_projection(xs, y, wo, vec(g_mlp))
        xs = _mlp(x1, h, wu, wd)
        outs[5].append(u.reshape(dbsz, t_len, CONV_WIDTH)[:, t_len - (CONV_K - 1):])
        outs[6].append(sk.reshape(dbsz, WINDOW, SWA_KV_HEADS, HEAD_DIM))
        outs[7].append(sv.reshape(dbsz, WINDOW, SWA_KV_HEADS, HEAD_DIM))

    return (xp.reshape(bsz, seq, D_MODEL), xs.reshape(dbsz, t_len, D_MODEL)) + tuple(jnp.stack(o) for o in outs)
```

```python
import functools
import math

import numpy as np
import jax
import jax.numpy as jnp
from jax import lax
from jax.experimental import pallas as pl
from jax.experimental.pallas import tpu as pltpu

D_MODEL = 2048
HEAD_DIM = 128
SWA_Q_HEADS = 8
SWA_KV_HEADS = 2
SWA_GROUP = SWA_Q_HEADS // SWA_KV_HEADS
SWA_WIDTH = SWA_Q_HEADS * HEAD_DIM
SWA_KV_WIDTH = SWA_KV_HEADS * HEAD_DIM
X_HEADS = 4
X_WIDTH = X_HEADS * HEAD_DIM
CONV_WIDTH = D_MODEL - SWA_WIDTH - X_WIDTH
CONV_K = 3
WINDOW = 128
NUM_BUCKETS = 32
MAX_DISTANCE = WINDOW
MEM_TOKENS = 256
D_FF = 4 * D_MODEL
EPS = 1e-6
NEG = -1e30
SCALE = HEAD_DIM ** -0.5

OFF_B = 0
OFF_C = CONV_WIDTH
OFF_H = 2 * CONV_WIDTH
OFF_Q = 3 * CONV_WIDTH
OFF_K = OFF_Q + SWA_WIDTH
OFF_V = OFF_K + SWA_KV_WIDTH
OFF_QX = OFF_V + SWA_KV_WIDTH
IN_WIDTH = OFF_QX + X_WIDTH

VMEM_LIMIT_V7X = 56 * 1024 * 1024
SUBLANES = 8

ROW_TILE = 512
FF_TILE = 1024
SAMPLE_BATCH_TILE = 8
MIX_BLOCKS = 4

BF16 = jnp.bfloat16
F32 = jnp.float32
NT_DIMS = (((1,), (1,)), ((), ()))


def _params(*sem):
    return pltpu.CompilerParams(dimension_semantics=sem, vmem_limit_bytes=VMEM_LIMIT_V7X)


def _resident(shape):
    nd = len(shape)
    return pl.BlockSpec(shape, lambda *_: (0,) * nd, pipeline_mode=pl.Buffered(1))


def _rms(x, g):
    ms = jnp.mean(x * x, axis=-1, keepdims=True)
    return x * lax.rsqrt(ms + EPS) * g


def _rel_bucket_np(dist):
    n = np.maximum(dist, 0)
    max_exact = NUM_BUCKETS // 2
    nf = np.maximum(n, 1).astype(np.float32)
    large = max_exact + (np.log(nf / np.float32(max_exact)) / np.float32(math.log(MAX_DISTANCE / max_exact))
                         * np.float32(NUM_BUCKETS - max_exact)).astype(np.int32)
    large = np.minimum(large, NUM_BUCKETS - 1)
    return np.where(n < max_exact, n, large).astype(np.int32)


SAMPLE_KEY_ROWS = 3 * WINDOW


def _sample_key_index(t_len):
    c = np.arange(SAMPLE_KEY_ROWS)
    n_cache = SWA_KV_HEADS * WINDOW
    assert n_cache + SWA_KV_HEADS * t_len <= SAMPLE_KEY_ROWS
    pos = np.where(c < n_cache, c // SWA_KV_HEADS, WINDOW + (c - n_cache) // SWA_KV_HEADS)
    return pos, c % SWA_KV_HEADS


def _bias_kernel(tab_ref, bp_ref, bs_ref, op_ref, os_ref):
    bp = bp_ref[...]
    bs = bs_ref[...]
    for hh in range(SWA_Q_HEADS):
        accp = jnp.zeros(bp.shape, F32)
        accs = jnp.zeros(bs.shape, F32)
        for k in range(NUM_BUCKETS):
            t = tab_ref[k * SWA_Q_HEADS + hh]
            accp = jnp.where(bp == k, t, accp)
            accs = jnp.where(bs == k, t, accs)
        h, g = divmod(hh, SWA_GROUP)
        op_ref[h, g * WINDOW:(g + 1) * WINDOW, :] = accp
        os_ref[hh * SUBLANES:(hh + 1) * SUBLANES, :] = accs


def _bias_tables(table, t_len):
    qi = np.arange(WINDOW)[:, None]
    kj = np.arange(2 * WINDOW)[None, :]
    bkt_p = _rel_bucket_np(WINDOW + qi - kj)
    key_pos, _ = _sample_key_index(t_len)
    bkt_s = _rel_bucket_np(np.arange(t_len)[:, None] + WINDOW - key_pos[None, :])
    return pl.pallas_call(
        _bias_kernel,
        out_shape=(jax.ShapeDtypeStruct((SWA_KV_HEADS, SWA_GROUP * WINDOW, 2 * WINDOW), F32),
                   jax.ShapeDtypeStruct((SWA_Q_HEADS * t_len, SAMPLE_KEY_ROWS), F32)),
        in_specs=[pl.BlockSpec(memory_space=pltpu.SMEM),
                  pl.BlockSpec(memory_space=pltpu.VMEM),
                  pl.BlockSpec(memory_space=pltpu.VMEM)],
        out_specs=(pl.BlockSpec(memory_space=pltpu.VMEM), pl.BlockSpec(memory_space=pltpu.VMEM)),
        name="bias",
    )(table.reshape(-1), jnp.asarray(bkt_p), jnp.asarray(bkt_s))


def _memkv_kernel(m_ref, g_ref, wk_ref, wv_ref, gk_ref, mk_ref, mv_ref, mkb_ref, mvb_ref):
    h = _rms(m_ref[...], g_ref[...]).astype(BF16)
    zk = jnp.dot(h, wk_ref[...], preferred_element_type=F32)
    zv = jnp.dot(h, wv_ref[...], preferred_element_type=F32)
    gk = gk_ref[...]
    for hx in range(X_HEADS):
        sl = slice(hx * HEAD_DIM, (hx + 1) * HEAD_DIM)
        mk = _rms(zk[:, sl], gk)
        mk_ref[:, sl] = mk
        mkb_ref[:, sl] = mk.astype(BF16)
    mv_ref[...] = zv
    mvb_ref[...] = zv.astype(BF16)


def _memory_kv(mem2d, g_mem, wk, wv, g_k_x):
    n = mem2d.shape[0]
    tile = MEM_TOKENS
    row = lambda i: (i, 0)
    out = lambda dt: jax.ShapeDtypeStruct((n, X_WIDTH), dt)
    return pl.pallas_call(
        _memkv_kernel,
        grid=(n // tile,),
        in_specs=[pl.BlockSpec((tile, D_MODEL), row), _resident((1, D_MODEL)),
                  _resident((D_MODEL, X_WIDTH)), _resident((D_MODEL, X_WIDTH)), _resident((1, HEAD_DIM))],
        out_specs=tuple(pl.BlockSpec((tile, X_WIDTH), row) for _ in range(4)),
        out_shape=(out(F32), out(F32), out(BF16), out(BF16)),
        compiler_params=_params("arbitrary"),
        name="memkv",
    )(mem2d, g_mem, wk, wv, g_k_x)


def _proj_kernel(x_ref, g_ref, w_ref, gq_ref, gk_ref, gx_ref,
                 b_ref, u_ref, q_ref, k_ref, v_ref, kb_ref, vb_ref, qx_ref):
    h = _rms(x_ref[...], g_ref[...]).astype(BF16)

    def seg(lo, width):
        return jnp.dot(h, w_ref[:, lo:lo + width], preferred_element_type=F32)

    b_ref[...] = seg(OFF_B, CONV_WIDTH)
    u_ref[...] = seg(OFF_C, CONV_WIDTH) * seg(OFF_H, CONV_WIDTH)

    def head_norm(z, g, n_heads, outs):
        for hh in range(n_heads):
            sl = slice(hh * HEAD_DIM, (hh + 1) * HEAD_DIM)
            y = _rms(z[:, sl], g)
            for o in outs:
                o[:, sl] = y.astype(o.dtype)

    head_norm(seg(OFF_Q, SWA_WIDTH), gq_ref[...], SWA_Q_HEADS, (q_ref,))
    head_norm(seg(OFF_K, SWA_KV_WIDTH), gk_ref[...], SWA_KV_HEADS, (k_ref, kb_ref))
    zv = seg(OFF_V, SWA_KV_WIDTH)
    v_ref[...] = zv
    vb_ref[...] = zv.astype(BF16)
    head_norm(seg(OFF_QX, X_WIDTH), gx_ref[...], X_HEADS, (qx_ref,))


def _projections(x2d, g_mix, w_in, g_q, g_k, g_qx, q_dtype):
    n = x2d.shape[0]
    tile = min(ROW_TILE, n)
    row = lambda i: (i, 0)
    widths = (CONV_WIDTH, CONV_WIDTH, SWA_WIDTH, SWA_KV_WIDTH, SWA_KV_WIDTH, SWA_KV_WIDTH, SWA_KV_WIDTH, X_WIDTH)
    dtypes = (F32, F32, q_dtype, F32, F32, BF16, BF16, q_dtype)
    return pl.pallas_call(
        _proj_kernel,
        grid=(n // tile,),
        in_specs=[pl.BlockSpec((tile, D_MODEL), row), _resident((1, D_MODEL)), _resident((D_MODEL, IN_WIDTH)),
                  _resident((1, HEAD_DIM)), _resident((1, HEAD_DIM)), _resident((1, HEAD_DIM))],
        out_specs=tuple(pl.BlockSpec((tile, w), row) for w in widths),
        out_shape=tuple(jax.ShapeDtypeStruct((n, w), dt) for w, dt in zip(widths, dtypes)),
        compiler_params=_params("arbitrary"),
        name="proj",
    )(x2d, g_mix, w_in, g_q, g_k, g_qx)


def _sink_softmax(s, sink_col):
    m = jnp.maximum(jnp.max(s, axis=-1, keepdims=True), sink_col)
    p = jnp.exp(s - m)
    den = jnp.sum(p, axis=-1, keepdims=True) + jnp.exp(sink_col - m)
    return p * (1.0 / den)


def _softmax(s):
    m = jnp.max(s, axis=-1, keepdims=True)
    p = jnp.exp(s - m)
    return p * (1.0 / jnp.sum(p, axis=-1, keepdims=True))


def _mix_prompt_kernel(sink_ref, q_ref, kc_ref, kp_ref, vc_ref, vp_ref, qx_ref, mk_ref, mv_ref,
                       b_ref, uc_ref, up_ref, cw_ref, bias_ref, band_ref, o_ref):
    has_prev = pl.program_id(1) > 0
    rows = MIX_BLOCKS * WINDOW

    u = uc_ref[...]
    prev = jnp.where(has_prev, up_ref[...], 0.0)
    ext = jnp.concatenate([prev, u], axis=0)
    cw = cw_ref[...]
    conv = cw[0:1] * ext[SUBLANES - 2:SUBLANES - 2 + rows]
    conv = conv + cw[1:2] * ext[SUBLANES - 1:SUBLANES - 1 + rows]
    conv = conv + cw[2:3] * u
    o_ref[:, 0:CONV_WIDTH] = (b_ref[...] * conv).astype(o_ref.dtype)

    for j in range(MIX_BLOCKS):
        rsl = slice(j * WINDOW, (j + 1) * WINDOW)
        min_band = jnp.where(has_prev, 0.5, 1.5) if j == 0 else 0.5
        for h in range(SWA_KV_HEADS):
            ksl = slice(h * HEAD_DIM, (h + 1) * HEAD_DIM)
            if j == 0:
                k_all = jnp.concatenate([kp_ref[:, ksl], kc_ref[rsl, ksl]], axis=0)
                v_all = jnp.concatenate([vp_ref[:, ksl], vc_ref[rsl, ksl]], axis=0)
            else:
                k_all = kc_ref[(j - 1) * WINDOW:(j + 1) * WINDOW, ksl]
                v_all = vc_ref[(j - 1) * WINDOW:(j + 1) * WINDOW, ksl]
            for g in range(SWA_GROUP):
                hh = h * SWA_GROUP + g
                q = q_ref[rsl, hh * HEAD_DIM:(hh + 1) * HEAD_DIM]
                s = lax.dot_general(q, k_all, NT_DIMS, preferred_element_type=F32) * SCALE
                s = jnp.where(band_ref[...] > min_band, s + bias_ref[h, g * WINDOW:(g + 1) * WINDOW, :], NEG)
                w = _sink_softmax(s, sink_ref[hh]).astype(BF16)
                col = CONV_WIDTH + hh * HEAD_DIM
                o_ref[rsl, col:col + HEAD_DIM] = jnp.dot(w, v_all, preferred_element_type=F32).astype(o_ref.dtype)

        for hx in range(X_HEADS):
            sl = slice(hx * HEAD_DIM, (hx + 1) * HEAD_DIM)
            s = lax.dot_general(qx_ref[rsl, sl], mk_ref[:, sl], NT_DIMS, preferred_element_type=F32) * SCALE
            w = _softmax(s).astype(BF16)
            col = CONV_WIDTH + SWA_WIDTH + hx * HEAD_DIM
            o_ref[rsl, col:col + HEAD_DIM] = jnp.dot(w, mv_ref[:, sl], preferred_element_type=F32).astype(o_ref.dtype)


def _mix_prompt(sinks, q, kb, vb, qx, mkb, mvb, b, u, conv_w, bias_p, bsz, seq):
    rows = MIX_BLOCKS * WINDOW
    steps = seq // rows
    cur = lambda bi, i: (bi * steps + i, 0)
    prv = lambda bi, i: (jnp.maximum((bi * steps + i) * MIX_BLOCKS - 1, 0), 0)
    prv8 = lambda bi, i: (jnp.maximum((bi * steps + i) * (rows // SUBLANES) - 1, 0), 0)
    per_b = lambda bi, i: (bi, 0)
    dist = WINDOW + np.arange(WINDOW)[:, None] - np.arange(2 * WINDOW)[None, :]
    band = np.where((dist >= 0) & (dist < WINDOW), np.where(np.arange(2 * WINDOW)[None, :] < WINDOW, 1.0, 2.0), 0.0)
    return pl.pallas_call(
        _mix_prompt_kernel,
        grid=(bsz, steps),
        in_specs=[pl.BlockSpec(memory_space=pltpu.SMEM),
                  pl.BlockSpec((rows, SWA_WIDTH), cur),
                  pl.BlockSpec((rows, SWA_KV_WIDTH), cur), pl.BlockSpec((WINDOW, SWA_KV_WIDTH), prv),
                  pl.BlockSpec((rows, SWA_KV_WIDTH), cur), pl.BlockSpec((WINDOW, SWA_KV_WIDTH), prv),
                  pl.BlockSpec((rows, X_WIDTH), cur),
                  pl.BlockSpec((MEM_TOKENS, X_WIDTH), per_b), pl.BlockSpec((MEM_TOKENS, X_WIDTH), per_b),
                  pl.BlockSpec((rows, CONV_WIDTH), cur), pl.BlockSpec((rows, CONV_WIDTH), cur),
                  pl.BlockSpec((SUBLANES, CONV_WIDTH), prv8),
                  _resident((CONV_K, CONV_WIDTH)),
                  _resident((SWA_KV_HEADS, SWA_GROUP * WINDOW, 2 * WINDOW)),
                  _resident((WINDOW, 2 * WINDOW))],
        out_specs=pl.BlockSpec((rows, D_MODEL), cur),
        out_shape=jax.ShapeDtypeStruct((bsz * seq, D_MODEL), BF16),
        compiler_params=_params("arbitrary", "arbitrary"),
        name="mix_prompt",
    )(sinks, q, kb, kb, vb, vb, qx, mkb, mvb, b, u, u, conv_w, bias_p, jnp.asarray(band, F32))


def _mix_sample_kernel(t_len, sink_ref, q_ref, kn_ref, vn_ref, ck_ref, cv_ref, qx_ref, cmk_ref, cmv_ref,
                       b_ref, u_ref, cc_ref, cw_ref, bias_ref, o_ref, sk_ref, sv_ref):
    nb = cc_ref.shape[0]
    n_cache = SWA_KV_HEADS * WINDOW
    n_new = SWA_KV_HEADS * t_len
    n_mem = X_HEADS * MEM_TOKENS
    log_t = int(math.log2(t_len))

    rows = SWA_Q_HEADS * t_len
    r = lax.broadcasted_iota(jnp.int32, (rows, SAMPLE_KEY_ROWS), 0)
    c = lax.broadcasted_iota(jnp.int32, (rows, SAMPLE_KEY_ROWS), 1)
    key_pos = jnp.where(c < n_cache, c >> 1, WINDOW + ((c - n_cache) >> 1))
    dist = (r & (t_len - 1)) + WINDOW - key_pos
    valid = (dist >= 0) & (dist < WINDOW) & ((c & (SWA_KV_HEADS - 1)) == (r >> int(math.log2(SWA_GROUP * t_len))))
    bias = bias_ref[...]
    sink_col = jnp.concatenate([jnp.full((t_len, 1), sink_ref[hh], F32) for hh in range(SWA_Q_HEADS)], axis=0)
    xr = lax.broadcasted_iota(jnp.int32, (X_HEADS * t_len, n_mem), 0)
    xc = lax.broadcasted_iota(jnp.int32, (X_HEADS * t_len, n_mem), 1)
    x_valid = (xc & (X_HEADS - 1)) == (xr >> log_t)
    cw = cw_ref[...]
    trow = lax.broadcasted_iota(jnp.int32, (t_len, CONV_WIDTH), 0)
    zeros_pad = jnp.zeros((SAMPLE_KEY_ROWS - n_cache - n_new, HEAD_DIM), F32)

    def body(bi, carry):
        rsl = pl.ds(pl.multiple_of(bi * t_len, t_len), t_len)
        c0 = pl.multiple_of(bi * n_cache, n_cache)
        n0 = pl.multiple_of(bi * n_new, n_new)
        m0 = pl.multiple_of(bi * n_mem, n_mem)

        for dst, cache, new in ((sk_ref, ck_ref, kn_ref), (sv_ref, cv_ref, vn_ref)):
            dst[pl.ds(c0, n_cache - n_new), :] = cache[pl.ds(c0 + n_new, n_cache - n_new), :]
            dst[pl.ds(c0 + (n_cache - n_new), n_new), :] = new[pl.ds(n0, n_new), :]

        u = u_ref[rsl, :]
        cc = cc_ref[bi]
        cc1 = jnp.broadcast_to(cc[1:2], u.shape)
        cc0 = jnp.broadcast_to(cc[0:1], u.shape)
        u_m1 = jnp.where(trow >= 1, pltpu.roll(u, 1, 0), cc1)
        u_m2 = jnp.where(trow >= 2, pltpu.roll(u, 2, 0), jnp.where(trow == 1, cc1, cc0))
        conv = cw[0:1] * u_m2
        conv = conv + cw[1:2] * u_m1
        conv = conv + cw[2:3] * u
        o_ref[rsl, 0:CONV_WIDTH] = (b_ref[rsl, :] * conv).astype(o_ref.dtype)

        qb = q_ref[rsl, :]
        q_rows = jnp.concatenate([qb[:, hh * HEAD_DIM:(hh + 1) * HEAD_DIM] for hh in range(SWA_Q_HEADS)], axis=0)
        k_all = jnp.concatenate([ck_ref[pl.ds(c0, n_cache), :], kn_ref[pl.ds(n0, n_new), :], zeros_pad], axis=0)
        v_all = jnp.concatenate([cv_ref[pl.ds(c0, n_cache), :], vn_ref[pl.ds(n0, n_new), :], zeros_pad], axis=0)
        s = lax.dot_general(q_rows.astype(BF16), k_all.astype(BF16), NT_DIMS, preferred_element_type=F32) * SCALE
        s = jnp.where(valid, s + bias, NEG)
        w = _sink_softmax(s, sink_col).astype(BF16)
        o = jnp.dot(w, v_all.astype(BF16), preferred_element_type=F32)
        for hh in range(SWA_Q_HEADS):
            col = CONV_WIDTH + hh * HEAD_DIM
            o_ref[rsl, col:col + HEAD_DIM] = o[hh * t_len:(hh + 1) * t_len].astype(o_ref.dtype)

        qxb = qx_ref[rsl, :]
        qx_rows = jnp.concatenate([qxb[:, hx * HEAD_DIM:(hx + 1) * HEAD_DIM] for hx in range(X_HEADS)], axis=0)
        mk = cmk_ref[pl.ds(m0, n_mem), :].astype(BF16)
        sx = lax.dot_general(qx_rows.astype(BF16), mk, NT_DIMS, preferred_element_type=F32) * SCALE
        wx = _softmax(jnp.where(x_valid, sx, NEG)).astype(BF16)
        ox = jnp.dot(wx, cmv_ref[pl.ds(m0, n_mem), :].astype(BF16), preferred_element_type=F32)
        for hx in range(X_HEADS):
            col = CONV_WIDTH + SWA_WIDTH + hx * HEAD_DIM
            o_ref[rsl, col:col + HEAD_DIM] = ox[hx * t_len:(hx + 1) * t_len].astype(o_ref.dtype)
        return carry

    lax.fori_loop(0, nb, body, 0)


def _mix_sample(sinks, q, k_rows, v_rows, cache_k, cache_v, qx, cache_mk, cache_mv, b, u, cache_conv, conv_w,
                bias_s, bsz, t_len):
    assert t_len == SUBLANES and bsz % SAMPLE_BATCH_TILE == 0 and SWA_KV_HEADS == 2
    nb = SAMPLE_BATCH_TILE
    n_cache, n_new, n_mem = SWA_KV_HEADS * WINDOW, SWA_KV_HEADS * t_len, X_HEADS * MEM_TOKENS
    row = lambda i: (i, 0)
    rows_of = lambda n, w: pl.BlockSpec((nb * n, w), row)
    return pl.pallas_call(
        functools.partial(_mix_sample_kernel, t_len),
        grid=(bsz // nb,),
        in_specs=[pl.BlockSpec(memory_space=pltpu.SMEM),
                  rows_of(t_len, SWA_WIDTH),
                  rows_of(n_new, HEAD_DIM), rows_of(n_new, HEAD_DIM),
                  rows_of(n_cache, HEAD_DIM), rows_of(n_cache, HEAD_DIM),
                  rows_of(t_len, X_WIDTH),
                  rows_of(n_mem, HEAD_DIM), rows_of(n_mem, HEAD_DIM),
                  rows_of(t_len, CONV_WIDTH), rows_of(t_len, CONV_WIDTH),
                  pl.BlockSpec((nb, CONV_K - 1, CONV_WIDTH), lambda i: (i, 0, 0)),
                  _resident((CONV_K, CONV_WIDTH)),
                  _resident((SWA_Q_HEADS * t_len, SAMPLE_KEY_ROWS))],
        out_specs=(rows_of(t_len, D_MODEL), rows_of(n_cache, HEAD_DIM), rows_of(n_cache, HEAD_DIM)),
        out_shape=(jax.ShapeDtypeStruct((bsz * t_len, D_MODEL), F32),
                   jax.ShapeDtypeStruct((bsz * n_cache, HEAD_DIM), F32),
                   jax.ShapeDtypeStruct((bsz * n_cache, HEAD_DIM), F32)),
        compiler_params=_params("arbitrary"),
        name="mix_sample",
    )(sinks, q, k_rows, v_rows, cache_k, cache_v, qx, cache_mk, cache_mv, b, u, cache_conv, conv_w, bias_s)


def _outproj_kernel(x_ref, y_ref, w_ref, g_ref, x1_ref, h_ref):
    x1 = x_ref[...] + jnp.dot(y_ref[...].astype(BF16), w_ref[...], preferred_element_type=F32)
    x1_ref[...] = x1
    h_ref[...] = _rms(x1, g_ref[...]).astype(BF16)


def _out_projection(x2d, y, w_out, g_mlp):
    n = x2d.shape[0]
    tile = min(ROW_TILE, n)
    row = lambda i: (i, 0)
    return pl.pallas_call(
        _outproj_kernel,
        grid=(n // tile,),
        in_specs=[pl.BlockSpec((tile, D_MODEL), row), pl.BlockSpec((tile, D_MODEL), row),
                  _resident((D_MODEL, D_MODEL)), _resident((1, D_MODEL))],
        out_specs=(pl.BlockSpec((tile, D_MODEL), row), pl.BlockSpec((tile, D_MODEL), row)),
        out_shape=(jax.ShapeDtypeStruct((n, D_MODEL), F32), jax.ShapeDtypeStruct((n, D_MODEL), BF16)),
        compiler_params=_params("arbitrary"),
        name="outproj",
    )(x2d, y, w_out, g_mlp)


def _mlp_kernel(x1_ref, h_ref, wu_ref, wd_ref, o_ref):
    @pl.when(pl.program_id(1) == 0)
    def _():
        o_ref[...] = x1_ref[...]

    a = jnp.maximum(jnp.dot(h_ref[...], wu_ref[...], preferred_element_type=F32), 0.0)
    o_ref[...] += jnp.dot((a * a).astype(BF16), wd_ref[...], preferred_element_type=F32)


def _mlp(x1, h, w_up, w_down):
    n = x1.shape[0]
    tile = min(ROW_TILE, n)
    return pl.pallas_call(
        _mlp_kernel,
        grid=(n // tile, D_FF // FF_TILE),
        in_specs=[pl.BlockSpec((tile, D_MODEL), lambda i, j: (i, 0)),
                  pl.BlockSpec((tile, D_MODEL), lambda i, j: (i, 0)),
                  pl.BlockSpec((D_MODEL, FF_TILE), lambda i, j: (0, j)),
                  pl.BlockSpec((FF_TILE, D_MODEL), lambda i, j: (j, 0))],
        out_specs=pl.BlockSpec((tile, D_MODEL), lambda i, j: (i, 0)),
        out_shape=jax.ShapeDtypeStruct((n, D_MODEL), F32),
        compiler_params=_params("arbitrary", "arbitrary"),
        name="mlp",
    )(x1, h, w_up, w_down)


def kernel(x_prompt, x_sample, mem_prompt, cache_conv, cache_swa_k, cache_swa_v, cache_mem_k, cache_mem_v,
           rel_bias_table, g_mix, w_in, conv_w, g_q_swa, g_k_swa, sinks, g_q_x, g_k_x, g_mem,
           w_mem_k, w_mem_v, w_out, g_mlp, w_up, w_down):
    depth = w_in.shape[0]
    bsz, seq, _ = x_prompt.shape
    dbsz, t_len, _ = x_sample.shape
    xp = x_prompt.reshape(bsz * seq, D_MODEL)
    xs = x_sample.reshape(dbsz * t_len, D_MODEL)
    mem2d = mem_prompt.reshape(bsz * MEM_TOKENS, D_MODEL)
    bias_p, bias_s = _bias_tables(rel_bias_table, t_len)

    outs = [[] for _ in range(8)]
    for l in range(depth):
        vec = lambda a: a[l].reshape(1, -1)
        wi, wo = w_in[l].astype(BF16), w_out[l].astype(BF16)
        wu, wd = w_up[l].astype(BF16), w_down[l].astype(BF16)
        wk, wv = w_mem_k[l].astype(BF16), w_mem_v[l].astype(BF16)
        proj = functools.partial(_projections, g_mix=vec(g_mix), w_in=wi, g_q=vec(g_q_swa), g_k=vec(g_k_swa),
                                 g_qx=vec(g_q_x))

        mk, mv, mkb, mvb = _memory_kv(mem2d, vec(g_mem), wk, wv, vec(g_k_x))
        b, u, q, k, v, kb, vb, qx = proj(xp, q_dtype=BF16)
        y = _mix_prompt(sinks[l], q, kb, vb, qx, mkb, mvb, b, u, conv_w[l], bias_p, bsz, seq)
        x1, h = _out_projection(xp, y, wo, vec(g_mlp))
        xp = _mlp(x1, h, wu, wd)
        outs[0].append(u.reshape(bsz, seq, CONV_WIDTH)[:, seq - (CONV_K - 1):])
        last_window = lambda a: a.reshape(bsz, seq, SWA_KV_WIDTH)[:, seq - WINDOW:].reshape(
            bsz, WINDOW, SWA_KV_HEADS, HEAD_DIM)
        outs[1].append(last_window(k))
        outs[2].append(last_window(v))
        outs[3].append(mk.reshape(bsz, MEM_TOKENS, X_HEADS, HEAD_DIM))
        outs[4].append(mv.reshape(bsz, MEM_TOKENS, X_HEADS, HEAD_DIM))

        head_rows = lambda a: a.reshape(-1, HEAD_DIM)
        b, u, q, k, v, _, _, qx = proj(xs, q_dtype=F32)
        y, sk, sv = _mix_sample(
            sinks[l], q, head_rows(k), head_rows(v), head_rows(cache_swa_k[l]), head_rows(cache_swa_v[l]),
            qx, head_rows(cache_mem_k[l]), head_rows(cache_mem_v[l]),
            b, u, cache_conv[l], conv_w[l], bias_s, dbsz, t_len)
        x1, h = _out_projection(xs, y, wo, vec(g_mlp))
        xs = _mlp(x1, h, wu, wd)
        outs[5].append(u.reshape(dbsz, t_len, CONV_WIDTH)[:, t_len - (CONV_K - 1):])
        outs[6].append(sk.reshape(dbsz, WINDOW, SWA_KV_HEADS, HEAD_DIM))
        outs[7].append(sv.reshape(dbsz, WINDOW, SWA_KV_HEADS, HEAD_DIM))

    return (xp.reshape(bsz, seq, D_MODEL), xs.reshape(dbsz, t_len, D_MODEL)) + tuple(jnp.stack(o) for o in outs)
```

```python
import functools
import math

import numpy as np
import jax
import jax.numpy as jnp
from jax import lax
from jax.experimental import pallas as pl
from jax.experimental.pallas import tpu as pltpu

D_MODEL = 2048
HEAD_DIM = 128
SWA_Q_HEADS = 8
SWA_KV_HEADS = 2
SWA_GROUP = SWA_Q_HEADS // SWA_KV_HEADS
SWA_WIDTH = SWA_Q_HEADS * HEAD_DIM
SWA_KV_WIDTH = SWA_KV_HEADS * HEAD_DIM
X_HEADS = 4
X_WIDTH = X_HEADS * HEAD_DIM
CONV_WIDTH = D_MODEL - SWA_WIDTH - X_WIDTH
CONV_K = 3
WINDOW = 128
NUM_BUCKETS = 32
MAX_DISTANCE = WINDOW
MEM_TOKENS = 256
D_FF = 4 * D_MODEL
EPS = 1e-6
NEG = -1e30
SCALE = HEAD_DIM ** -0.5

OFF_B = 0
OFF_C = CONV_WIDTH
OFF_H = 2 * CONV_WIDTH
OFF_Q = 3 * CONV_WIDTH
OFF_K = OFF_Q + SWA_WIDTH
OFF_V = OFF_K + SWA_KV_WIDTH
OFF_QX = OFF_V + SWA_KV_WIDTH
IN_WIDTH = OFF_QX + X_WIDTH

VMEM_LIMIT_V7X = 56 * 1024 * 1024
SUBLANES = 8

ROW_TILE = 512
FF_TILE = 1024
SAMPLE_BATCH_TILE = 8
MIX_BLOCKS = 4

BF16 = jnp.bfloat16
F32 = jnp.float32
NT_DIMS = (((1,), (1,)), ((), ()))


def _params(*sem):
    return pltpu.CompilerParams(dimension_semantics=sem, vmem_limit_bytes=VMEM_LIMIT_V7X)


def _resident(shape):
    nd = len(shape)
    return pl.BlockSpec(shape, lambda *_: (0,) * nd, pipeline_mode=pl.Buffered(1))


def _rms(x, g):
    ms = jnp.mean(x * x, axis=-1, keepdims=True)
    return x * lax.rsqrt(ms + EPS) * g


def _rel_bucket_np(dist):
    n = np.maximum(dist, 0)
    max_exact = NUM_BUCKETS // 2
    nf = np.maximum(n, 1).astype(np.float32)
    large = max_exact + (np.log(nf / np.float32(max_exact)) / np.float32(math.log(MAX_DISTANCE / max_exact))
                         * np.float32(NUM_BUCKETS - max_exact)).astype(np.int32)
    large = np.minimum(large, NUM_BUCKETS - 1)
    return np.where(n < max_exact, n, large).astype(np.int32)


SAMPLE_KEY_ROWS = 3 * WINDOW


def _sample_key_index(t_len):
    c = np.arange(SAMPLE_KEY_ROWS)
    n_cache = SWA_KV_HEADS * WINDOW
    assert n_cache + SWA_KV_HEADS * t_len <= SAMPLE_KEY_ROWS
    pos = np.where(c < n_cache, c // SWA_KV_HEADS, WINDOW + (c - n_cache) // SWA_KV_HEADS)
    return pos, c % SWA_KV_HEADS


def _bias_kernel(tab_ref, bp_ref, bs_ref, op_ref, os_ref):
    bp = bp_ref[...]
    bs = bs_ref[...]
    for hh in range(SWA_Q_HEADS):
        accp = jnp.zeros(bp.shape, F32)
        accs = jnp.zeros(bs.shape, F32)
        for k in range(NUM_BUCKETS):
            t = tab_ref[k * SWA_Q_HEADS + hh]
            accp = jnp.where(bp == k, t, accp)
            accs = jnp.where(bs == k, t, accs)
        h, g = divmod(hh, SWA_GROUP)
        op_ref[h, g * WINDOW:(g + 1) * WINDOW, :] = accp
        os_ref[hh * SUBLANES:(hh + 1) * SUBLANES, :] = accs


def _bias_tables(table, t_len):
    qi = np.arange(WINDOW)[:, None]
    kj = np.arange(2 * WINDOW)[None, :]
    bkt_p = _rel_bucket_np(WINDOW + qi - kj)
    key_pos, _ = _sample_key_index(t_len)
    bkt_s = _rel_bucket_np(np.arange(t_len)[:, None] + WINDOW - key_pos[None, :])
    return pl.pallas_call(
        _bias_kernel,
        out_shape=(jax.ShapeDtypeStruct((SWA_KV_HEADS, SWA_GROUP * WINDOW, 2 * WINDOW), F32),
                   jax.ShapeDtypeStruct((SWA_Q_HEADS * t_len, SAMPLE_KEY_ROWS), F32)),
        in_specs=[pl.BlockSpec(memory_space=pltpu.SMEM),
                  pl.BlockSpec(memory_space=pltpu.VMEM),
                  pl.BlockSpec(memory_space=pltpu.VMEM)],
        out_specs=(pl.BlockSpec(memory_space=pltpu.VMEM), pl.BlockSpec(memory_space=pltpu.VMEM)),
        name="bias",
    )(table.reshape(-1), jnp.asarray(bkt_p), jnp.asarray(bkt_s))


def _memkv_kernel(m_ref, g_ref, wk_ref, wv_ref, gk_ref, mk_ref, mv_ref, mkb_ref, mvb_ref):
    h = _rms(m_ref[...], g_ref[...]).astype(BF16)
    zk = jnp.dot(h, wk_ref[...], preferred_element_type=F32)
    zv = jnp.dot(h, wv_ref[...], preferred_element_type=F32)
    gk = gk_ref[...]
    for hx in range(X_HEADS):
        sl = slice(hx * HEAD_DIM, (hx + 1) * HEAD_DIM)
        mk = _rms(zk[:, sl], gk)
        mk_ref[:, sl] = mk
        mkb_ref[:, sl] = mk.astype(BF16)
    mv_ref[...] = zv
    mvb_ref[...] = zv.astype(BF16)


def _memory_kv(mem2d, g_mem, wk, wv, g_k_x):
    n = mem2d.shape[0]
    tile = MEM_TOKENS
    row = lambda i: (i, 0)
    out = lambda dt: jax.ShapeDtypeStruct((n, X_WIDTH), dt)
    return pl.pallas_call(
        _memkv_kernel,
        grid=(n // tile,),
        in_specs=[pl.BlockSpec((tile, D_MODEL), row), _resident((1, D_MODEL)),
                  _resident((D_MODEL, X_WIDTH)), _resident((D_MODEL, X_WIDTH)), _resident((1, HEAD_DIM))],
        out_specs=tuple(pl.BlockSpec((tile, X_WIDTH), row) for _ in range(4)),
        out_shape=(out(F32), out(F32), out(BF16), out(BF16)),
        compiler_params=_params("arbitrary"),
        name="memkv",
    )(mem2d, g_mem, wk, wv, g_k_x)


def _proj_kernel(x_ref, g_ref, w_ref, gq_ref, gk_ref, gx_ref,
                 b_ref, u_ref, q_ref, k_ref, v_ref, kb_ref, vb_ref, qx_ref):
    h = _rms(x_ref[...], g_ref[...]).astype(BF16)

    def seg(lo, width):
        return jnp.dot(h, w_ref[:, lo:lo + width], preferred_element_type=F32)

    b_ref[...] = seg(OFF_B, CONV_WIDTH)
    u_ref[...] = seg(OFF_C, CONV_WIDTH) * seg(OFF_H, CONV_WIDTH)

    def head_norm(z, g, n_heads, outs):
        for hh in range(n_heads):
            sl = slice(hh * HEAD_DIM, (hh + 1) * HEAD_DIM)
            y = _rms(z[:, sl], g)
            for o in outs:
                o[:, sl] = y.astype(o.dtype)

    head_norm(seg(OFF_Q, SWA_WIDTH), gq_ref[...], SWA_Q_HEADS, (q_ref,))
    head_norm(seg(OFF_K, SWA_KV_WIDTH), gk_ref[...], SWA_KV_HEADS, (k_ref, kb_ref))
    zv = seg(OFF_V, SWA_KV_WIDTH)
    v_ref[...] = zv
    vb_ref[...] = zv.astype(BF16)
    head_norm(seg(OFF_QX, X_WIDTH), gx_ref[...], X_HEADS, (qx_ref,))


def _projections(x2d, g_mix, w_in, g_q, g_k, g_qx, q_dtype):
    n = x2d.shape[0]
    tile = min(ROW_TILE, n)
    row = lambda i: (i, 0)
    widths = (CONV_WIDTH, CONV_WIDTH, SWA_WIDTH, SWA_KV_WIDTH, SWA_KV_WIDTH, SWA_KV_WIDTH, SWA_KV_WIDTH, X_WIDTH)
    dtypes = (F32, F32, q_dtype, F32, F32, BF16, BF16, q_dtype)
    return pl.pallas_call(
        _proj_kernel,
        grid=(n // tile,),
        in_specs=[pl.BlockSpec((tile, D_MODEL), row), _resident((1, D_MODEL)), _resident((D_MODEL, IN_WIDTH)),
                  _resident((1, HEAD_DIM)), _resident((1, HEAD_DIM)), _resident((1, HEAD_DIM))],
        out_specs=tuple(pl.BlockSpec((tile, w), row) for w in widths),
        out_shape=tuple(jax.ShapeDtypeStruct((n, w), dt) for w, dt in zip(widths, dtypes)),
        compiler_params=_params("arbitrary"),
        name="proj",
    )(x2d, g_mix, w_in, g_q, g_k, g_qx)


def _sink_softmax(s, sink_col):
    m = jnp.maximum(jnp.max(s, axis=-1, keepdims=True), sink_col)
    p = jnp.exp(s - m)
    den = jnp.sum(p, axis=-1, keepdims=True) + jnp.exp(sink_col - m)
    return p * (1.0 / den)


def _softmax(s):
    m = jnp.max(s, axis=-1, keepdims=True)
    p = jnp.exp(s - m)
    return p * (1.0 / jnp.sum(p, axis=-1, keepdims=True))


def _mix_prompt_kernel(sink_ref, q_ref, kc_ref, kp_ref, vc_ref, vp_ref, qx_ref, mk_ref, mv_ref,
                       b_ref, uc_ref, up_ref, cw_ref, bias_ref, band_ref, o_ref):
    has_prev = pl.program_id(1) > 0
    rows = MIX_BLOCKS * WINDOW

    u = uc_ref[...]
    prev = jnp.where(has_prev, up_ref[...], 0.0)
    ext = jnp.concatenate([prev, u], axis=0)
    cw = cw_ref[...]
    conv = cw[0:1] * ext[SUBLANES - 2:SUBLANES - 2 + rows]
    conv = conv + cw[1:2] * ext[SUBLANES - 1:SUBLANES - 1 + rows]
    conv = conv + cw[2:3] * u
    o_ref[:, 0:CONV_WIDTH] = (b_ref[...] * conv).astype(o_ref.dtype)

    for j in range(MIX_BLOCKS):
        rsl = slice(j * WINDOW, (j + 1) * WINDOW)
        min_band = jnp.where(has_prev, 0.5, 1.5) if j == 0 else 0.5
        for h in range(SWA_KV_HEADS):
            ksl = slice(h * HEAD_DIM, (h + 1) * HEAD_DIM)
            if j == 0:
                k_all = jnp.concatenate([kp_ref[:, ksl], kc_ref[rsl, ksl]], axis=0)
                v_all = jnp.concatenate([vp_ref[:, ksl], vc_ref[rsl, ksl]], axis=0)
            else:
                k_all = kc_ref[(j - 1) * WINDOW:(j + 1) * WINDOW, ksl]
                v_all = vc_ref[(j - 1) * WINDOW:(j + 1) * WINDOW, ksl]
            for g in range(SWA_GROUP):
                hh = h * SWA_GROUP + g
                q = q_ref[rsl, hh * HEAD_DIM:(hh + 1) * HEAD_DIM]
                s = lax.dot_general(q, k_all, NT_DIMS, preferred_element_type=F32) * SCALE
                s = jnp.where(band_ref[...] > min_band, s + bias_ref[h, g * WINDOW:(g + 1) * WINDOW, :], NEG)
                w = _sink_softmax(s, sink_ref[hh]).astype(BF16)
                col = CONV_WIDTH + hh * HEAD_DIM
                o_ref[rsl, col:col + HEAD_DIM] = jnp.dot(w, v_all, preferred_element_type=F32).astype(o_ref.dtype)

        for hx in range(X_HEADS):
            sl = slice(hx * HEAD_DIM, (hx + 1) * HEAD_DIM)
            s = lax.dot_general(qx_ref[rsl, sl], mk_ref[:, sl], NT_DIMS, preferred_element_type=F32) * SCALE
            w = _softmax(s).astype(BF16)
            col = CONV_WIDTH + SWA_WIDTH + hx * HEAD_DIM
            o_ref[rsl, col:col + HEAD_DIM] = jnp.dot(w, mv_ref[:, sl], preferred_element_type=F32).astype(o_ref.dtype)


def _mix_prompt(sinks, q, kb, vb, qx, mkb, mvb, b, u, conv_w, bias_p, bsz, seq):
    rows = MIX_BLOCKS * WINDOW
    steps = seq // rows
    cur = lambda bi, i: (bi * steps + i, 0)
    prv = lambda bi, i: (jnp.maximum((bi * steps + i) * MIX_BLOCKS - 1, 0), 0)
    prv8 = lambda bi, i: (jnp.maximum((bi * steps + i) * (rows // SUBLANES) - 1, 0), 0)
    per_b = lambda bi, i: (bi, 0)
    dist = WINDOW + np.arange(WINDOW)[:, None] - np.arange(2 * WINDOW)[None, :]
    band = np.where((dist >= 0) & (dist < WINDOW), np.where(np.arange(2 * WINDOW)[None, :] < WINDOW, 1.0, 2.0), 0.0)
    return pl.pallas_call(
        _mix_prompt_kernel,
        grid=(bsz, steps),
        in_specs=[pl.BlockSpec(memory_space=pltpu.SMEM),
                  pl.BlockSpec((rows, SWA_WIDTH), cur),
                  pl.BlockSpec((rows, SWA_KV_WIDTH), cur), pl.BlockSpec((WINDOW, SWA_KV_WIDTH), prv),
                  pl.BlockSpec((rows, SWA_KV_WIDTH), cur), pl.BlockSpec((WINDOW, SWA_KV_WIDTH), prv),
                  pl.BlockSpec((rows, X_WIDTH), cur),
                  pl.BlockSpec((MEM_TOKENS, X_WIDTH), per_b), pl.BlockSpec((MEM_TOKENS, X_WIDTH), per_b),
                  pl.BlockSpec((rows, CONV_WIDTH), cur), pl.BlockSpec((rows, CONV_WIDTH), cur),
                  pl.BlockSpec((SUBLANES, CONV_WIDTH), prv8),
                  _resident((CONV_K, CONV_WIDTH)),
                  _resident((SWA_KV_HEADS, SWA_GROUP * WINDOW, 2 * WINDOW)),
                  _resident((WINDOW, 2 * WINDOW))],
        out_specs=pl.BlockSpec((rows, D_MODEL), cur),
        out_shape=jax.ShapeDtypeStruct((bsz * seq, D_MODEL), BF16),
        compiler_params=_params("arbitrary", "arbitrary"),
        name="mix_prompt",
    )(sinks, q, kb, kb, vb, vb, qx, mkb, mvb, b, u, u, conv_w, bias_p, jnp.asarray(band, F32))


def _mix_sample_kernel(t_len, sink_ref, q_ref, kn_ref, vn_ref, ck_ref, cv_ref, qx_ref, cmk_ref, cmv_ref,
                       b_ref, u_ref, cc_ref, cw_ref, bias_ref, o_ref, sk_ref, sv_ref):
    nb = cc_ref.shape[0]
    n_cache = SWA_KV_HEADS * WINDOW
    n_new = SWA_KV_HEADS * t_len
    n_mem = X_HEADS * MEM_TOKENS
    log_t = int(math.log2(t_len))

    rows = SWA_Q_HEADS * t_len
    r = lax.broadcasted_iota(jnp.int32, (nb * rows, SAMPLE_KEY_ROWS), 0) & (rows - 1)
    c = lax.broadcasted_iota(jnp.int32, (nb * rows, SAMPLE_KEY_ROWS), 1)
    key_pos = jnp.where(c < n_cache, c >> 1, WINDOW + ((c - n_cache) >> 1))
    dist = (r & (t_len - 1)) + WINDOW - key_pos
    valid = (dist >= 0) & (dist < WINDOW) & ((c & (SWA_KV_HEADS - 1)) == (r >> int(math.log2(SWA_GROUP * t_len))))
    bias = jnp.concatenate([bias_ref[...]] * nb, axis=0)
    sink_col = jnp.concatenate([jnp.full((t_len, 1), sink_ref[hh], F32) for hh in range(SWA_Q_HEADS)] * nb, axis=0)
    xrows = X_HEADS * t_len
    xr = lax.broadcasted_iota(jnp.int32, (nb * xrows, n_mem), 0) & (xrows - 1)
    xc = lax.broadcasted_iota(jnp.int32, (nb * xrows, n_mem), 1)
    x_valid = (xc & (X_HEADS - 1)) == (xr >> log_t)
    cw = cw_ref[...]
    trow = lax.broadcasted_iota(jnp.int32, (t_len, CONV_WIDTH), 0)
    zeros_pad = jnp.zeros((SAMPLE_KEY_ROWS - n_cache - n_new, HEAD_DIM), F32)

    s_list, sx_list = [], []
    for bi in range(nb):
        rsl = slice(bi * t_len, (bi + 1) * t_len)
        c0, n0, m0 = bi * n_cache, bi * n_new, bi * n_mem

        for dst, cache, new in ((sk_ref, ck_ref, kn_ref), (sv_ref, cv_ref, vn_ref)):
            dst[c0:c0 + n_cache - n_new, :] = cache[c0 + n_new:c0 + n_cache, :]
            dst[c0 + n_cache - n_new:c0 + n_cache, :] = new[n0:n0 + n_new, :]

        u = u_ref[rsl, :]
        cc = cc_ref[bi]
        cc1 = jnp.broadcast_to(cc[1:2], u.shape)
        cc0 = jnp.broadcast_to(cc[0:1], u.shape)
        u_m1 = jnp.where(trow >= 1, pltpu.roll(u, 1, 0), cc1)
        u_m2 = jnp.where(trow >= 2, pltpu.roll(u, 2, 0), jnp.where(trow == 1, cc1, cc0))
        conv = cw[0:1] * u_m2
        conv = conv + cw[1:2] * u_m1
        conv = conv + cw[2:3] * u
        o_ref[rsl, 0:CONV_WIDTH] = (b_ref[rsl, :] * conv).astype(o_ref.dtype)

        qb = q_ref[rsl, :]
        q_rows = jnp.concatenate([qb[:, hh * HEAD_DIM:(hh + 1) * HEAD_DIM] for hh in range(SWA_Q_HEADS)], axis=0)
        k_all = jnp.concatenate([ck_ref[c0:c0 + n_cache, :], kn_ref[n0:n0 + n_new, :], zeros_pad], axis=0)
        s_list.append(lax.dot_general(q_rows.astype(BF16), k_all.astype(BF16), NT_DIMS, preferred_element_type=F32))
        qxb = qx_ref[rsl, :]
        qx_rows = jnp.concatenate([qxb[:, hx * HEAD_DIM:(hx + 1) * HEAD_DIM] for hx in range(X_HEADS)], axis=0)
        mk = cmk_ref[m0:m0 + n_mem, :].astype(BF16)
        sx_list.append(lax.dot_general(qx_rows.astype(BF16), mk, NT_DIMS, preferred_element_type=F32))

    s = jnp.concatenate(s_list, axis=0) * SCALE
    w = _sink_softmax(jnp.where(valid, s + bias, NEG), sink_col).astype(BF16)
    sx = jnp.concatenate(sx_list, axis=0) * SCALE
    wx = _softmax(jnp.where(x_valid, sx, NEG)).astype(BF16)

    for bi in range(nb):
        rsl = slice(bi * t_len, (bi + 1) * t_len)
        c0, n0, m0 = bi * n_cache, bi * n_new, bi * n_mem
        v_all = jnp.concatenate([cv_ref[c0:c0 + n_cache, :], vn_ref[n0:n0 + n_new, :], zeros_pad], axis=0)
        o = jnp.dot(w[bi * rows:(bi + 1) * rows], v_all.astype(BF16), preferred_element_type=F32)
        for hh in range(SWA_Q_HEADS):
            col = CONV_WIDTH + hh * HEAD_DIM
            o_ref[rsl, col:col + HEAD_DIM] = o[hh * t_len:(hh + 1) * t_len].astype(o_ref.dtype)
        mv = cmv_ref[m0:m0 + n_mem, :].astype(BF16)
        ox = jnp.dot(wx[bi * xrows:(bi + 1) * xrows], mv, preferred_element_type=F32)
        for hx in range(X_HEADS):
            col = CONV_WIDTH + SWA_WIDTH + hx * HEAD_DIM
            o_ref[rsl, col:col + HEAD_DIM] = ox[hx * t_len:(hx + 1) * t_len].astype(o_ref.dtype)


def _mix_sample(sinks, q, k_rows, v_rows, cache_k, cache_v, qx, cache_mk, cache_mv, b, u, cache_conv, conv_w,
                bias_s, bsz, t_len):
    assert t_len == SUBLANES and bsz % SAMPLE_BATCH_TILE == 0 and SWA_KV_HEADS == 2
    nb = SAMPLE_BATCH_TILE
    n_cache, n_new, n_mem = SWA_KV_HEADS * WINDOW, SWA_KV_HEADS * t_len, X_HEADS * MEM_TOKENS
    row = lambda i: (i, 0)
    rows_of = lambda n, w: pl.BlockSpec((nb * n, w), row)
    return pl.pallas_call(
        functools.partial(_mix_sample_kernel, t_len),
        grid=(bsz // nb,),
        in_specs=[pl.BlockSpec(memory_space=pltpu.SMEM),
                  rows_of(t_len, SWA_WIDTH),
                  rows_of(n_new, HEAD_DIM), rows_of(n_new, HEAD_DIM),
                  rows_of(n_cache, HEAD_DIM), rows_of(n_cache, HEAD_DIM),
                  rows_of(t_len, X_WIDTH),
                  rows_of(n_mem, HEAD_DIM), rows_of(n_mem, HEAD_DIM),
                  rows_of(t_len, CONV_WIDTH), rows_of(t_len, CONV_WIDTH),
                  pl.BlockSpec((nb, CONV_K - 1, CONV_WIDTH), lambda i: (i, 0, 0)),
                  _resident((CONV_K, CONV_WIDTH)),
                  _resident((SWA_Q_HEADS * t_len, SAMPLE_KEY_ROWS))],
        out_specs=(rows_of(t_len, D_MODEL), rows_of(n_cache, HEAD_DIM), rows_of(n_cache, HEAD_DIM)),
        out_shape=(jax.ShapeDtypeStruct((bsz * t_len, D_MODEL), F32),
                   jax.ShapeDtypeStruct((bsz * n_cache, HEAD_DIM), F32),
                   jax.ShapeDtypeStruct((bsz * n_cache, HEAD_DIM), F32)),
        compiler_params=_params("arbitrary"),
        name="mix_sample",
    )(sinks, q, k_rows, v_rows, cache_k, cache_v, qx, cache_mk, cache_mv, b, u, cache_conv, conv_w, bias_s)


def _outproj_kernel(x_ref, y_ref, w_ref, g_ref, x1_ref, h_ref):
    x1 = x_ref[...] + jnp.dot(y_ref[...].astype(BF16), w_ref[...], preferred_element_type=F32)
    x1_ref[...] = x1
    h_ref[...] = _rms(x1, g_ref[...]).astype(BF16)


def _out_projection(x2d, y, w_out, g_mlp):
    n = x2d.shape[0]
    tile = min(ROW_TILE, n)
    row = lambda i: (i, 0)
    return pl.pallas_call(
        _outproj_kernel,
        grid=(n // tile,),
        in_specs=[pl.BlockSpec((tile, D_MODEL), row), pl.BlockSpec((tile, D_MODEL), row),
                  _resident((D_MODEL, D_MODEL)), _resident((1, D_MODEL))],
        out_specs=(pl.BlockSpec((tile, D_MODEL), row), pl.BlockSpec((tile, D_MODEL), row)),
        out_shape=(jax.ShapeDtypeStruct((n, D_MODEL), F32), jax.ShapeDtypeStruct((n, D_MODEL), BF16)),
        compiler_params=_params("arbitrary"),
        name="outproj",
    )(x2d, y, w_out, g_mlp)


def _mlp_kernel(x1_ref, h_ref, wu_ref, wd_ref, o_ref):
    @pl.when(pl.program_id(1) == 0)
    def _():
        o_ref[...] = x1_ref[...]

    a = jnp.maximum(jnp.dot(h_ref[...], wu_ref[...], preferred_element_type=F32), 0.0)
    o_ref[...] += jnp.dot((a * a).astype(BF16), wd_ref[...], preferred_element_type=F32)


def _mlp(x1, h, w_up, w_down):
    n = x1.shape[0]
    tile = min(ROW_TILE, n)
    return pl.pallas_call(
        _mlp_kernel,
        grid=(n // tile, D_FF // FF_TILE),
        in_specs=[pl.BlockSpec((tile, D_MODEL), lambda i, j: (i, 0)),
                  pl.BlockSpec((tile, D_MODEL), lambda i, j: (i, 0)),
                  pl.BlockSpec((D_MODEL, FF_TILE), lambda i, j: (0, j)),
                  pl.BlockSpec((FF_TILE, D_MODEL), lambda i, j: (j, 0))],
        out_specs=pl.BlockSpec((tile, D_MODEL), lambda i, j: (i, 0)),
        out_shape=jax.ShapeDtypeStruct((n, D_MODEL), F32),
        compiler_params=_params("arbitrary", "arbitrary"),
        name="mlp",
    )(x1, h, w_up, w_down)


def kernel(x_prompt, x_sample, mem_prompt, cache_conv, cache_swa_k, cache_swa_v, cache_mem_k, cache_mem_v,
           rel_bias_table, g_mix, w_in, conv_w, g_q_swa, g_k_swa, sinks, g_q_x, g_k_x, g_mem,
           w_mem_k, w_mem_v, w_out, g_mlp, w_up, w_down):
    depth = w_in.shape[0]
    bsz, seq, _ = x_prompt.shape
    dbsz, t_len, _ = x_sample.shape
    xp = x_prompt.reshape(bsz * seq, D_MODEL)
    xs = x_sample.reshape(dbsz * t_len, D_MODEL)
    mem2d = mem_prompt.reshape(bsz * MEM_TOKENS, D_MODEL)
    bias_p, bias_s = _bias_tables(rel_bias_table, t_len)

    outs = [[] for _ in range(8)]
    for l in range(depth):
        vec = lambda a: a[l].reshape(1, -1)
        wi, wo = w_in[l].astype(BF16), w_out[l].astype(BF16)
        wu, wd = w_up[l].astype(BF16), w_down[l].astype(BF16)
        wk, wv = w_mem_k[l].astype(BF16), w_mem_v[l].astype(BF16)
        proj = functools.partial(_projections, g_mix=vec(g_mix), w_in=wi, g_q=vec(g_q_swa), g_k=vec(g_k_swa),
                                 g_qx=vec(g_q_x))

        mk, mv, mkb, mvb = _memory_kv(mem2d, vec(g_mem), wk, wv, vec(g_k_x))
        b, u, q, k, v, kb, vb, qx = proj(xp, q_dtype=BF16)
        y = _mix_prompt(sinks[l], q, kb, vb, qx, mkb, mvb, b, u, conv_w[l], bias_p, bsz, seq)
        x1, h = _out_projection(xp, y, wo, vec(g_mlp))
        xp = _mlp(x1, h, wu, wd)
        outs[0].append(u.reshape(bsz, seq, CONV_WIDTH)[:, seq - (CONV_K - 1):])
        last_window = lambda a: a.reshape(bsz, seq, SWA_KV_WIDTH)[:, seq - WINDOW:].reshape(
            bsz, WINDOW, SWA_KV_HEADS, HEAD_DIM)
        outs[1].append(last_window(k))
        outs[2].append(last_window(v))
        outs[3].append(mk.reshape(bsz, MEM_TOKENS, X_HEADS, HEAD_DIM))
        outs[4].append(mv.reshape(bsz, MEM_TOKENS, X_HEADS, HEAD_DIM))

        head_rows = lambda a: a.reshape(-1, HEAD_DIM)
        b, u, q, k, v, _, _, qx = proj(xs, q_dtype=F32)
        y, sk, sv = _mix_sample(
            sinks[l], q, head_rows(k), head_rows(v), head_rows(cache_swa_k[l]), head_rows(cache_swa_v[l]),
            qx, head_rows(cache_mem_k[l]), head_rows(cache_mem_v[l]),
            b, u, cache_conv[l], conv_w[l], bias_s, dbsz, t_len)
        x1, h = _out_projection(xs, y, wo, vec(g_mlp))
        xs = _mlp(x1, h, wu, wd)
        outs[5].append(u.reshape(dbsz, t_len, CONV_WIDTH)[:, t_len - (CONV_K - 1):])
        outs[6].append(sk.reshape(dbsz, WINDOW, SWA_KV_HEADS, HEAD_DIM))
        outs[7].append(sv.reshape(dbsz, WINDOW, SWA_KV_HEADS, HEAD_DIM))

    return (xp.reshape(bsz, seq, D_MODEL), xs.reshape(dbsz, t_len, D_MODEL)) + tuple(jnp.stack(o) for o in outs)
```

```python
import functools
import math

import numpy as np
import jax
import jax.numpy as jnp
from jax import lax
from jax.experimental import pallas as pl
from jax.experimental.pallas import tpu as pltpu

D_MODEL = 2048
HEAD_DIM = 128
SWA_Q_HEADS = 8
SWA_KV_HEADS = 2
SWA_GROUP = SWA_Q_HEADS // SWA_KV_HEADS
SWA_WIDTH = SWA_Q_HEADS * HEAD_DIM
SWA_KV_WIDTH = SWA_KV_HEADS * HEAD_DIM
X_HEADS = 4
X_WIDTH = X_HEADS * HEAD_DIM
CONV_WIDTH = D_MODEL - SWA_WIDTH - X_WIDTH
CONV_K = 3
WINDOW = 128
NUM_BUCKETS = 32
MAX_DISTANCE = WINDOW
MEM_TOKENS = 256
D_FF = 4 * D_MODEL
EPS = 1e-6
NEG = -1e30
SCALE = HEAD_DIM ** -0.5

OFF_B = 0
OFF_C = CONV_WIDTH
OFF_H = 2 * CONV_WIDTH
OFF_Q = 3 * CONV_WIDTH
OFF_K = OFF_Q + SWA_WIDTH
OFF_V = OFF_K + SWA_KV_WIDTH
OFF_QX = OFF_V + SWA_KV_WIDTH
IN_WIDTH = OFF_QX + X_WIDTH

VMEM_LIMIT_V7X = 56 * 1024 * 1024
SUBLANES = 8

ROW_TILE = 512
MLP_ROW_TILE = 1024
FF_TILE = 512
MLP_OUT_CHUNK = 512
SAMPLE_BATCH_TILE = 8
MIX_BLOCKS = 4

BF16 = jnp.bfloat16
F32 = jnp.float32
NT_DIMS = (((1,), (1,)), ((), ()))


def _params(*sem):
    return pltpu.CompilerParams(dimension_semantics=sem, vmem_limit_bytes=VMEM_LIMIT_V7X)


def _resident(shape):
    nd = len(shape)
    return pl.BlockSpec(shape, lambda *_: (0,) * nd, pipeline_mode=pl.Buffered(1))


def _rms(x, g):
    ms = jnp.mean(x * x, axis=-1, keepdims=True)
    return x * lax.rsqrt(ms + EPS) * g


def _rel_bucket_np(dist):
    n = np.maximum(dist, 0)
    max_exact = NUM_BUCKETS // 2
    nf = np.maximum(n, 1).astype(np.float32)
    large = max_exact + (np.log(nf / np.float32(max_exact)) / np.float32(math.log(MAX_DISTANCE / max_exact))
                         * np.float32(NUM_BUCKETS - max_exact)).astype(np.int32)
    large = np.minimum(large, NUM_BUCKETS - 1)
    return np.where(n < max_exact, n, large).astype(np.int32)


SAMPLE_KEY_ROWS = 3 * WINDOW


def _sample_key_index(t_len):
    c = np.arange(SAMPLE_KEY_ROWS)
    n_cache = SWA_KV_HEADS * WINDOW
    assert n_cache + SWA_KV_HEADS * t_len <= SAMPLE_KEY_ROWS
    pos = np.where(c < n_cache, c // SWA_KV_HEADS, WINDOW + (c - n_cache) // SWA_KV_HEADS)
    return pos, c % SWA_KV_HEADS


def _bias_kernel(tab_ref, bp_ref, bs_ref, op_ref, os_ref):
    bp = bp_ref[...]
    bs = bs_ref[...]
    for hh in range(SWA_Q_HEADS):
        accp = jnp.zeros(bp.shape, F32)
        accs = jnp.zeros(bs.shape, F32)
        for k in range(NUM_BUCKETS):
            t = tab_ref[k * SWA_Q_HEADS + hh]
            accp = jnp.where(bp == k, t, accp)
            accs = jnp.where(bs == k, t, accs)
        h, g = divmod(hh, SWA_GROUP)
        op_ref[h, g * WINDOW:(g + 1) * WINDOW, :] = accp
        os_ref[hh * SUBLANES:(hh + 1) * SUBLANES, :] = accs


def _bias_tables(table, t_len):
    qi = np.arange(WINDOW)[:, None]
    kj = np.arange(2 * WINDOW)[None, :]
    bkt_p = _rel_bucket_np(WINDOW + qi - kj)
    key_pos, _ = _sample_key_index(t_len)
    bkt_s = _rel_bucket_np(np.arange(t_len)[:, None] + WINDOW - key_pos[None, :])
    return pl.pallas_call(
        _bias_kernel,
        out_shape=(jax.ShapeDtypeStruct((SWA_KV_HEADS, SWA_GROUP * WINDOW, 2 * WINDOW), F32),
                   jax.ShapeDtypeStruct((SWA_Q_HEADS * t_len, SAMPLE_KEY_ROWS), F32)),
        in_specs=[pl.BlockSpec(memory_space=pltpu.SMEM),
                  pl.BlockSpec(memory_space=pltpu.VMEM),
                  pl.BlockSpec(memory_space=pltpu.VMEM)],
        out_specs=(pl.BlockSpec(memory_space=pltpu.VMEM), pl.BlockSpec(memory_space=pltpu.VMEM)),
        name="bias",
    )(table.reshape(-1), jnp.asarray(bkt_p), jnp.asarray(bkt_s))


def _memkv_kernel(m_ref, g_ref, wk_ref, wv_ref, gk_ref, mk_ref, mv_ref, mkb_ref, mvb_ref):
    h = _rms(m_ref[...], g_ref[...]).astype(BF16)
    zk = jnp.dot(h, wk_ref[...], preferred_element_type=F32)
    zv = jnp.dot(h, wv_ref[...], preferred_element_type=F32)
    gk = gk_ref[...]
    for hx in range(X_HEADS):
        sl = slice(hx * HEAD_DIM, (hx + 1) * HEAD_DIM)
        mk = _rms(zk[:, sl], gk)
        mk_ref[:, sl] = mk
        mkb_ref[:, sl] = mk.astype(BF16)
    mv_ref[...] = zv
    mvb_ref[...] = zv.astype(BF16)


def _memory_kv(mem2d, g_mem, wk, wv, g_k_x):
    n = mem2d.shape[0]
    tile = MEM_TOKENS
    row = lambda i: (i, 0)
    out = lambda dt: jax.ShapeDtypeStruct((n, X_WIDTH), dt)
    return pl.pallas_call(
        _memkv_kernel,
        grid=(n // tile,),
        in_specs=[pl.BlockSpec((tile, D_MODEL), row), _resident((1, D_MODEL)),
                  _resident((D_MODEL, X_WIDTH)), _resident((D_MODEL, X_WIDTH)), _resident((1, HEAD_DIM))],
        out_specs=tuple(pl.BlockSpec((tile, X_WIDTH), row) for _ in range(4)),
        out_shape=(out(F32), out(F32), out(BF16), out(BF16)),
        compiler_params=_params("arbitrary"),
        name="memkv",
    )(mem2d, g_mem, wk, wv, g_k_x)


def _proj_kernel(x_ref, g_ref, w_ref, gq_ref, gk_ref, gx_ref,
                 b_ref, u_ref, q_ref, k_ref, v_ref, kb_ref, vb_ref, qx_ref):
    h = _rms(x_ref[...], g_ref[...]).astype(BF16)

    def seg(lo, width):
        return jnp.dot(h, w_ref[:, lo:lo + width], preferred_element_type=F32)

    b_ref[...] = seg(OFF_B, CONV_WIDTH)
    u_ref[...] = seg(OFF_C, CONV_WIDTH) * seg(OFF_H, CONV_WIDTH)

    def head_norm(z, g, n_heads, outs):
        for hh in range(n_heads):
            sl = slice(hh * HEAD_DIM, (hh + 1) * HEAD_DIM)
            y = _rms(z[:, sl], g)
            for o in outs:
                o[:, sl] = y.astype(o.dtype)

    head_norm(seg(OFF_Q, SWA_WIDTH), gq_ref[...], SWA_Q_HEADS, (q_ref,))
    head_norm(seg(OFF_K, SWA_KV_WIDTH), gk_ref[...], SWA_KV_HEADS, (k_ref, kb_ref))
    zv = seg(OFF_V, SWA_KV_WIDTH)
    v_ref[...] = zv
    vb_ref[...] = zv.astype(BF16)
    head_norm(seg(OFF_QX, X_WIDTH), gx_ref[...], X_HEADS, (qx_ref,))


def _projections(x2d, g_mix, w_in, g_q, g_k, g_qx, q_dtype):
    n = x2d.shape[0]
    tile = min(ROW_TILE, n)
    row = lambda i: (i, 0)
    widths = (CONV_WIDTH, CONV_WIDTH, SWA_WIDTH, SWA_KV_WIDTH, SWA_KV_WIDTH, SWA_KV_WIDTH, SWA_KV_WIDTH, X_WIDTH)
    dtypes = (F32, F32, q_dtype, F32, F32, BF16, BF16, q_dtype)
    return pl.pallas_call(
        _proj_kernel,
        grid=(n // tile,),
        in_specs=[pl.BlockSpec((tile, D_MODEL), row), _resident((1, D_MODEL)), _resident((D_MODEL, IN_WIDTH)),
                  _resident((1, HEAD_DIM)), _resident((1, HEAD_DIM)), _resident((1, HEAD_DIM))],
        out_specs=tuple(pl.BlockSpec((tile, w), row) for w in widths),
        out_shape=tuple(jax.ShapeDtypeStruct((n, w), dt) for w, dt in zip(widths, dtypes)),
        compiler_params=_params("arbitrary"),
        name="proj",
    )(x2d, g_mix, w_in, g_q, g_k, g_qx)


def _sink_softmax(s, sink_col):
    m = jnp.maximum(jnp.max(s, axis=-1, keepdims=True), sink_col)
    p = jnp.exp(s - m)
    den = jnp.sum(p, axis=-1, keepdims=True) + jnp.exp(sink_col - m)
    return p * (1.0 / den)


def _softmax(s):
    m = jnp.max(s, axis=-1, keepdims=True)
    p = jnp.exp(s - m)
    return p * (1.0 / jnp.sum(p, axis=-1, keepdims=True))


def _mix_prompt_kernel(sink_ref, q_ref, kc_ref, kp_ref, vc_ref, vp_ref, qx_ref, mk_ref, mv_ref,
                       b_ref, uc_ref, up_ref, cw_ref, bias_ref, band_ref, o_ref):
    has_prev = pl.program_id(1) > 0
    rows = MIX_BLOCKS * WINDOW

    u = uc_ref[...]
    prev = jnp.where(has_prev, up_ref[...], 0.0)
    ext = jnp.concatenate([prev, u], axis=0)
    cw = cw_ref[...]
    conv = cw[0:1] * ext[SUBLANES - 2:SUBLANES - 2 + rows]
    conv = conv + cw[1:2] * ext[SUBLANES - 1:SUBLANES - 1 + rows]
    conv = conv + cw[2:3] * u
    o_ref[:, 0:CONV_WIDTH] = (b_ref[...] * conv).astype(o_ref.dtype)

    for j in range(MIX_BLOCKS):
        rsl = slice(j * WINDOW, (j + 1) * WINDOW)
        min_band = jnp.where(has_prev, 0.5, 1.5) if j == 0 else 0.5
        for h in range(SWA_KV_HEADS):
            ksl = slice(h * HEAD_DIM, (h + 1) * HEAD_DIM)
            if j == 0:
                k_all = jnp.concatenate([kp_ref[:, ksl], kc_ref[rsl, ksl]], axis=0)
                v_all = jnp.concatenate([vp_ref[:, ksl], vc_ref[rsl, ksl]], axis=0)
            else:
                k_all = kc_ref[(j - 1) * WINDOW:(j + 1) * WINDOW, ksl]
                v_all = vc_ref[(j - 1) * WINDOW:(j + 1) * WINDOW, ksl]
            for g in range(SWA_GROUP):
                hh = h * SWA_GROUP + g
                q = q_ref[rsl, hh * HEAD_DIM:(hh + 1) * HEAD_DIM]
                s = lax.dot_general(q, k_all, NT_DIMS, preferred_element_type=F32) * SCALE
                s = jnp.where(band_ref[...] > min_band, s + bias_ref[h, g * WINDOW:(g + 1) * WINDOW, :], NEG)
                w = _sink_softmax(s, sink_ref[hh]).astype(BF16)
                col = CONV_WIDTH + hh * HEAD_DIM
                o_ref[rsl, col:col + HEAD_DIM] = jnp.dot(w, v_all, preferred_element_type=F32).astype(o_ref.dtype)

        for hx in range(X_HEADS):
            sl = slice(hx * HEAD_DIM, (hx + 1) * HEAD_DIM)
            s = lax.dot_general(qx_ref[rsl, sl], mk_ref[:, sl], NT_DIMS, preferred_element_type=F32) * SCALE
            w = _softmax(s).astype(BF16)
            col = CONV_WIDTH + SWA_WIDTH + hx * HEAD_DIM
            o_ref[rsl, col:col + HEAD_DIM] = jnp.dot(w, mv_ref[:, sl], preferred_element_type=F32).astype(o_ref.dtype)


def _mix_prompt(sinks, q, kb, vb, qx, mkb, mvb, b, u, conv_w, bias_p, bsz, seq):
    rows = MIX_BLOCKS * WINDOW
    steps = seq // rows
    cur = lambda bi, i: (bi * steps + i, 0)
    prv = lambda bi, i: (jnp.maximum((bi * steps + i) * MIX_BLOCKS - 1, 0), 0)
    prv8 = lambda bi, i: (jnp.maximum((bi * steps + i) * (rows // SUBLANES) - 1, 0), 0)
    per_b = lambda bi, i: (bi, 0)
    dist = WINDOW + np.arange(WINDOW)[:, None] - np.arange(2 * WINDOW)[None, :]
    band = np.where((dist >= 0) & (dist < WINDOW), np.where(np.arange(2 * WINDOW)[None, :] < WINDOW, 1.0, 2.0), 0.0)
    return pl.pallas_call(
        _mix_prompt_kernel,
        grid=(bsz, steps),
        in_specs=[pl.BlockSpec(memory_space=pltpu.SMEM),
                  pl.BlockSpec((rows, SWA_WIDTH), cur),
                  pl.BlockSpec((rows, SWA_KV_WIDTH), cur), pl.BlockSpec((WINDOW, SWA_KV_WIDTH), prv),
                  pl.BlockSpec((rows, SWA_KV_WIDTH), cur), pl.BlockSpec((WINDOW, SWA_KV_WIDTH), prv),
                  pl.BlockSpec((rows, X_WIDTH), cur),
                  pl.BlockSpec((MEM_TOKENS, X_WIDTH), per_b), pl.BlockSpec((MEM_TOKENS, X_WIDTH), per_b),
                  pl.BlockSpec((rows, CONV_WIDTH), cur), pl.BlockSpec((rows, CONV_WIDTH), cur),
                  pl.BlockSpec((SUBLANES, CONV_WIDTH), prv8),
                  _resident((CONV_K, CONV_WIDTH)),
                  _resident((SWA_KV_HEADS, SWA_GROUP * WINDOW, 2 * WINDOW)),
                  _resident((WINDOW, 2 * WINDOW))],
        out_specs=pl.BlockSpec((rows, D_MODEL), cur),
        out_shape=jax.ShapeDtypeStruct((bsz * seq, D_MODEL), BF16),
        compiler_params=_params("arbitrary", "arbitrary"),
        name="mix_prompt",
    )(sinks, q, kb, kb, vb, vb, qx, mkb, mvb, b, u, u, conv_w, bias_p, jnp.asarray(band, F32))


def _mix_sample_kernel(t_len, sink_ref, q_ref, kn_ref, vn_ref, ck_ref, cv_ref, qx_ref, cmk_ref, cmv_ref,
                       b_ref, u_ref, cc_ref, cw_ref, bias_ref, o_ref, sk_ref, sv_ref):
    nb = cc_ref.shape[0]
    n_cache = SWA_KV_HEADS * WINDOW
    n_new = SWA_KV_HEADS * t_len
    n_mem = X_HEADS * MEM_TOKENS
    log_t = int(math.log2(t_len))

    rows = SWA_Q_HEADS * t_len
    r = lax.broadcasted_iota(jnp.int32, (nb * rows, SAMPLE_KEY_ROWS), 0) & (rows - 1)
    c = lax.broadcasted_iota(jnp.int32, (nb * rows, SAMPLE_KEY_ROWS), 1)
    key_pos = jnp.where(c < n_cache, c >> 1, WINDOW + ((c - n_cache) >> 1))
    dist = (r & (t_len - 1)) + WINDOW - key_pos
    valid = (dist >= 0) & (dist < WINDOW) & ((c & (SWA_KV_HEADS - 1)) == (r >> int(math.log2(SWA_GROUP * t_len))))
    bias = jnp.concatenate([bias_ref[...]] * nb, axis=0)
    sink_col = jnp.concatenate([jnp.full((t_len, 1), sink_ref[hh], F32) for hh in range(SWA_Q_HEADS)] * nb, axis=0)
    xrows = X_HEADS * t_len
    xr = lax.broadcasted_iota(jnp.int32, (nb * xrows, n_mem), 0) & (xrows - 1)
    xc = lax.broadcasted_iota(jnp.int32, (nb * xrows, n_mem), 1)
    x_valid = (xc & (X_HEADS - 1)) == (xr >> log_t)
    cw = cw_ref[...]
    trow = lax.broadcasted_iota(jnp.int32, (t_len, CONV_WIDTH), 0)
    zeros_pad = jnp.zeros((SAMPLE_KEY_ROWS - n_cache - n_new, HEAD_DIM), F32)

    s_list, sx_list = [], []
    for bi in range(nb):
        rsl = slice(bi * t_len, (bi + 1) * t_len)
        c0, n0, m0 = bi * n_cache, bi * n_new, bi * n_mem

        for dst, cache, new in ((sk_ref, ck_ref, kn_ref), (sv_ref, cv_ref, vn_ref)):
            dst[c0:c0 + n_cache - n_new, :] = cache[c0 + n_new:c0 + n_cache, :]
            dst[c0 + n_cache - n_new:c0 + n_cache, :] = new[n0:n0 + n_new, :]

        u = u_ref[rsl, :]
        cc = cc_ref[bi]
        cc1 = jnp.broadcast_to(cc[1:2], u.shape)
        cc0 = jnp.broadcast_to(cc[0:1], u.shape)
        u_m1 = jnp.where(trow >= 1, pltpu.roll(u, 1, 0), cc1)
        u_m2 = jnp.where(trow >= 2, pltpu.roll(u, 2, 0), jnp.where(trow == 1, cc1, cc0))
        conv = cw[0:1] * u_m2
        conv = conv + cw[1:2] * u_m1
        conv = conv + cw[2:3] * u
        o_ref[rsl, 0:CONV_WIDTH] = (b_ref[rsl, :] * conv).astype(o_ref.dtype)

        qb = q_ref[rsl, :]
        q_rows = jnp.concatenate([qb[:, hh * HEAD_DIM:(hh + 1) * HEAD_DIM] for hh in range(SWA_Q_HEADS)], axis=0)
        k_all = jnp.concatenate([ck_ref[c0:c0 + n_cache, :], kn_ref[n0:n0 + n_new, :], zeros_pad], axis=0)
        s_list.append(lax.dot_general(q_rows.astype(BF16), k_all.astype(BF16), NT_DIMS, preferred_element_type=F32))
        qxb = qx_ref[rsl, :]
        qx_rows = jnp.concatenate([qxb[:, hx * HEAD_DIM:(hx + 1) * HEAD_DIM] for hx in range(X_HEADS)], axis=0)
        mk = cmk_ref[m0:m0 + n_mem, :].astype(BF16)
        sx_list.append(lax.dot_general(qx_rows.astype(BF16), mk, NT_DIMS, preferred_element_type=F32))

    s = jnp.concatenate(s_list, axis=0) * SCALE
    w = _sink_softmax(jnp.where(valid, s + bias, NEG), sink_col).astype(BF16)
    sx = jnp.concatenate(sx_list, axis=0) * SCALE
    wx = _softmax(jnp.where(x_valid, sx, NEG)).astype(BF16)

    for bi in range(nb):
        rsl = slice(bi * t_len, (bi + 1) * t_len)
        c0, n0, m0 = bi * n_cache, bi * n_new, bi * n_mem
        v_all = jnp.concatenate([cv_ref[c0:c0 + n_cache, :], vn_ref[n0:n0 + n_new, :], zeros_pad], axis=0)
        o = jnp.dot(w[bi * rows:(bi + 1) * rows], v_all.astype(BF16), preferred_element_type=F32)
        for hh in range(SWA_Q_HEADS):
            col = CONV_WIDTH + hh * HEAD_DIM
            o_ref[rsl, col:col + HEAD_DIM] = o[hh * t_len:(hh + 1) * t_len].astype(o_ref.dtype)
        mv = cmv_ref[m0:m0 + n_mem, :].astype(BF16)
        ox = jnp.dot(wx[bi * xrows:(bi + 1) * xrows], mv, preferred_element_type=F32)
        for hx in range(X_HEADS):
            col = CONV_WIDTH + SWA_WIDTH + hx * HEAD_DIM
            o_ref[rsl, col:col + HEAD_DIM] = ox[hx * t_len:(hx + 1) * t_len].astype(o_ref.dtype)


def _mix_sample(sinks, q, k_rows, v_rows, cache_k, cache_v, qx, cache_mk, cache_mv, b, u, cache_conv, conv_w,
                bias_s, bsz, t_len):
    assert t_len == SUBLANES and bsz % SAMPLE_BATCH_TILE == 0 and SWA_KV_HEADS == 2
    nb = SAMPLE_BATCH_TILE
    n_cache, n_new, n_mem = SWA_KV_HEADS * WINDOW, SWA_KV_HEADS * t_len, X_HEADS * MEM_TOKENS
    row = lambda i: (i, 0)
    rows_of = lambda n, w: pl.BlockSpec((nb * n, w), row)
    return pl.pallas_call(
        functools.partial(_mix_sample_kernel, t_len),
        grid=(bsz // nb,),
        in_specs=[pl.BlockSpec(memory_space=pltpu.SMEM),
                  rows_of(t_len, SWA_WIDTH),
                  rows_of(n_new, HEAD_DIM), rows_of(n_new, HEAD_DIM),
                  rows_of(n_cache, HEAD_DIM), rows_of(n_cache, HEAD_DIM),
                  rows_of(t_len, X_WIDTH),
                  rows_of(n_mem, HEAD_DIM), rows_of(n_mem, HEAD_DIM),
                  rows_of(t_len, CONV_WIDTH), rows_of(t_len, CONV_WIDTH),
                  pl.BlockSpec((nb, CONV_K - 1, CONV_WIDTH), lambda i: (i, 0, 0)),
                  _resident((CONV_K, CONV_WIDTH)),
                  _resident((SWA_Q_HEADS * t_len, SAMPLE_KEY_ROWS))],
        out_specs=(rows_of(t_len, D_MODEL), rows_of(n_cache, HEAD_DIM), rows_of(n_cache, HEAD_DIM)),
        out_shape=(jax.ShapeDtypeStruct((bsz * t_len, D_MODEL), F32),
                   jax.ShapeDtypeStruct((bsz * n_cache, HEAD_DIM), F32),
                   jax.ShapeDtypeStruct((bsz * n_cache, HEAD_DIM), F32)),
        compiler_params=_params("arbitrary"),
        name="mix_sample",
    )(sinks, q, k_rows, v_rows, cache_k, cache_v, qx, cache_mk, cache_mv, b, u, cache_conv, conv_w, bias_s)


def _outproj_kernel(x_ref, y_ref, w_ref, g_ref, x1_ref, h_ref):
    x1 = x_ref[...] + jnp.dot(y_ref[...].astype(BF16), w_ref[...], preferred_element_type=F32)
    x1_ref[...] = x1
    h_ref[...] = _rms(x1, g_ref[...]).astype(BF16)


def _out_projection(x2d, y, w_out, g_mlp):
    n = x2d.shape[0]
    tile = min(ROW_TILE, n)
    row = lambda i: (i, 0)
    return pl.pallas_call(
        _outproj_kernel,
        grid=(n // tile,),
        in_specs=[pl.BlockSpec((tile, D_MODEL), row), pl.BlockSpec((tile, D_MODEL), row),
                  _resident((D_MODEL, D_MODEL)), _resident((1, D_MODEL))],
        out_specs=(pl.BlockSpec((tile, D_MODEL), row), pl.BlockSpec((tile, D_MODEL), row)),
        out_shape=(jax.ShapeDtypeStruct((n, D_MODEL), F32), jax.ShapeDtypeStruct((n, D_MODEL), BF16)),
        compiler_params=_params("arbitrary"),
        name="outproj",
    )(x2d, y, w_out, g_mlp)


def _mlp_kernel(x1_hbm, h_ref, wu_ref, wd_ref, o_ref, sem):
    i, j = pl.program_id(0), pl.program_id(1)
    tile = o_ref.shape[0]

    def residual_copy():
        return pltpu.make_async_copy(x1_hbm.at[pl.ds(pl.multiple_of(i * tile, tile), tile), :], o_ref, sem)

    @pl.when(j == 0)
    def _():
        residual_copy().start()

    a = jnp.maximum(jnp.dot(h_ref[...], wu_ref[...].astype(BF16), preferred_element_type=F32), 0.0)
    a = (a * a).astype(BF16)

    @pl.when(j == 0)
    def _():
        residual_copy().wait()

    for n in range(0, D_MODEL, MLP_OUT_CHUNK):
        cols = slice(n, n + MLP_OUT_CHUNK)
        o_ref[:, cols] += jnp.dot(a, wd_ref[:, cols].astype(BF16), preferred_element_type=F32)


def _mlp(x1, h, w_up, w_down, layer):
    n = x1.shape[0]
    tile = min(MLP_ROW_TILE, n)
    return pl.pallas_call(
        _mlp_kernel,
        grid=(n // tile, D_FF // FF_TILE),
        in_specs=[pl.BlockSpec(memory_space=pl.ANY),
                  pl.BlockSpec((tile, D_MODEL), lambda i, j: (i, 0)),
                  pl.BlockSpec((None, D_MODEL, FF_TILE), lambda i, j: (layer, 0, j)),
                  pl.BlockSpec((None, FF_TILE, D_MODEL), lambda i, j: (layer, j, 0))],
        out_specs=pl.BlockSpec((tile, D_MODEL), lambda i, j: (i, 0)),
        out_shape=jax.ShapeDtypeStruct((n, D_MODEL), F32),
        scratch_shapes=[pltpu.SemaphoreType.DMA(())],
        compiler_params=_params("arbitrary", "arbitrary"),
        name="mlp",
    )(x1, h, w_up, w_down)


def kernel(x_prompt, x_sample, mem_prompt, cache_conv, cache_swa_k, cache_swa_v, cache_mem_k, cache_mem_v,
           rel_bias_table, g_mix, w_in, conv_w, g_q_swa, g_k_swa, sinks, g_q_x, g_k_x, g_mem,
           w_mem_k, w_mem_v, w_out, g_mlp, w_up, w_down):
    depth = w_in.shape[0]
    bsz, seq, _ = x_prompt.shape
    dbsz, t_len, _ = x_sample.shape
    xp = x_prompt.reshape(bsz * seq, D_MODEL)
    xs = x_sample.reshape(dbsz * t_len, D_MODEL)
    mem2d = mem_prompt.reshape(bsz * MEM_TOKENS, D_MODEL)
    bias_p, bias_s = _bias_tables(rel_bias_table, t_len)

    outs = [[] for _ in range(8)]
    for l in range(depth):
        vec = lambda a: a[l].reshape(1, -1)
        wi, wo = w_in[l].astype(BF16), w_out[l].astype(BF16)
        wk, wv = w_mem_k[l].astype(BF16), w_mem_v[l].astype(BF16)
        proj = functools.partial(_projections, g_mix=vec(g_mix), w_in=wi, g_q=vec(g_q_swa), g_k=vec(g_k_swa),
                                 g_qx=vec(g_q_x))

        mk, mv, mkb, mvb = _memory_kv(mem2d, vec(g_mem), wk, wv, vec(g_k_x))
        b, u, q, k, v, kb, vb, qx = proj(xp, q_dtype=BF16)
        y = _mix_prompt(sinks[l], q, kb, vb, qx, mkb, mvb, b, u, conv_w[l], bias_p, bsz, seq)
        x1, h = _out_projection(xp, y, wo, vec(g_mlp))
        xp = _mlp(x1, h, w_up, w_down, l)
        outs[0].append(u.reshape(bsz, seq, CONV_WIDTH)[:, seq - (CONV_K - 1):])
        last_window = lambda a: a.reshape(bsz, seq, SWA_KV_WIDTH)[:, seq - WINDOW:].reshape(
            bsz, WINDOW, SWA_KV_HEADS, HEAD_DIM)
        outs[1].append(last_window(k))
        outs[2].append(last_window(v))
        outs[3].append(mk.reshape(bsz, MEM_TOKENS, X_HEADS, HEAD_DIM))
        outs[4].append(mv.reshape(bsz, MEM_TOKENS, X_HEADS, HEAD_DIM))

        head_rows = lambda a: a.reshape(-1, HEAD_DIM)
        b, u, q, k, v, _, _, qx = proj(xs, q_dtype=F32)
        y, sk, sv = _mix_sample(
            sinks[l], q, head_rows(k), head_rows(v), head_rows(cache_swa_k[l]), head_rows(cache_swa_v[l]),
            qx, head_rows(cache_mem_k[l]), head_rows(cache_mem_v[l]),
            b, u, cache_conv[l], conv_w[l], bias_s, dbsz, t_len)
        x1, h = _out_projection(xs, y, wo, vec(g_mlp))
        xs = _mlp(x1, h, w_up, w_down, l)
        outs[5].append(u.reshape(dbsz, t_len, CONV_WIDTH)[:, t_len - (CONV_K - 1):])
        outs[6].append(sk.reshape(dbsz, WINDOW, SWA_KV_HEADS, HEAD_DIM))
        outs[7].append(sv.reshape(dbsz, WINDOW, SWA_KV_HEADS, HEAD_DIM))

    return (xp.reshape(bsz, seq, D_MODEL), xs.reshape(dbsz, t_len, D_MODEL)) + tuple(jnp.stack(o) for o in outs)
```

```python
import functools
import math

import numpy as np
import jax
import jax.numpy as jnp
from jax import lax
from jax.experimental import pallas as pl
from jax.experimental.pallas import tpu as pltpu

D_MODEL = 2048
HEAD_DIM = 128
SWA_Q_HEADS = 8
SWA_KV_HEADS = 2
SWA_GROUP = SWA_Q_HEADS // SWA_KV_HEADS
SWA_WIDTH = SWA_Q_HEADS * HEAD_DIM
SWA_KV_WIDTH = SWA_KV_HEADS * HEAD_DIM
X_HEADS = 4
X_WIDTH = X_HEADS * HEAD_DIM
CONV_WIDTH = D_MODEL - SWA_WIDTH - X_WIDTH
CONV_K = 3
WINDOW = 128
NUM_BUCKETS = 32
MAX_DISTANCE = WINDOW
MEM_TOKENS = 256
D_FF = 4 * D_MODEL
EPS = 1e-6
NEG = -1e30
SCALE = HEAD_DIM ** -0.5

OFF_B = 0
OFF_C = CONV_WIDTH
OFF_H = 2 * CONV_WIDTH
OFF_Q = 3 * CONV_WIDTH
OFF_K = OFF_Q + SWA_WIDTH
OFF_V = OFF_K + SWA_KV_WIDTH
OFF_QX = OFF_V + SWA_KV_WIDTH
IN_WIDTH = OFF_QX + X_WIDTH

VMEM_LIMIT_V7X = 56 * 1024 * 1024
SUBLANES = 8

ROW_TILE = 512
MLP_ROW_TILE = 1024
MLP_FF_TILE = 1024
MLP_CAST_FF_TILE = 512
MLP_OUT_CHUNK = 512
SAMPLE_BATCH_TILE = 8
MIX_BLOCKS = 4

BF16 = jnp.bfloat16
F32 = jnp.float32
NT_DIMS = (((1,), (1,)), ((), ()))


def _params(*sem):
    return pltpu.CompilerParams(dimension_semantics=sem, vmem_limit_bytes=VMEM_LIMIT_V7X)


def _resident(shape):
    nd = len(shape)
    return pl.BlockSpec(shape, lambda *_: (0,) * nd, pipeline_mode=pl.Buffered(1))


def _rms(x, g):
    ms = jnp.mean(x * x, axis=-1, keepdims=True)
    return x * lax.rsqrt(ms + EPS) * g


def _rel_bucket_np(dist):
    n = np.maximum(dist, 0)
    max_exact = NUM_BUCKETS // 2
    nf = np.maximum(n, 1).astype(np.float32)
    large = max_exact + (np.log(nf / np.float32(max_exact)) / np.float32(math.log(MAX_DISTANCE / max_exact))
                         * np.float32(NUM_BUCKETS - max_exact)).astype(np.int32)
    large = np.minimum(large, NUM_BUCKETS - 1)
    return np.where(n < max_exact, n, large).astype(np.int32)


SAMPLE_KEY_ROWS = 3 * WINDOW


def _sample_key_index(t_len):
    c = np.arange(SAMPLE_KEY_ROWS)
    n_cache = SWA_KV_HEADS * WINDOW
    assert n_cache + SWA_KV_HEADS * t_len <= SAMPLE_KEY_ROWS
    pos = np.where(c < n_cache, c // SWA_KV_HEADS, WINDOW + (c - n_cache) // SWA_KV_HEADS)
    return pos, c % SWA_KV_HEADS


def _bias_kernel(tab_ref, bp_ref, bs_ref, op_ref, os_ref):
    bp = bp_ref[...]
    bs = bs_ref[...]
    for hh in range(SWA_Q_HEADS):
        accp = jnp.zeros(bp.shape, F32)
        accs = jnp.zeros(bs.shape, F32)
        for k in range(NUM_BUCKETS):
            t = tab_ref[k * SWA_Q_HEADS + hh]
            accp = jnp.where(bp == k, t, accp)
            accs = jnp.where(bs == k, t, accs)
        h, g = divmod(hh, SWA_GROUP)
        op_ref[h, g * WINDOW:(g + 1) * WINDOW, :] = accp
        os_ref[hh * SUBLANES:(hh + 1) * SUBLANES, :] = accs


def _bias_tables(table, t_len):
    qi = np.arange(WINDOW)[:, None]
    kj = np.arange(2 * WINDOW)[None, :]
    bkt_p = _rel_bucket_np(WINDOW + qi - kj)
    key_pos, _ = _sample_key_index(t_len)
    bkt_s = _rel_bucket_np(np.arange(t_len)[:, None] + WINDOW - key_pos[None, :])
    return pl.pallas_call(
        _bias_kernel,
        out_shape=(jax.ShapeDtypeStruct((SWA_KV_HEADS, SWA_GROUP * WINDOW, 2 * WINDOW), F32),
                   jax.ShapeDtypeStruct((SWA_Q_HEADS * t_len, SAMPLE_KEY_ROWS), F32)),
        in_specs=[pl.BlockSpec(memory_space=pltpu.SMEM),
                  pl.BlockSpec(memory_space=pltpu.VMEM),
                  pl.BlockSpec(memory_space=pltpu.VMEM)],
        out_specs=(pl.BlockSpec(memory_space=pltpu.VMEM), pl.BlockSpec(memory_space=pltpu.VMEM)),
        name="bias",
    )(table.reshape(-1), jnp.asarray(bkt_p), jnp.asarray(bkt_s))


def _memkv_kernel(m_ref, g_ref, wk_ref, wv_ref, gk_ref, mk_ref, mv_ref, mkb_ref, mvb_ref):
    h = _rms(m_ref[...], g_ref[...]).astype(BF16)
    zk = jnp.dot(h, wk_ref[...], preferred_element_type=F32)
    zv = jnp.dot(h, wv_ref[...], preferred_element_type=F32)
    gk = gk_ref[...]
    for hx in range(X_HEADS):
        sl = slice(hx * HEAD_DIM, (hx + 1) * HEAD_DIM)
        mk = _rms(zk[:, sl], gk)
        mk_ref[:, sl] = mk
        mkb_ref[:, sl] = mk.astype(BF16)
    mv_ref[...] = zv
    mvb_ref[...] = zv.astype(BF16)


def _memory_kv(mem2d, g_mem, wk, wv, g_k_x):
    n = mem2d.shape[0]
    tile = MEM_TOKENS
    row = lambda i: (i, 0)
    out = lambda dt: jax.ShapeDtypeStruct((n, X_WIDTH), dt)
    return pl.pallas_call(
        _memkv_kernel,
        grid=(n // tile,),
        in_specs=[pl.BlockSpec((tile, D_MODEL), row), _resident((1, D_MODEL)),
                  _resident((D_MODEL, X_WIDTH)), _resident((D_MODEL, X_WIDTH)), _resident((1, HEAD_DIM))],
        out_specs=tuple(pl.BlockSpec((tile, X_WIDTH), row) for _ in range(4)),
        out_shape=(out(F32), out(F32), out(BF16), out(BF16)),
        compiler_params=_params("arbitrary"),
        name="memkv",
    )(mem2d, g_mem, wk, wv, g_k_x)


def _proj_kernel(x_ref, g_ref, w_ref, gq_ref, gk_ref, gx_ref,
                 b_ref, u_ref, q_ref, k_ref, v_ref, kb_ref, vb_ref, qx_ref):
    h = _rms(x_ref[...], g_ref[...]).astype(BF16)

    def seg(lo, width):
        return jnp.dot(h, w_ref[:, lo:lo + width], preferred_element_type=F32)

    b_ref[...] = seg(OFF_B, CONV_WIDTH)
    u_ref[...] = seg(OFF_C, CONV_WIDTH) * seg(OFF_H, CONV_WIDTH)

    def head_norm(z, g, n_heads, outs):
        for hh in range(n_heads):
            sl = slice(hh * HEAD_DIM, (hh + 1) * HEAD_DIM)
            y = _rms(z[:, sl], g)
            for o in outs:
                o[:, sl] = y.astype(o.dtype)

    head_norm(seg(OFF_Q, SWA_WIDTH), gq_ref[...], SWA_Q_HEADS, (q_ref,))
    head_norm(seg(OFF_K, SWA_KV_WIDTH), gk_ref[...], SWA_KV_HEADS, (k_ref, kb_ref))
    zv = seg(OFF_V, SWA_KV_WIDTH)
    v_ref[...] = zv
    vb_ref[...] = zv.astype(BF16)
    head_norm(seg(OFF_QX, X_WIDTH), gx_ref[...], X_HEADS, (qx_ref,))


def _projections(x2d, g_mix, w_in, g_q, g_k, g_qx, q_dtype):
    n = x2d.shape[0]
    tile = min(ROW_TILE, n)
    row = lambda i: (i, 0)
    widths = (CONV_WIDTH, CONV_WIDTH, SWA_WIDTH, SWA_KV_WIDTH, SWA_KV_WIDTH, SWA_KV_WIDTH, SWA_KV_WIDTH, X_WIDTH)
    dtypes = (F32, F32, q_dtype, F32, F32, BF16, BF16, q_dtype)
    return pl.pallas_call(
        _proj_kernel,
        grid=(n // tile,),
        in_specs=[pl.BlockSpec((tile, D_MODEL), row), _resident((1, D_MODEL)), _resident((D_MODEL, IN_WIDTH)),
                  _resident((1, HEAD_DIM)), _resident((1, HEAD_DIM)), _resident((1, HEAD_DIM))],
        out_specs=tuple(pl.BlockSpec((tile, w), row) for w in widths),
        out_shape=tuple(jax.ShapeDtypeStruct((n, w), dt) for w, dt in zip(widths, dtypes)),
        compiler_params=_params("arbitrary"),
        name="proj",
    )(x2d, g_mix, w_in, g_q, g_k, g_qx)


def _sink_softmax(s, sink_col):
    m = jnp.maximum(jnp.max(s, axis=-1, keepdims=True), sink_col)
    p = jnp.exp(s - m)
    den = jnp.sum(p, axis=-1, keepdims=True) + jnp.exp(sink_col - m)
    return p * (1.0 / den)


def _softmax(s):
    m = jnp.max(s, axis=-1, keepdims=True)
    p = jnp.exp(s - m)
    return p * (1.0 / jnp.sum(p, axis=-1, keepdims=True))


def _mix_prompt_kernel(sink_ref, q_ref, kc_ref, kp_ref, vc_ref, vp_ref, qx_ref, mk_ref, mv_ref,
                       b_ref, uc_ref, up_ref, cw_ref, bias_ref, band_ref, o_ref):
    has_prev = pl.program_id(1) > 0
    rows = MIX_BLOCKS * WINDOW

    u = uc_ref[...]
    prev = jnp.where(has_prev, up_ref[...], 0.0)
    ext = jnp.concatenate([prev, u], axis=0)
    cw = cw_ref[...]
    conv = cw[0:1] * ext[SUBLANES - 2:SUBLANES - 2 + rows]
    conv = conv + cw[1:2] * ext[SUBLANES - 1:SUBLANES - 1 + rows]
    conv = conv + cw[2:3] * u
    o_ref[:, 0:CONV_WIDTH] = (b_ref[...] * conv).astype(o_ref.dtype)

    for j in range(MIX_BLOCKS):
        rsl = slice(j * WINDOW, (j + 1) * WINDOW)
        min_band = jnp.where(has_prev, 0.5, 1.5) if j == 0 else 0.5
        for h in range(SWA_KV_HEADS):
            ksl = slice(h * HEAD_DIM, (h + 1) * HEAD_DIM)
            if j == 0:
                k_all = jnp.concatenate([kp_ref[:, ksl], kc_ref[rsl, ksl]], axis=0)
                v_all = jnp.concatenate([vp_ref[:, ksl], vc_ref[rsl, ksl]], axis=0)
            else:
                k_all = kc_ref[(j - 1) * WINDOW:(j + 1) * WINDOW, ksl]
                v_all = vc_ref[(j - 1) * WINDOW:(j + 1) * WINDOW, ksl]
            for g in range(SWA_GROUP):
                hh = h * SWA_GROUP + g
                q = q_ref[rsl, hh * HEAD_DIM:(hh + 1) * HEAD_DIM]
                s = lax.dot_general(q, k_all, NT_DIMS, preferred_element_type=F32) * SCALE
                s = jnp.where(band_ref[...] > min_band, s + bias_ref[h, g * WINDOW:(g + 1) * WINDOW, :], NEG)
                w = _sink_softmax(s, sink_ref[hh]).astype(BF16)
                col = CONV_WIDTH + hh * HEAD_DIM
                o_ref[rsl, col:col + HEAD_DIM] = jnp.dot(w, v_all, preferred_element_type=F32).astype(o_ref.dtype)

        for hx in range(X_HEADS):
            sl = slice(hx * HEAD_DIM, (hx + 1) * HEAD_DIM)
            s = lax.dot_general(qx_ref[rsl, sl], mk_ref[:, sl], NT_DIMS, preferred_element_type=F32) * SCALE
            w = _softmax(s).astype(BF16)
            col = CONV_WIDTH + SWA_WIDTH + hx * HEAD_DIM
            o_ref[rsl, col:col + HEAD_DIM] = jnp.dot(w, mv_ref[:, sl], preferred_element_type=F32).astype(o_ref.dtype)


def _mix_prompt(sinks, q, kb, vb, qx, mkb, mvb, b, u, conv_w, bias_p, bsz, seq):
    rows = MIX_BLOCKS * WINDOW
    steps = seq // rows
    cur = lambda bi, i: (bi * steps + i, 0)
    prv = lambda bi, i: (jnp.maximum((bi * steps + i) * MIX_BLOCKS - 1, 0), 0)
    prv8 = lambda bi, i: (jnp.maximum((bi * steps + i) * (rows // SUBLANES) - 1, 0), 0)
    per_b = lambda bi, i: (bi, 0)
    dist = WINDOW + np.arange(WINDOW)[:, None] - np.arange(2 * WINDOW)[None, :]
    band = np.where((dist >= 0) & (dist < WINDOW), np.where(np.arange(2 * WINDOW)[None, :] < WINDOW, 1.0, 2.0), 0.0)
    return pl.pallas_call(
        _mix_prompt_kernel,
        grid=(bsz, steps),
        in_specs=[pl.BlockSpec(memory_space=pltpu.SMEM),
                  pl.BlockSpec((rows, SWA_WIDTH), cur),
                  pl.BlockSpec((rows, SWA_KV_WIDTH), cur), pl.BlockSpec((WINDOW, SWA_KV_WIDTH), prv),
                  pl.BlockSpec((rows, SWA_KV_WIDTH), cur), pl.BlockSpec((WINDOW, SWA_KV_WIDTH), prv),
                  pl.BlockSpec((rows, X_WIDTH), cur),
                  pl.BlockSpec((MEM_TOKENS, X_WIDTH), per_b), pl.BlockSpec((MEM_TOKENS, X_WIDTH), per_b),
                  pl.BlockSpec((rows, CONV_WIDTH), cur), pl.BlockSpec((rows, CONV_WIDTH), cur),
                  pl.BlockSpec((SUBLANES, CONV_WIDTH), prv8),
                  _resident((CONV_K, CONV_WIDTH)),
                  _resident((SWA_KV_HEADS, SWA_GROUP * WINDOW, 2 * WINDOW)),
                  _resident((WINDOW, 2 * WINDOW))],
        out_specs=pl.BlockSpec((rows, D_MODEL), cur),
        out_shape=jax.ShapeDtypeStruct((bsz * seq, D_MODEL), BF16),
        compiler_params=_params("arbitrary", "arbitrary"),
        name="mix_prompt",
    )(sinks, q, kb, kb, vb, vb, qx, mkb, mvb, b, u, u, conv_w, bias_p, jnp.asarray(band, F32))


def _mix_sample_kernel(t_len, sink_ref, q_ref, kn_ref, vn_ref, ck_ref, cv_ref, qx_ref, cmk_ref, cmv_ref,
                       b_ref, u_ref, cc_ref, cw_ref, bias_ref, o_ref, sk_ref, sv_ref):
    nb = cc_ref.shape[0]
    n_cache = SWA_KV_HEADS * WINDOW
    n_new = SWA_KV_HEADS * t_len
    n_mem = X_HEADS * MEM_TOKENS
    log_t = int(math.log2(t_len))

    rows = SWA_Q_HEADS * t_len
    r = lax.broadcasted_iota(jnp.int32, (nb * rows, SAMPLE_KEY_ROWS), 0) & (rows - 1)
    c = lax.broadcasted_iota(jnp.int32, (nb * rows, SAMPLE_KEY_ROWS), 1)
    key_pos = jnp.where(c < n_cache, c >> 1, WINDOW + ((c - n_cache) >> 1))
    dist = (r & (t_len - 1)) + WINDOW - key_pos
    valid = (dist >= 0) & (dist < WINDOW) & ((c & (SWA_KV_HEADS - 1)) == (r >> int(math.log2(SWA_GROUP * t_len))))
    bias = jnp.concatenate([bias_ref[...]] * nb, axis=0)
    sink_col = jnp.concatenate([jnp.full((t_len, 1), sink_ref[hh], F32) for hh in range(SWA_Q_HEADS)] * nb, axis=0)
    xrows = X_HEADS * t_len
    xr = lax.broadcasted_iota(jnp.int32, (nb * xrows, n_mem), 0) & (xrows - 1)
    xc = lax.broadcasted_iota(jnp.int32, (nb * xrows, n_mem), 1)
    x_valid = (xc & (X_HEADS - 1)) == (xr >> log_t)
    cw = cw_ref[...]
    trow = lax.broadcasted_iota(jnp.int32, (t_len, CONV_WIDTH), 0)
    zeros_pad = jnp.zeros((SAMPLE_KEY_ROWS - n_cache - n_new, HEAD_DIM), F32)

    s_list, sx_list = [], []
    for bi in range(nb):
        rsl = slice(bi * t_len, (bi + 1) * t_len)
        c0, n0, m0 = bi * n_cache, bi * n_new, bi * n_mem

        for dst, cache, new in ((sk_ref, ck_ref, kn_ref), (sv_ref, cv_ref, vn_ref)):
            dst[c0:c0 + n_cache - n_new, :] = cache[c0 + n_new:c0 + n_cache, :]
            dst[c0 + n_cache - n_new:c0 + n_cache, :] = new[n0:n0 + n_new, :]

        u = u_ref[rsl, :]
        cc = cc_ref[bi]
        cc1 = jnp.broadcast_to(cc[1:2], u.shape)
        cc0 = jnp.broadcast_to(cc[0:1], u.shape)
        u_m1 = jnp.where(trow >= 1, pltpu.roll(u, 1, 0), cc1)
        u_m2 = jnp.where(trow >= 2, pltpu.roll(u, 2, 0), jnp.where(trow == 1, cc1, cc0))
        conv = cw[0:1] * u_m2
        conv = conv + cw[1:2] * u_m1
        conv = conv + cw[2:3] * u
        o_ref[rsl, 0:CONV_WIDTH] = (b_ref[rsl, :] * conv).astype(o_ref.dtype)

        qb = q_ref[rsl, :]
        q_rows = jnp.concatenate([qb[:, hh * HEAD_DIM:(hh + 1) * HEAD_DIM] for hh in range(SWA_Q_HEADS)], axis=0)
        k_all = jnp.concatenate([ck_ref[c0:c0 + n_cache, :], kn_ref[n0:n0 + n_new, :], zeros_pad], axis=0)
        s_list.append(lax.dot_general(q_rows.astype(BF16), k_all.astype(BF16), NT_DIMS, preferred_element_type=F32))
        qxb = qx_ref[rsl, :]
        qx_rows = jnp.concatenate([qxb[:, hx * HEAD_DIM:(hx + 1) * HEAD_DIM] for hx in range(X_HEADS)], axis=0)
        mk = cmk_ref[m0:m0 + n_mem, :].astype(BF16)
        sx_list.append(lax.dot_general(qx_rows.astype(BF16), mk, NT_DIMS, preferred_element_type=F32))

    s = jnp.concatenate(s_list, axis=0) * SCALE
    w = _sink_softmax(jnp.where(valid, s + bias, NEG), sink_col).astype(BF16)
    sx = jnp.concatenate(sx_list, axis=0) * SCALE
    wx = _softmax(jnp.where(x_valid, sx, NEG)).astype(BF16)

    for bi in range(nb):
        rsl = slice(bi * t_len, (bi + 1) * t_len)
        c0, n0, m0 = bi * n_cache, bi * n_new, bi * n_mem
        v_all = jnp.concatenate([cv_ref[c0:c0 + n_cache, :], vn_ref[n0:n0 + n_new, :], zeros_pad], axis=0)
        o = jnp.dot(w[bi * rows:(bi + 1) * rows], v_all.astype(BF16), preferred_element_type=F32)
        for hh in range(SWA_Q_HEADS):
            col = CONV_WIDTH + hh * HEAD_DIM
            o_ref[rsl, col:col + HEAD_DIM] = o[hh * t_len:(hh + 1) * t_len].astype(o_ref.dtype)
        mv = cmv_ref[m0:m0 + n_mem, :].astype(BF16)
        ox = jnp.dot(wx[bi * xrows:(bi + 1) * xrows], mv, preferred_element_type=F32)
        for hx in range(X_HEADS):
            col = CONV_WIDTH + SWA_WIDTH + hx * HEAD_DIM
            o_ref[rsl, col:col + HEAD_DIM] = ox[hx * t_len:(hx + 1) * t_len].astype(o_ref.dtype)


def _mix_sample(sinks, q, k_rows, v_rows, cache_k, cache_v, qx, cache_mk, cache_mv, b, u, cache_conv, conv_w,
                bias_s, bsz, t_len):
    assert t_len == SUBLANES and bsz % SAMPLE_BATCH_TILE == 0 and SWA_KV_HEADS == 2
    nb = SAMPLE_BATCH_TILE
    n_cache, n_new, n_mem = SWA_KV_HEADS * WINDOW, SWA_KV_HEADS * t_len, X_HEADS * MEM_TOKENS
    row = lambda i: (i, 0)
    rows_of = lambda n, w: pl.BlockSpec((nb * n, w), row)
    return pl.pallas_call(
        functools.partial(_mix_sample_kernel, t_len),
        grid=(bsz // nb,),
        in_specs=[pl.BlockSpec(memory_space=pltpu.SMEM),
                  rows_of(t_len, SWA_WIDTH),
                  rows_of(n_new, HEAD_DIM), rows_of(n_new, HEAD_DIM),
                  rows_of(n_cache, HEAD_DIM), rows_of(n_cache, HEAD_DIM),
                  rows_of(t_len, X_WIDTH),
                  rows_of(n_mem, HEAD_DIM), rows_of(n_mem, HEAD_DIM),
                  rows_of(t_len, CONV_WIDTH), rows_of(t_len, CONV_WIDTH),
                  pl.BlockSpec((nb, CONV_K - 1, CONV_WIDTH), lambda i: (i, 0, 0)),
                  _resident((CONV_K, CONV_WIDTH)),
                  _resident((SWA_Q_HEADS * t_len, SAMPLE_KEY_ROWS))],
        out_specs=(rows_of(t_len, D_MODEL), rows_of(n_cache, HEAD_DIM), rows_of(n_cache, HEAD_DIM)),
        out_shape=(jax.ShapeDtypeStruct((bsz * t_len, D_MODEL), F32),
                   jax.ShapeDtypeStruct((bsz * n_cache, HEAD_DIM), F32),
                   jax.ShapeDtypeStruct((bsz * n_cache, HEAD_DIM), F32)),
        compiler_params=_params("arbitrary"),
        name="mix_sample",
    )(sinks, q, k_rows, v_rows, cache_k, cache_v, qx, cache_mk, cache_mv, b, u, cache_conv, conv_w, bias_s)


def _outproj_kernel(x_ref, y_ref, w_ref, g_ref, x1_ref, h_ref):
    x1 = x_ref[...] + jnp.dot(y_ref[...].astype(BF16), w_ref[...], preferred_element_type=F32)
    x1_ref[...] = x1
    h_ref[...] = _rms(x1, g_ref[...]).astype(BF16)


def _out_projection(x2d, y, w_out, g_mlp):
    n = x2d.shape[0]
    tile = min(ROW_TILE, n)
    row = lambda i: (i, 0)
    return pl.pallas_call(
        _outproj_kernel,
        grid=(n // tile,),
        in_specs=[pl.BlockSpec((tile, D_MODEL), row), pl.BlockSpec((tile, D_MODEL), row),
                  _resident((D_MODEL, D_MODEL)), _resident((1, D_MODEL))],
        out_specs=(pl.BlockSpec((tile, D_MODEL), row), pl.BlockSpec((tile, D_MODEL), row)),
        out_shape=(jax.ShapeDtypeStruct((n, D_MODEL), F32), jax.ShapeDtypeStruct((n, D_MODEL), BF16)),
        compiler_params=_params("arbitrary"),
        name="outproj",
    )(x2d, y, w_out, g_mlp)


def _mlp_kernel(x1_hbm, h_ref, wu_ref, wd_ref, o_ref, *rest):
    *bf16_out, sem = rest
    i, j = pl.program_id(0), pl.program_id(1)
    tile = o_ref.shape[0]

    def residual_copy():
        return pltpu.make_async_copy(x1_hbm.at[pl.ds(pl.multiple_of(i * tile, tile), tile), :], o_ref, sem)

    @pl.when(j == 0)
    def _():
        residual_copy().start()

    wu = wu_ref[...].astype(BF16)
    if bf16_out:
        bf16_out[0][...] = wu
    a = jnp.maximum(jnp.dot(h_ref[...], wu, preferred_element_type=F32), 0.0)
    a = (a * a).astype(BF16)

    @pl.when(j == 0)
    def _():
        residual_copy().wait()

    for n in range(0, D_MODEL, MLP_OUT_CHUNK):
        cols = slice(n, n + MLP_OUT_CHUNK)
        wd = wd_ref[:, cols].astype(BF16)
        if bf16_out:
            bf16_out[1][:, cols] = wd
        o_ref[:, cols] += jnp.dot(a, wd, preferred_element_type=F32)


def _mlp(x1, h, w_up, w_down, row_tile, ff_tile, emit_bf16):
    n = x1.shape[0]
    tile = min(row_tile, n)
    up_spec = pl.BlockSpec((D_MODEL, ff_tile), lambda i, j: (0, j))
    down_spec = pl.BlockSpec((ff_tile, D_MODEL), lambda i, j: (j, 0))
    out_specs = [pl.BlockSpec((tile, D_MODEL), lambda i, j: (i, 0))]
    out_shape = [jax.ShapeDtypeStruct((n, D_MODEL), F32)]
    if emit_bf16:
        assert n == tile
        out_specs += [up_spec, down_spec]
        out_shape += [jax.ShapeDtypeStruct(w_up.shape, BF16), jax.ShapeDtypeStruct(w_down.shape, BF16)]
    return pl.pallas_call(
        _mlp_kernel,
        grid=(n // tile, D_FF // ff_tile),
        in_specs=[pl.BlockSpec(memory_space=pl.ANY),
                  pl.BlockSpec((tile, D_MODEL), lambda i, j: (i, 0)), up_spec, down_spec],
        out_specs=tuple(out_specs),
        out_shape=tuple(out_shape),
        scratch_shapes=[pltpu.SemaphoreType.DMA(())],
        compiler_params=_params("arbitrary", "arbitrary"),
        name="mlp_cast" if emit_bf16 else "mlp",
    )(x1, h, w_up, w_down)


def kernel(x_prompt, x_sample, mem_prompt, cache_conv, cache_swa_k, cache_swa_v, cache_mem_k, cache_mem_v,
           rel_bias_table, g_mix, w_in, conv_w, g_q_swa, g_k_swa, sinks, g_q_x, g_k_x, g_mem,
           w_mem_k, w_mem_v, w_out, g_mlp, w_up, w_down):
    depth = w_in.shape[0]
    bsz, seq, _ = x_prompt.shape
    dbsz, t_len, _ = x_sample.shape
    xp = x_prompt.reshape(bsz * seq, D_MODEL)
    xs = x_sample.reshape(dbsz * t_len, D_MODEL)
    mem2d = mem_prompt.reshape(bsz * MEM_TOKENS, D_MODEL)
    bias_p, bias_s = _bias_tables(rel_bias_table, t_len)

    outs = [[] for _ in range(8)]
    for l in range(depth):
        vec = lambda a: a[l].reshape(1, -1)
        wi, wo = w_in[l].astype(BF16), w_out[l].astype(BF16)
        wk, wv = w_mem_k[l].astype(BF16), w_mem_v[l].astype(BF16)
        proj = functools.partial(_projections, g_mix=vec(g_mix), w_in=wi, g_q=vec(g_q_swa), g_k=vec(g_k_swa),
                                 g_qx=vec(g_q_x))

        head_rows = lambda a: a.reshape(-1, HEAD_DIM)
        b, u, q, k, v, _, _, qx = proj(xs, q_dtype=F32)
        y, sk, sv = _mix_sample(
            sinks[l], q, head_rows(k), head_rows(v), head_rows(cache_swa_k[l]), head_rows(cache_swa_v[l]),
            qx, head_rows(cache_mem_k[l]), head_rows(cache_mem_v[l]),
            b, u, cache_conv[l], conv_w[l], bias_s, dbsz, t_len)
        x1, h = _out_projection(xs, y, wo, vec(g_mlp))
        xs, wu, wd = _mlp(x1, h, w_up[l], w_down[l], row_tile=xs.shape[0], ff_tile=MLP_CAST_FF_TILE, emit_bf16=True)
        outs[5].append(u.reshape(dbsz, t_len, CONV_WIDTH)[:, t_len - (CONV_K - 1):])
        outs[6].append(sk.reshape(dbsz, WINDOW, SWA_KV_HEADS, HEAD_DIM))
        outs[7].append(sv.reshape(dbsz, WINDOW, SWA_KV_HEADS, HEAD_DIM))

        mk, mv, mkb, mvb = _memory_kv(mem2d, vec(g_mem), wk, wv, vec(g_k_x))
        b, u, q, k, v, kb, vb, qx = proj(xp, q_dtype=BF16)
        y = _mix_prompt(sinks[l], q, kb, vb, qx, mkb, mvb, b, u, conv_w[l], bias_p, bsz, seq)
        x1, h = _out_projection(xp, y, wo, vec(g_mlp))
        xp, = _mlp(x1, h, wu, wd, row_tile=MLP_ROW_TILE, ff_tile=MLP_FF_TILE, emit_bf16=False)
        outs[0].append(u.reshape(bsz, seq, CONV_WIDTH)[:, seq - (CONV_K - 1):])
        last_window = lambda a: a.reshape(bsz, seq, SWA_KV_WIDTH)[:, seq - WINDOW:].reshape(
            bsz, WINDOW, SWA_KV_HEADS, HEAD_DIM)
        outs[1].append(last_window(k))
        outs[2].append(last_window(v))
        outs[3].append(mk.reshape(bsz, MEM_TOKENS, X_HEADS, HEAD_DIM))
        outs[4].append(mv.reshape(bsz, MEM_TOKENS, X_HEADS, HEAD_DIM))

    return (xp.reshape(bsz, seq, D_MODEL), xs.reshape(dbsz, t_len, D_MODEL)) + tuple(jnp.stack(o) for o in outs)
```

```python
import functools
import math

import numpy as np
import jax
import jax.numpy as jnp
from jax import lax
from jax.experimental import pallas as pl
from jax.experimental.pallas import tpu as pltpu

D_MODEL = 2048
HEAD_DIM = 128
SWA_Q_HEADS = 8
SWA_KV_HEADS = 2
SWA_GROUP = SWA_Q_HEADS // SWA_KV_HEADS
SWA_WIDTH = SWA_Q_HEADS * HEAD_DIM
SWA_KV_WIDTH = SWA_KV_HEADS * HEAD_DIM
X_HEADS = 4
X_WIDTH = X_HEADS * HEAD_DIM
CONV_WIDTH = D_MODEL - SWA_WIDTH - X_WIDTH
CONV_K = 3
WINDOW = 128
NUM_BUCKETS = 32
MAX_DISTANCE = WINDOW
MEM_TOKENS = 256
D_FF = 4 * D_MODEL
EPS = 1e-6
NEG = -1e30
SCALE = HEAD_DIM ** -0.5

OFF_B = 0
OFF_C = CONV_WIDTH
OFF_H = 2 * CONV_WIDTH
OFF_Q = 3 * CONV_WIDTH
OFF_K = OFF_Q + SWA_WIDTH
OFF_V = OFF_K + SWA_KV_WIDTH
OFF_QX = OFF_V + SWA_KV_WIDTH
IN_WIDTH = OFF_QX + X_WIDTH

VMEM_LIMIT_V7X = 56 * 1024 * 1024
SUBLANES = 8

ROW_TILE = 512
MLP_ROW_TILE = 512
MLP_FF_TILE = 1024
MLP_CAST_FF_TILE = 512
SAMPLE_BATCH_TILE = 8
MIX_BLOCKS = 4

BF16 = jnp.bfloat16
F32 = jnp.float32
NT_DIMS = (((1,), (1,)), ((), ()))


def _params(*sem):
    return pltpu.CompilerParams(dimension_semantics=sem, vmem_limit_bytes=VMEM_LIMIT_V7X)


def _resident(shape):
    nd = len(shape)
    return pl.BlockSpec(shape, lambda *_: (0,) * nd, pipeline_mode=pl.Buffered(1))


def _rms(x, g):
    ms = jnp.mean(x * x, axis=-1, keepdims=True)
    return x * lax.rsqrt(ms + EPS) * g


def _rel_bucket_np(dist):
    n = np.maximum(dist, 0)
    max_exact = NUM_BUCKETS // 2
    nf = np.maximum(n, 1).astype(np.float32)
    large = max_exact + (np.log(nf / np.float32(max_exact)) / np.float32(math.log(MAX_DISTANCE / max_exact))
                         * np.float32(NUM_BUCKETS - max_exact)).astype(np.int32)
    large = np.minimum(large, NUM_BUCKETS - 1)
    return np.where(n < max_exact, n, large).astype(np.int32)


SAMPLE_KEY_ROWS = 3 * WINDOW


def _sample_key_index(t_len):
    c = np.arange(SAMPLE_KEY_ROWS)
    n_cache = SWA_KV_HEADS * WINDOW
    assert n_cache + SWA_KV_HEADS * t_len <= SAMPLE_KEY_ROWS
    pos = np.where(c < n_cache, c // SWA_KV_HEADS, WINDOW + (c - n_cache) // SWA_KV_HEADS)
    return pos, c % SWA_KV_HEADS


def _bias_kernel(tab_ref, bp_ref, bs_ref, op_ref, os_ref):
    bp = bp_ref[...]
    bs = bs_ref[...]
    for hh in range(SWA_Q_HEADS):
        accp = jnp.zeros(bp.shape, F32)
        accs = jnp.zeros(bs.shape, F32)
        for k in range(NUM_BUCKETS):
            t = tab_ref[k * SWA_Q_HEADS + hh]
            accp = jnp.where(bp == k, t, accp)
            accs = jnp.where(bs == k, t, accs)
        h, g = divmod(hh, SWA_GROUP)
        op_ref[h, g * WINDOW:(g + 1) * WINDOW, :] = accp
        os_ref[hh * SUBLANES:(hh + 1) * SUBLANES, :] = accs


def _bias_tables(table, t_len):
    qi = np.arange(WINDOW)[:, None]
    kj = np.arange(2 * WINDOW)[None, :]
    bkt_p = _rel_bucket_np(WINDOW + qi - kj)
    key_pos, _ = _sample_key_index(t_len)
    bkt_s = _rel_bucket_np(np.arange(t_len)[:, None] + WINDOW - key_pos[None, :])
    return pl.pallas_call(
        _bias_kernel,
        out_shape=(jax.ShapeDtypeStruct((SWA_KV_HEADS, SWA_GROUP * WINDOW, 2 * WINDOW), F32),
                   jax.ShapeDtypeStruct((SWA_Q_HEADS * t_len, SAMPLE_KEY_ROWS), F32)),
        in_specs=[pl.BlockSpec(memory_space=pltpu.SMEM),
                  pl.BlockSpec(memory_space=pltpu.VMEM),
                  pl.BlockSpec(memory_space=pltpu.VMEM)],
        out_specs=(pl.BlockSpec(memory_space=pltpu.VMEM), pl.BlockSpec(memory_space=pltpu.VMEM)),
        name="bias",
    )(table.reshape(-1), jnp.asarray(bkt_p), jnp.asarray(bkt_s))


def _memkv_kernel(m_ref, g_ref, wk_ref, wv_ref, gk_ref, mk_ref, mv_ref, mkb_ref, mvb_ref):
    h = _rms(m_ref[...], g_ref[...]).astype(BF16)
    zk = jnp.dot(h, wk_ref[...], preferred_element_type=F32)
    zv = jnp.dot(h, wv_ref[...], preferred_element_type=F32)
    gk = gk_ref[...]
    for hx in range(X_HEADS):
        sl = slice(hx * HEAD_DIM, (hx + 1) * HEAD_DIM)
        mk = _rms(zk[:, sl], gk)
        mk_ref[:, sl] = mk
        mkb_ref[:, sl] = mk.astype(BF16)
    mv_ref[...] = zv
    mvb_ref[...] = zv.astype(BF16)


def _memory_kv(mem2d, g_mem, wk, wv, g_k_x):
    n = mem2d.shape[0]
    tile = MEM_TOKENS
    row = lambda i: (i, 0)
    out = lambda dt: jax.ShapeDtypeStruct((n, X_WIDTH), dt)
    return pl.pallas_call(
        _memkv_kernel,
        grid=(n // tile,),
        in_specs=[pl.BlockSpec((tile, D_MODEL), row), _resident((1, D_MODEL)),
                  _resident((D_MODEL, X_WIDTH)), _resident((D_MODEL, X_WIDTH)), _resident((1, HEAD_DIM))],
        out_specs=tuple(pl.BlockSpec((tile, X_WIDTH), row) for _ in range(4)),
        out_shape=(out(F32), out(F32), out(BF16), out(BF16)),
        compiler_params=_params("arbitrary"),
        name="memkv",
    )(mem2d, g_mem, wk, wv, g_k_x)


def _proj_kernel(x_ref, g_ref, w_ref, gq_ref, gk_ref, gx_ref,
                 b_ref, u_ref, q_ref, k_ref, v_ref, kb_ref, vb_ref, qx_ref):
    h = _rms(x_ref[...], g_ref[...]).astype(BF16)

    def seg(lo, width):
        return jnp.dot(h, w_ref[:, lo:lo + width], preferred_element_type=F32)

    b_ref[...] = seg(OFF_B, CONV_WIDTH)
    u_ref[...] = seg(OFF_C, CONV_WIDTH) * seg(OFF_H, CONV_WIDTH)

    def head_norm(z, g, n_heads, outs):
        for hh in range(n_heads):
            sl = slice(hh * HEAD_DIM, (hh + 1) * HEAD_DIM)
            y = _rms(z[:, sl], g)
            for o in outs:
                o[:, sl] = y.astype(o.dtype)

    head_norm(seg(OFF_Q, SWA_WIDTH), gq_ref[...], SWA_Q_HEADS, (q_ref,))
    head_norm(seg(OFF_K, SWA_KV_WIDTH), gk_ref[...], SWA_KV_HEADS, (k_ref, kb_ref))
    zv = seg(OFF_V, SWA_KV_WIDTH)
    v_ref[...] = zv
    vb_ref[...] = zv.astype(BF16)
    head_norm(seg(OFF_QX, X_WIDTH), gx_ref[...], X_HEADS, (qx_ref,))


def _projections(x2d, g_mix, w_in, g_q, g_k, g_qx, q_dtype):
    n = x2d.shape[0]
    tile = min(ROW_TILE, n)
    row = lambda i: (i, 0)
    widths = (CONV_WIDTH, CONV_WIDTH, SWA_WIDTH, SWA_KV_WIDTH, SWA_KV_WIDTH, SWA_KV_WIDTH, SWA_KV_WIDTH, X_WIDTH)
    dtypes = (F32, F32, q_dtype, F32, F32, BF16, BF16, q_dtype)
    return pl.pallas_call(
        _proj_kernel,
        grid=(n // tile,),
        in_specs=[pl.BlockSpec((tile, D_MODEL), row), _resident((1, D_MODEL)), _resident((D_MODEL, IN_WIDTH)),
                  _resident((1, HEAD_DIM)), _resident((1, HEAD_DIM)), _resident((1, HEAD_DIM))],
        out_specs=tuple(pl.BlockSpec((tile, w), row) for w in widths),
        out_shape=tuple(jax.ShapeDtypeStruct((n, w), dt) for w, dt in zip(widths, dtypes)),
        compiler_params=_params("arbitrary"),
        name="proj",
    )(x2d, g_mix, w_in, g_q, g_k, g_qx)


def _sink_softmax(s, sink_col):
    m = jnp.maximum(jnp.max(s, axis=-1, keepdims=True), sink_col)
    p = jnp.exp(s - m)
    den = jnp.sum(p, axis=-1, keepdims=True) + jnp.exp(sink_col - m)
    return p * (1.0 / den)


def _softmax(s):
    m = jnp.max(s, axis=-1, keepdims=True)
    p = jnp.exp(s - m)
    return p * (1.0 / jnp.sum(p, axis=-1, keepdims=True))


def _mix_prompt_kernel(sink_ref, q_ref, kc_ref, kp_ref, vc_ref, vp_ref, qx_ref, mk_ref, mv_ref,
                       b_ref, uc_ref, up_ref, cw_ref, bias_ref, band_ref, o_ref):
    has_prev = pl.program_id(1) > 0
    rows = MIX_BLOCKS * WINDOW

    u = uc_ref[...]
    prev = jnp.where(has_prev, up_ref[...], 0.0)
    ext = jnp.concatenate([prev, u], axis=0)
    cw = cw_ref[...]
    conv = cw[0:1] * ext[SUBLANES - 2:SUBLANES - 2 + rows]
    conv = conv + cw[1:2] * ext[SUBLANES - 1:SUBLANES - 1 + rows]
    conv = conv + cw[2:3] * u
    o_ref[:, 0:CONV_WIDTH] = (b_ref[...] * conv).astype(o_ref.dtype)

    for j in range(MIX_BLOCKS):
        rsl = slice(j * WINDOW, (j + 1) * WINDOW)
        min_band = jnp.where(has_prev, 0.5, 1.5) if j == 0 else 0.5
        for h in range(SWA_KV_HEADS):
            ksl = slice(h * HEAD_DIM, (h + 1) * HEAD_DIM)
            if j == 0:
                k_all = jnp.concatenate([kp_ref[:, ksl], kc_ref[rsl, ksl]], axis=0)
                v_all = jnp.concatenate([vp_ref[:, ksl], vc_ref[rsl, ksl]], axis=0)
            else:
                k_all = kc_ref[(j - 1) * WINDOW:(j + 1) * WINDOW, ksl]
                v_all = vc_ref[(j - 1) * WINDOW:(j + 1) * WINDOW, ksl]
            for g in range(SWA_GROUP):
                hh = h * SWA_GROUP + g
                q = q_ref[rsl, hh * HEAD_DIM:(hh + 1) * HEAD_DIM]
                s = lax.dot_general(q, k_all, NT_DIMS, preferred_element_type=F32) * SCALE
                s = jnp.where(band_ref[...] > min_band, s + bias_ref[h, g * WINDOW:(g + 1) * WINDOW, :], NEG)
                w = _sink_softmax(s, sink_ref[hh]).astype(BF16)
                col = CONV_WIDTH + hh * HEAD_DIM
                o_ref[rsl, col:col + HEAD_DIM] = jnp.dot(w, v_all, preferred_element_type=F32).astype(o_ref.dtype)

        for hx in range(X_HEADS):
            sl = slice(hx * HEAD_DIM, (hx + 1) * HEAD_DIM)
            s = lax.dot_general(qx_ref[rsl, sl], mk_ref[:, sl], NT_DIMS, preferred_element_type=F32) * SCALE
            w = _softmax(s).astype(BF16)
            col = CONV_WIDTH + SWA_WIDTH + hx * HEAD_DIM
            o_ref[rsl, col:col + HEAD_DIM] = jnp.dot(w, mv_ref[:, sl], preferred_element_type=F32).astype(o_ref.dtype)


def _mix_prompt(sinks, q, kb, vb, qx, mkb, mvb, b, u, conv_w, bias_p, bsz, seq):
    rows = MIX_BLOCKS * WINDOW
    steps = seq // rows
    cur = lambda bi, i: (bi * steps + i, 0)
    prv = lambda bi, i: (jnp.maximum((bi * steps + i) * MIX_BLOCKS - 1, 0), 0)
    prv8 = lambda bi, i: (jnp.maximum((bi * steps + i) * (rows // SUBLANES) - 1, 0), 0)
    per_b = lambda bi, i: (bi, 0)
    dist = WINDOW + np.arange(WINDOW)[:, None] - np.arange(2 * WINDOW)[None, :]
    band = np.where((dist >= 0) & (dist < WINDOW), np.where(np.arange(2 * WINDOW)[None, :] < WINDOW, 1.0, 2.0), 0.0)
    return pl.pallas_call(
        _mix_prompt_kernel,
        grid=(bsz, steps),
        in_specs=[pl.BlockSpec(memory_space=pltpu.SMEM),
                  pl.BlockSpec((rows, SWA_WIDTH), cur),
                  pl.BlockSpec((rows, SWA_KV_WIDTH), cur), pl.BlockSpec((WINDOW, SWA_KV_WIDTH), prv),
                  pl.BlockSpec((rows, SWA_KV_WIDTH), cur), pl.BlockSpec((WINDOW, SWA_KV_WIDTH), prv),
                  pl.BlockSpec((rows, X_WIDTH), cur),
                  pl.BlockSpec((MEM_TOKENS, X_WIDTH), per_b), pl.BlockSpec((MEM_TOKENS, X_WIDTH), per_b),
                  pl.BlockSpec((rows, CONV_WIDTH), cur), pl.BlockSpec((rows, CONV_WIDTH), cur),
                  pl.BlockSpec((SUBLANES, CONV_WIDTH), prv8),
                  _resident((CONV_K, CONV_WIDTH)),
                  _resident((SWA_KV_HEADS, SWA_GROUP * WINDOW, 2 * WINDOW)),
                  _resident((WINDOW, 2 * WINDOW))],
        out_specs=pl.BlockSpec((rows, D_MODEL), cur),
        out_shape=jax.ShapeDtypeStruct((bsz * seq, D_MODEL), BF16),
        compiler_params=_params("arbitrary", "arbitrary"),
        name="mix_prompt",
    )(sinks, q, kb, kb, vb, vb, qx, mkb, mvb, b, u, u, conv_w, bias_p, jnp.asarray(band, F32))


def _mix_sample_kernel(t_len, sink_ref, q_ref, kn_ref, vn_ref, ck_ref, cv_ref, qx_ref, cmk_ref, cmv_ref,
                       b_ref, u_ref, cc_ref, cw_ref, bias_ref, o_ref, sk_ref, sv_ref):
    nb = cc_ref.shape[0]
    n_cache = SWA_KV_HEADS * WINDOW
    n_new = SWA_KV_HEADS * t_len
    n_mem = X_HEADS * MEM_TOKENS
    log_t = int(math.log2(t_len))

    rows = SWA_Q_HEADS * t_len
    r = lax.broadcasted_iota(jnp.int32, (nb * rows, SAMPLE_KEY_ROWS), 0) & (rows - 1)
    c = lax.broadcasted_iota(jnp.int32, (nb * rows, SAMPLE_KEY_ROWS), 1)
    key_pos = jnp.where(c < n_cache, c >> 1, WINDOW + ((c - n_cache) >> 1))
    dist = (r & (t_len - 1)) + WINDOW - key_pos
    valid = (dist >= 0) & (dist < WINDOW) & ((c & (SWA_KV_HEADS - 1)) == (r >> int(math.log2(SWA_GROUP * t_len))))
    bias = jnp.concatenate([bias_ref[...]] * nb, axis=0)
    sink_col = jnp.concatenate([jnp.full((t_len, 1), sink_ref[hh], F32) for hh in range(SWA_Q_HEADS)] * nb, axis=0)
    xrows = X_HEADS * t_len
    xr = lax.broadcasted_iota(jnp.int32, (nb * xrows, n_mem), 0) & (xrows - 1)
    xc = lax.broadcasted_iota(jnp.int32, (nb * xrows, n_mem), 1)
    x_valid = (xc & (X_HEADS - 1)) == (xr >> log_t)
    cw = cw_ref[...]
    trow = lax.broadcasted_iota(jnp.int32, (t_len, CONV_WIDTH), 0)
    zeros_pad = jnp.zeros((SAMPLE_KEY_ROWS - n_cache - n_new, HEAD_DIM), F32)

    s_list, sx_list = [], []
    for bi in range(nb):
        rsl = slice(bi * t_len, (bi + 1) * t_len)
        c0, n0, m0 = bi * n_cache, bi * n_new, bi * n_mem

        for dst, cache, new in ((sk_ref, ck_ref, kn_ref), (sv_ref, cv_ref, vn_ref)):
            dst[c0:c0 + n_cache - n_new, :] = cache[c0 + n_new:c0 + n_cache, :]
            dst[c0 + n_cache - n_new:c0 + n_cache, :] = new[n0:n0 + n_new, :]

        u = u_ref[rsl, :]
        cc = cc_ref[bi]
        cc1 = jnp.broadcast_to(cc[1:2], u.shape)
        cc0 = jnp.broadcast_to(cc[0:1], u.shape)
        u_m1 = jnp.where(trow >= 1, pltpu.roll(u, 1, 0), cc1)
        u_m2 = jnp.where(trow >= 2, pltpu.roll(u, 2, 0), jnp.where(trow == 1, cc1, cc0))
        conv = cw[0:1] * u_m2
        conv = conv + cw[1:2] * u_m1
        conv = conv + cw[2:3] * u
        o_ref[rsl, 0:CONV_WIDTH] = (b_ref[rsl, :] * conv).astype(o_ref.dtype)

        qb = q_ref[rsl, :]
        q_rows = jnp.concatenate([qb[:, hh * HEAD_DIM:(hh + 1) * HEAD_DIM] for hh in range(SWA_Q_HEADS)], axis=0)
        k_all = jnp.concatenate([ck_ref[c0:c0 + n_cache, :], kn_ref[n0:n0 + n_new, :], zeros_pad], axis=0)
        s_list.append(lax.dot_general(q_rows.astype(BF16), k_all.astype(BF16), NT_DIMS, preferred_element_type=F32))
        qxb = qx_ref[rsl, :]
        qx_rows = jnp.concatenate([qxb[:, hx * HEAD_DIM:(hx + 1) * HEAD_DIM] for hx in range(X_HEADS)], axis=0)
        mk = cmk_ref[m0:m0 + n_mem, :].astype(BF16)
        sx_list.append(lax.dot_general(qx_rows.astype(BF16), mk, NT_DIMS, preferred_element_type=F32))

    s = jnp.concatenate(s_list, axis=0) * SCALE
    w = _sink_softmax(jnp.where(valid, s + bias, NEG), sink_col).astype(BF16)
    sx = jnp.concatenate(sx_list, axis=0) * SCALE
    wx = _softmax(jnp.where(x_valid, sx, NEG)).astype(BF16)

    for bi in range(nb):
        rsl = slice(bi * t_len, (bi + 1) * t_len)
        c0, n0, m0 = bi * n_cache, bi * n_new, bi * n_mem
        v_all = jnp.concatenate([cv_ref[c0:c0 + n_cache, :], vn_ref[n0:n0 + n_new, :], zeros_pad], axis=0)
        o = jnp.dot(w[bi * rows:(bi + 1) * rows], v_all.astype(BF16), preferred_element_type=F32)
        for hh in range(SWA_Q_HEADS):
            col = CONV_WIDTH + hh * HEAD_DIM
            o_ref[rsl, col:col + HEAD_DIM] = o[hh * t_len:(hh + 1) * t_len].astype(o_ref.dtype)
        mv = cmv_ref[m0:m0 + n_mem, :].astype(BF16)
        ox = jnp.dot(wx[bi * xrows:(bi + 1) * xrows], mv, preferred_element_type=F32)
        for hx in range(X_HEADS):
            col = CONV_WIDTH + SWA_WIDTH + hx * HEAD_DIM
            o_ref[rsl, col:col + HEAD_DIM] = ox[hx * t_len:(hx + 1) * t_len].astype(o_ref.dtype)


def _mix_sample(sinks, q, k_rows, v_rows, cache_k, cache_v, qx, cache_mk, cache_mv, b, u, cache_conv, conv_w,
                bias_s, bsz, t_len):
    assert t_len == SUBLANES and bsz % SAMPLE_BATCH_TILE == 0 and SWA_KV_HEADS == 2
    nb = SAMPLE_BATCH_TILE
    n_cache, n_new, n_mem = SWA_KV_HEADS * WINDOW, SWA_KV_HEADS * t_len, X_HEADS * MEM_TOKENS
    row = lambda i: (i, 0)
    rows_of = lambda n, w: pl.BlockSpec((nb * n, w), row)
    return pl.pallas_call(
        functools.partial(_mix_sample_kernel, t_len),
        grid=(bsz // nb,),
        in_specs=[pl.BlockSpec(memory_space=pltpu.SMEM),
                  rows_of(t_len, SWA_WIDTH),
                  rows_of(n_new, HEAD_DIM), rows_of(n_new, HEAD_DIM),
                  rows_of(n_cache, HEAD_DIM), rows_of(n_cache, HEAD_DIM),
                  rows_of(t_len, X_WIDTH),
                  rows_of(n_mem, HEAD_DIM), rows_of(n_mem, HEAD_DIM),
                  rows_of(t_len, CONV_WIDTH), rows_of(t_len, CONV_WIDTH),
                  pl.BlockSpec((nb, CONV_K - 1, CONV_WIDTH), lambda i: (i, 0, 0)),
                  _resident((CONV_K, CONV_WIDTH)),
                  _resident((SWA_Q_HEADS * t_len, SAMPLE_KEY_ROWS))],
        out_specs=(rows_of(t_len, D_MODEL), rows_of(n_cache, HEAD_DIM), rows_of(n_cache, HEAD_DIM)),
        out_shape=(jax.ShapeDtypeStruct((bsz * t_len, D_MODEL), F32),
                   jax.ShapeDtypeStruct((bsz * n_cache, HEAD_DIM), F32),
                   jax.ShapeDtypeStruct((bsz * n_cache, HEAD_DIM), F32)),
        compiler_params=_params("arbitrary"),
        name="mix_sample",
    )(sinks, q, k_rows, v_rows, cache_k, cache_v, qx, cache_mk, cache_mv, b, u, cache_conv, conv_w, bias_s)


def _outproj_kernel(x_ref, y_ref, w_ref, g_ref, x1_ref, h_ref):
    x1 = x_ref[...] + jnp.dot(y_ref[...].astype(BF16), w_ref[...], preferred_element_type=F32)
    x1_ref[...] = x1
    h_ref[...] = _rms(x1, g_ref[...]).astype(BF16)


def _out_projection(x2d, y, w_out, g_mlp):
    n = x2d.shape[0]
    tile = min(ROW_TILE, n)
    row = lambda i: (i, 0)
    return pl.pallas_call(
        _outproj_kernel,
        grid=(n // tile,),
        in_specs=[pl.BlockSpec((tile, D_MODEL), row), pl.BlockSpec((tile, D_MODEL), row),
                  _resident((D_MODEL, D_MODEL)), _resident((1, D_MODEL))],
        out_specs=(pl.BlockSpec((tile, D_MODEL), row), pl.BlockSpec((tile, D_MODEL), row)),
        out_shape=(jax.ShapeDtypeStruct((n, D_MODEL), F32), jax.ShapeDtypeStruct((n, D_MODEL), BF16)),
        compiler_params=_params("arbitrary"),
        name="outproj",
    )(x2d, y, w_out, g_mlp)


def _mlp_kernel(x1_ref, h_ref, wu_ref, wd_ref, o_ref):
    @pl.when(pl.program_id(1) == 0)
    def _():
        o_ref[...] = x1_ref[...]

    a = jnp.maximum(jnp.dot(h_ref[...], wu_ref[...], preferred_element_type=F32), 0.0)
    o_ref[...] += jnp.dot((a * a).astype(BF16), wd_ref[...], preferred_element_type=F32)


def _mlp_cast_kernel(x1_hbm, h_ref, wu_ref, wd_ref, o_ref, wub_ref, wdb_ref, sem):
    j = pl.program_id(1)
    residual_copy = pltpu.make_async_copy(x1_hbm, o_ref, sem)

    @pl.when(j == 0)
    def _():
        residual_copy.start()

    wu = wu_ref[...].astype(BF16)
    wub_ref[...] = wu
    a = jnp.maximum(jnp.dot(h_ref[...], wu, preferred_element_type=F32), 0.0)
    a = (a * a).astype(BF16)
    wd = wd_ref[...].astype(BF16)
    wdb_ref[...] = wd

    @pl.when(j == 0)
    def _():
        residual_copy.wait()

    o_ref[...] += jnp.dot(a, wd, preferred_element_type=F32)


def _mlp(x1, h, w_up, w_down):
    n = x1.shape[0]
    tile = min(MLP_ROW_TILE, n)
    rows = pl.BlockSpec((tile, D_MODEL), lambda i, j: (i, 0))
    return pl.pallas_call(
        _mlp_kernel,
        grid=(n // tile, D_FF // MLP_FF_TILE),
        in_specs=[rows, rows,
                  pl.BlockSpec((D_MODEL, MLP_FF_TILE), lambda i, j: (0, j)),
                  pl.BlockSpec((MLP_FF_TILE, D_MODEL), lambda i, j: (j, 0))],
        out_specs=rows,
        out_shape=jax.ShapeDtypeStruct((n, D_MODEL), F32),
        compiler_params=_params("arbitrary", "arbitrary"),
        name="mlp",
    )(x1, h, w_up, w_down)


def _mlp_cast(x1, h, w_up, w_down):
    n = x1.shape[0]
    rows = pl.BlockSpec((n, D_MODEL), lambda i, j: (0, 0))
    up_spec = pl.BlockSpec((D_MODEL, MLP_CAST_FF_TILE), lambda i, j: (0, j))
    down_spec = pl.BlockSpec((MLP_CAST_FF_TILE, D_MODEL), lambda i, j: (j, 0))
    return pl.pallas_call(
        _mlp_cast_kernel,
        grid=(1, D_FF // MLP_CAST_FF_TILE),
        in_specs=[pl.BlockSpec(memory_space=pl.ANY), rows, up_spec, down_spec],
        out_specs=(rows, up_spec, down_spec),
        out_shape=(jax.ShapeDtypeStruct((n, D_MODEL), F32),
                   jax.ShapeDtypeStruct(w_up.shape, BF16), jax.ShapeDtypeStruct(w_down.shape, BF16)),
        scratch_shapes=[pltpu.SemaphoreType.DMA(())],
        compiler_params=_params("arbitrary", "arbitrary"),
        name="mlp_cast",
    )(x1, h, w_up, w_down)


def kernel(x_prompt, x_sample, mem_prompt, cache_conv, cache_swa_k, cache_swa_v, cache_mem_k, cache_mem_v,
           rel_bias_table, g_mix, w_in, conv_w, g_q_swa, g_k_swa, sinks, g_q_x, g_k_x, g_mem,
           w_mem_k, w_mem_v, w_out, g_mlp, w_up, w_down):
    depth = w_in.shape[0]
    bsz, seq, _ = x_prompt.shape
    dbsz, t_len, _ = x_sample.shape
    xp = x_prompt.reshape(bsz * seq, D_MODEL)
    xs = x_sample.reshape(dbsz * t_len, D_MODEL)
    mem2d = mem_prompt.reshape(bsz * MEM_TOKENS, D_MODEL)
    bias_p, bias_s = _bias_tables(rel_bias_table, t_len)

    outs = [[] for _ in range(8)]
    for l in range(depth):
        vec = lambda a: a[l].reshape(1, -1)
        wi, wo = w_in[l].astype(BF16), w_out[l].astype(BF16)
        wk, wv = w_mem_k[l].astype(BF16), w_mem_v[l].astype(BF16)
        proj = functools.partial(_projections, g_mix=vec(g_mix), w_in=wi, g_q=vec(g_q_swa), g_k=vec(g_k_swa),
                                 g_qx=vec(g_q_x))

        head_rows = lambda a: a.reshape(-1, HEAD_DIM)
        b, u, q, k, v, _, _, qx = proj(xs, q_dtype=F32)
        y, sk, sv = _mix_sample(
            sinks[l], q, head_rows(k), head_rows(v), head_rows(cache_swa_k[l]), head_rows(cache_swa_v[l]),
            qx, head_rows(cache_mem_k[l]), head_rows(cache_mem_v[l]),
            b, u, cache_conv[l], conv_w[l], bias_s, dbsz, t_len)
        x1, h = _out_projection(xs, y, wo, vec(g_mlp))
        xs, wu, wd = _mlp_cast(x1, h, w_up[l], w_down[l])
        outs[5].append(u.reshape(dbsz, t_len, CONV_WIDTH)[:, t_len - (CONV_K - 1):])
        outs[6].append(sk.reshape(dbsz, WINDOW, SWA_KV_HEADS, HEAD_DIM))
        outs[7].append(sv.reshape(dbsz, WINDOW, SWA_KV_HEADS, HEAD_DIM))

        mk, mv, mkb, mvb = _memory_kv(mem2d, vec(g_mem), wk, wv, vec(g_k_x))
        b, u, q, k, v, kb, vb, qx = proj(xp, q_dtype=BF16)
        y = _mix_prompt(sinks[l], q, kb, vb, qx, mkb, mvb, b, u, conv_w[l], bias_p, bsz, seq)
        x1, h = _out_projection(xp, y, wo, vec(g_mlp))
        xp = _mlp(x1, h, wu, wd)
        outs[0].append(u.reshape(bsz, seq, CONV_WIDTH)[:, seq - (CONV_K - 1):])
        last_window = lambda a: a.reshape(bsz, seq, SWA_KV_WIDTH)[:, seq - WINDOW:].reshape(
            bsz, WINDOW, SWA_KV_HEADS, HEAD_DIM)
        outs[1].append(last_window(k))
        outs[2].append(last_window(v))
        outs[3].append(mk.reshape(bsz, MEM_TOKENS, X_HEADS, HEAD_DIM))
        outs[4].append(mv.reshape(bsz, MEM_TOKENS, X_HEADS, HEAD_DIM))

    return (xp.reshape(bsz, seq, D_MODEL), xs.reshape(dbsz, t_len, D_MODEL)) + tuple(jnp.stack(o) for o in outs)
```

```python
import functools
import math

import numpy as np
import jax
import jax.numpy as jnp
from jax import lax
from jax.experimental import pallas as pl
from jax.experimental.pallas import tpu as pltpu

D_MODEL = 2048
HEAD_DIM = 128
SWA_Q_HEADS = 8
SWA_KV_HEADS = 2
SWA_GROUP = SWA_Q_HEADS // SWA_KV_HEADS
SWA_WIDTH = SWA_Q_HEADS * HEAD_DIM
SWA_KV_WIDTH = SWA_KV_HEADS * HEAD_DIM
X_HEADS = 4
X_WIDTH = X_HEADS * HEAD_DIM
CONV_WIDTH = D_MODEL - SWA_WIDTH - X_WIDTH
CONV_K = 3
WINDOW = 128
NUM_BUCKETS = 32
MAX_DISTANCE = WINDOW
MEM_TOKENS = 256
D_FF = 4 * D_MODEL
EPS = 1e-6
NEG = -1e30
SCALE = HEAD_DIM ** -0.5

OFF_B = 0
OFF_C = CONV_WIDTH
OFF_H = 2 * CONV_WIDTH
OFF_Q = 3 * CONV_WIDTH
OFF_K = OFF_Q + SWA_WIDTH
OFF_V = OFF_K + SWA_KV_WIDTH
OFF_QX = OFF_V + SWA_KV_WIDTH
IN_WIDTH = OFF_QX + X_WIDTH

VMEM_LIMIT_V7X = 56 * 1024 * 1024
SUBLANES = 8

ROW_TILE = 512
ROW_CHUNK = 256
MLP_ROW_TILE = 512
MLP_FF_TILE = 1024
MLP_CAST_FF_TILE = 512
SAMPLE_BATCH_TILE = 8
MIX_BLOCKS = 4

BF16 = jnp.bfloat16
F32 = jnp.float32
NT_DIMS = (((1,), (1,)), ((), ()))


def _params(*sem):
    return pltpu.CompilerParams(dimension_semantics=sem, vmem_limit_bytes=VMEM_LIMIT_V7X)


def _resident(shape):
    nd = len(shape)
    return pl.BlockSpec(shape, lambda *_: (0,) * nd, pipeline_mode=pl.Buffered(1))


def _rms(x, g):
    ms = jnp.mean(x * x, axis=-1, keepdims=True)
    return x * lax.rsqrt(ms + EPS) * g


def _rel_bucket_np(dist):
    n = np.maximum(dist, 0)
    max_exact = NUM_BUCKETS // 2
    nf = np.maximum(n, 1).astype(np.float32)
    large = max_exact + (np.log(nf / np.float32(max_exact)) / np.float32(math.log(MAX_DISTANCE / max_exact))
                         * np.float32(NUM_BUCKETS - max_exact)).astype(np.int32)
    large = np.minimum(large, NUM_BUCKETS - 1)
    return np.where(n < max_exact, n, large).astype(np.int32)


SAMPLE_KEY_ROWS = 3 * WINDOW


def _sample_key_index(t_len):
    c = np.arange(SAMPLE_KEY_ROWS)
    n_cache = SWA_KV_HEADS * WINDOW
    assert n_cache + SWA_KV_HEADS * t_len <= SAMPLE_KEY_ROWS
    pos = np.where(c < n_cache, c // SWA_KV_HEADS, WINDOW + (c - n_cache) // SWA_KV_HEADS)
    return pos, c % SWA_KV_HEADS


def _bias_kernel(tab_ref, bp_ref, bs_ref, op_ref, os_ref):
    bp = bp_ref[...]
    bs = bs_ref[...]
    for hh in range(SWA_Q_HEADS):
        accp = jnp.zeros(bp.shape, F32)
        accs = jnp.zeros(bs.shape, F32)
        for k in range(NUM_BUCKETS):
            t = tab_ref[k * SWA_Q_HEADS + hh]
            accp = jnp.where(bp == k, t, accp)
            accs = jnp.where(bs == k, t, accs)
        h, g = divmod(hh, SWA_GROUP)
        op_ref[h, g * WINDOW:(g + 1) * WINDOW, :] = accp
        os_ref[hh * SUBLANES:(hh + 1) * SUBLANES, :] = accs


def _bias_tables(table, t_len):
    qi = np.arange(WINDOW)[:, None]
    kj = np.arange(2 * WINDOW)[None, :]
    bkt_p = _rel_bucket_np(WINDOW + qi - kj)
    key_pos, _ = _sample_key_index(t_len)
    bkt_s = _rel_bucket_np(np.arange(t_len)[:, None] + WINDOW - key_pos[None, :])
    return pl.pallas_call(
        _bias_kernel,
        out_shape=(jax.ShapeDtypeStruct((SWA_KV_HEADS, SWA_GROUP * WINDOW, 2 * WINDOW), F32),
                   jax.ShapeDtypeStruct((SWA_Q_HEADS * t_len, SAMPLE_KEY_ROWS), F32)),
        in_specs=[pl.BlockSpec(memory_space=pltpu.SMEM),
                  pl.BlockSpec(memory_space=pltpu.VMEM),
                  pl.BlockSpec(memory_space=pltpu.VMEM)],
        out_specs=(pl.BlockSpec(memory_space=pltpu.VMEM), pl.BlockSpec(memory_space=pltpu.VMEM)),
        name="bias",
    )(table.reshape(-1), jnp.asarray(bkt_p), jnp.asarray(bkt_s))


def _memkv_kernel(m_ref, g_ref, wk_ref, wv_ref, gk_ref, mk_ref, mv_ref, mkb_ref, mvb_ref):
    h = _rms(m_ref[...], g_ref[...]).astype(BF16)
    zk = jnp.dot(h, wk_ref[...], preferred_element_type=F32)
    zv = jnp.dot(h, wv_ref[...], preferred_element_type=F32)
    gk = gk_ref[...]
    for hx in range(X_HEADS):
        sl = slice(hx * HEAD_DIM, (hx + 1) * HEAD_DIM)
        mk = _rms(zk[:, sl], gk)
        mk_ref[:, sl] = mk
        mkb_ref[:, sl] = mk.astype(BF16)
    mv_ref[...] = zv
    mvb_ref[...] = zv.astype(BF16)


def _memory_kv(mem2d, g_mem, wk, wv, g_k_x):
    n = mem2d.shape[0]
    tile = MEM_TOKENS
    row = lambda i: (i, 0)
    out = lambda dt: jax.ShapeDtypeStruct((n, X_WIDTH), dt)
    return pl.pallas_call(
        _memkv_kernel,
        grid=(n // tile,),
        in_specs=[pl.BlockSpec((tile, D_MODEL), row), _resident((1, D_MODEL)),
                  _resident((D_MODEL, X_WIDTH)), _resident((D_MODEL, X_WIDTH)), _resident((1, HEAD_DIM))],
        out_specs=tuple(pl.BlockSpec((tile, X_WIDTH), row) for _ in range(4)),
        out_shape=(out(F32), out(F32), out(BF16), out(BF16)),
        compiler_params=_params("arbitrary"),
        name="memkv",
    )(mem2d, g_mem, wk, wv, g_k_x)


def _proj_kernel(x_ref, g_ref, w_ref, gq_ref, gk_ref, gx_ref,
                 b_ref, u_ref, q_ref, k_ref, v_ref, kb_ref, vb_ref, qx_ref):
    for r0 in range(0, x_ref.shape[0], ROW_CHUNK):
        rows = slice(r0, r0 + ROW_CHUNK)
        h = _rms(x_ref[rows, :], g_ref[...]).astype(BF16)

        def seg(lo, width):
            return jnp.dot(h, w_ref[:, lo:lo + width], preferred_element_type=F32)

        b_ref[rows, :] = seg(OFF_B, CONV_WIDTH)
        u_ref[rows, :] = seg(OFF_C, CONV_WIDTH) * seg(OFF_H, CONV_WIDTH)

        def head_norm(z, g, n_heads, outs):
            for hh in range(n_heads):
                sl = slice(hh * HEAD_DIM, (hh + 1) * HEAD_DIM)
                y = _rms(z[:, sl], g)
                for o in outs:
                    o[rows, sl] = y.astype(o.dtype)

        head_norm(seg(OFF_Q, SWA_WIDTH), gq_ref[...], SWA_Q_HEADS, (q_ref,))
        head_norm(seg(OFF_K, SWA_KV_WIDTH), gk_ref[...], SWA_KV_HEADS, (k_ref, kb_ref))
        zv = seg(OFF_V, SWA_KV_WIDTH)
        v_ref[rows, :] = zv
        vb_ref[rows, :] = zv.astype(BF16)
        head_norm(seg(OFF_QX, X_WIDTH), gx_ref[...], X_HEADS, (qx_ref,))


def _projections(x2d, g_mix, w_in, g_q, g_k, g_qx, q_dtype):
    n = x2d.shape[0]
    tile = min(ROW_TILE, n)
    row = lambda i: (i, 0)
    widths = (CONV_WIDTH, CONV_WIDTH, SWA_WIDTH, SWA_KV_WIDTH, SWA_KV_WIDTH, SWA_KV_WIDTH, SWA_KV_WIDTH, X_WIDTH)
    dtypes = (F32, F32, q_dtype, F32, F32, BF16, BF16, q_dtype)
    return pl.pallas_call(
        _proj_kernel,
        grid=(n // tile,),
        in_specs=[pl.BlockSpec((tile, D_MODEL), row), _resident((1, D_MODEL)), _resident((D_MODEL, IN_WIDTH)),
                  _resident((1, HEAD_DIM)), _resident((1, HEAD_DIM)), _resident((1, HEAD_DIM))],
        out_specs=tuple(pl.BlockSpec((tile, w), row) for w in widths),
        out_shape=tuple(jax.ShapeDtypeStruct((n, w), dt) for w, dt in zip(widths, dtypes)),
        compiler_params=_params("arbitrary"),
        name="proj",
    )(x2d, g_mix, w_in, g_q, g_k, g_qx)


def _sink_softmax(s, sink_col):
    m = jnp.maximum(jnp.max(s, axis=-1, keepdims=True), sink_col)
    p = jnp.exp(s - m)
    den = jnp.sum(p, axis=-1, keepdims=True) + jnp.exp(sink_col - m)
    return p * (1.0 / den)


def _softmax(s):
    m = jnp.max(s, axis=-1, keepdims=True)
    p = jnp.exp(s - m)
    return p * (1.0 / jnp.sum(p, axis=-1, keepdims=True))


def _mix_out_kernel(sink_ref, q_ref, kc_ref, kp_ref, vc_ref, vp_ref, qx_ref, mk_ref, mv_ref,
                    b_ref, uc_ref, up_ref, cw_ref, bias_ref, band_ref, x_ref, w_ref, g_ref, x1_ref, h_ref):
    has_prev = pl.program_id(1) > 0
    rows = MIX_BLOCKS * WINDOW
    blocks = [slice(j * WINDOW, (j + 1) * WINDOW) for j in range(MIX_BLOCKS)]

    def project(y_cols, col0, acc):
        y = jnp.concatenate(y_cols, axis=1).astype(BF16)
        return acc + jnp.dot(y, w_ref[col0:col0 + y.shape[1], :], preferred_element_type=F32)

    u = uc_ref[...]
    prev = jnp.where(has_prev, up_ref[...], 0.0)
    ext = jnp.concatenate([prev, u], axis=0)
    cw = cw_ref[...]
    conv = cw[0:1] * ext[SUBLANES - 2:SUBLANES - 2 + rows]
    conv = conv + cw[1:2] * ext[SUBLANES - 1:SUBLANES - 1 + rows]
    conv = conv + cw[2:3] * u
    x1 = project([b_ref[...] * conv], 0, x_ref[...])

    for h in range(SWA_KV_HEADS):
        ksl = slice(h * HEAD_DIM, (h + 1) * HEAD_DIM)
        outs = [[None] * MIX_BLOCKS for _ in range(SWA_GROUP)]
        for j, rsl in enumerate(blocks):
            min_band = jnp.where(has_prev, 0.5, 1.5) if j == 0 else 0.5
            if j == 0:
                k_all = jnp.concatenate([kp_ref[:, ksl], kc_ref[rsl, ksl]], axis=0)
                v_all = jnp.concatenate([vp_ref[:, ksl], vc_ref[rsl, ksl]], axis=0)
            else:
                k_all = kc_ref[(j - 1) * WINDOW:(j + 1) * WINDOW, ksl]
                v_all = vc_ref[(j - 1) * WINDOW:(j + 1) * WINDOW, ksl]
            for g in range(SWA_GROUP):
                hh = h * SWA_GROUP + g
                q = q_ref[rsl, hh * HEAD_DIM:(hh + 1) * HEAD_DIM]
                s = lax.dot_general(q, k_all, NT_DIMS, preferred_element_type=F32) * SCALE
                s = jnp.where(band_ref[...] > min_band, s + bias_ref[h, g * WINDOW:(g + 1) * WINDOW, :], NEG)
                w = _sink_softmax(s, sink_ref[hh]).astype(BF16)
                outs[g][j] = jnp.dot(w, v_all, preferred_element_type=F32).astype(BF16)
        x1 = project([jnp.concatenate(o, axis=0) for o in outs], CONV_WIDTH + h * SWA_GROUP * HEAD_DIM, x1)

    outs = [[None] * MIX_BLOCKS for _ in range(X_HEADS)]
    for j, rsl in enumerate(blocks):
        for hx in range(X_HEADS):
            sl = slice(hx * HEAD_DIM, (hx + 1) * HEAD_DIM)
            s = lax.dot_general(qx_ref[rsl, sl], mk_ref[:, sl], NT_DIMS, preferred_element_type=F32) * SCALE
            w = _softmax(s).astype(BF16)
            outs[hx][j] = jnp.dot(w, mv_ref[:, sl], preferred_element_type=F32).astype(BF16)
    x1 = project([jnp.concatenate(o, axis=0) for o in outs], CONV_WIDTH + SWA_WIDTH, x1)

    x1_ref[...] = x1
    h_ref[...] = _rms(x1, g_ref[...]).astype(BF16)


def _mix_out_prompt(sinks, q, kb, vb, qx, mkb, mvb, b, u, conv_w, bias_p, x2d, w_out, g_mlp, bsz, seq):
    rows = MIX_BLOCKS * WINDOW
    steps = seq // rows
    cur = lambda bi, i: (bi * steps + i, 0)
    prv = lambda bi, i: (jnp.maximum((bi * steps + i) * MIX_BLOCKS - 1, 0), 0)
    prv8 = lambda bi, i: (jnp.maximum((bi * steps + i) * (rows // SUBLANES) - 1, 0), 0)
    per_b = lambda bi, i: (bi, 0)
    dist = WINDOW + np.arange(WINDOW)[:, None] - np.arange(2 * WINDOW)[None, :]
    band = np.where((dist >= 0) & (dist < WINDOW), np.where(np.arange(2 * WINDOW)[None, :] < WINDOW, 1.0, 2.0), 0.0)
    return pl.pallas_call(
        _mix_out_kernel,
        grid=(bsz, steps),
        in_specs=[pl.BlockSpec(memory_space=pltpu.SMEM),
                  pl.BlockSpec((rows, SWA_WIDTH), cur),
                  pl.BlockSpec((rows, SWA_KV_WIDTH), cur), pl.BlockSpec((WINDOW, SWA_KV_WIDTH), prv),
                  pl.BlockSpec((rows, SWA_KV_WIDTH), cur), pl.BlockSpec((WINDOW, SWA_KV_WIDTH), prv),
                  pl.BlockSpec((rows, X_WIDTH), cur),
                  pl.BlockSpec((MEM_TOKENS, X_WIDTH), per_b), pl.BlockSpec((MEM_TOKENS, X_WIDTH), per_b),
                  pl.BlockSpec((rows, CONV_WIDTH), cur), pl.BlockSpec((rows, CONV_WIDTH), cur),
                  pl.BlockSpec((SUBLANES, CONV_WIDTH), prv8),
                  _resident((CONV_K, CONV_WIDTH)),
                  _resident((SWA_KV_HEADS, SWA_GROUP * WINDOW, 2 * WINDOW)),
                  _resident((WINDOW, 2 * WINDOW)),
                  pl.BlockSpec((rows, D_MODEL), cur),
                  _resident((D_MODEL, D_MODEL)), _resident((1, D_MODEL))],
        out_specs=(pl.BlockSpec((rows, D_MODEL), cur), pl.BlockSpec((rows, D_MODEL), cur)),
        out_shape=(jax.ShapeDtypeStruct((bsz * seq, D_MODEL), F32), jax.ShapeDtypeStruct((bsz * seq, D_MODEL), BF16)),
        compiler_params=_params("arbitrary", "arbitrary"),
        name="mix_out",
    )(sinks, q, kb, kb, vb, vb, qx, mkb, mvb, b, u, u, conv_w, bias_p, jnp.asarray(band, F32), x2d, w_out, g_mlp)


def _mix_sample_kernel(t_len, sink_ref, q_ref, kn_ref, vn_ref, ck_ref, cv_ref, qx_ref, cmk_ref, cmv_ref,
                       b_ref, u_ref, cc_ref, cw_ref, bias_ref, o_ref, sk_ref, sv_ref):
    nb = cc_ref.shape[0]
    n_cache = SWA_KV_HEADS * WINDOW
    n_new = SWA_KV_HEADS * t_len
    n_mem = X_HEADS * MEM_TOKENS
    log_t = int(math.log2(t_len))

    rows = SWA_Q_HEADS * t_len
    r = lax.broadcasted_iota(jnp.int32, (nb * rows, SAMPLE_KEY_ROWS), 0) & (rows - 1)
    c = lax.broadcasted_iota(jnp.int32, (nb * rows, SAMPLE_KEY_ROWS), 1)
    key_pos = jnp.where(c < n_cache, c >> 1, WINDOW + ((c - n_cache) >> 1))
    dist = (r & (t_len - 1)) + WINDOW - key_pos
    valid = (dist >= 0) & (dist < WINDOW) & ((c & (SWA_KV_HEADS - 1)) == (r >> int(math.log2(SWA_GROUP * t_len))))
    bias = jnp.concatenate([bias_ref[...]] * nb, axis=0)
    sink_col = jnp.concatenate([jnp.full((t_len, 1), sink_ref[hh], F32) for hh in range(SWA_Q_HEADS)] * nb, axis=0)
    xrows = X_HEADS * t_len
    xr = lax.broadcasted_iota(jnp.int32, (nb * xrows, n_mem), 0) & (xrows - 1)
    xc = lax.broadcasted_iota(jnp.int32, (nb * xrows, n_mem), 1)
    x_valid = (xc & (X_HEADS - 1)) == (xr >> log_t)
    cw = cw_ref[...]
    trow = lax.broadcasted_iota(jnp.int32, (t_len, CONV_WIDTH), 0)
    zeros_pad = jnp.zeros((SAMPLE_KEY_ROWS - n_cache - n_new, HEAD_DIM), F32)

    s_list, sx_list = [], []
    for bi in range(nb):
        rsl = slice(bi * t_len, (bi + 1) * t_len)
        c0, n0, m0 = bi * n_cache, bi * n_new, bi * n_mem

        for dst, cache, new in ((sk_ref, ck_ref, kn_ref), (sv_ref, cv_ref, vn_ref)):
            dst[c0:c0 + n_cache - n_new, :] = cache[c0 + n_new:c0 + n_cache, :]
            dst[c0 + n_cache - n_new:c0 + n_cache, :] = new[n0:n0 + n_new, :]

        u = u_ref[rsl, :]
        cc = cc_ref[bi]
        cc1 = jnp.broadcast_to(cc[1:2], u.shape)
        cc0 = jnp.broadcast_to(cc[0:1], u.shape)
        u_m1 = jnp.where(trow >= 1, pltpu.roll(u, 1, 0), cc1)
        u_m2 = jnp.where(trow >= 2, pltpu.roll(u, 2, 0), jnp.where(trow == 1, cc1, cc0))
        conv = cw[0:1] * u_m2
        conv = conv + cw[1:2] * u_m1
        conv = conv + cw[2:3] * u
        o_ref[rsl, 0:CONV_WIDTH] = (b_ref[rsl, :] * conv).astype(o_ref.dtype)

        qb = q_ref[rsl, :]
        q_rows = jnp.concatenate([qb[:, hh * HEAD_DIM:(hh + 1) * HEAD_DIM] for hh in range(SWA_Q_HEADS)], axis=0)
        k_all = jnp.concatenate([ck_ref[c0:c0 + n_cache, :], kn_ref[n0:n0 + n_new, :], zeros_pad], axis=0)
        s_list.append(lax.dot_general(q_rows.astype(BF16), k_all.astype(BF16), NT_DIMS, preferred_element_type=F32))
        qxb = qx_ref[rsl, :]
        qx_rows = jnp.concatenate([qxb[:, hx * HEAD_DIM:(hx + 1) * HEAD_DIM] for hx in range(X_HEADS)], axis=0)
        mk = cmk_ref[m0:m0 + n_mem, :].astype(BF16)
        sx_list.append(lax.dot_general(qx_rows.astype(BF16), mk, NT_DIMS, preferred_element_type=F32))

    s = jnp.concatenate(s_list, axis=0) * SCALE
    w = _sink_softmax(jnp.where(valid, s + bias, NEG), sink_col).astype(BF16)
    sx = jnp.concatenate(sx_list, axis=0) * SCALE
    wx = _softmax(jnp.where(x_valid, sx, NEG)).astype(BF16)

    for bi in range(nb):
        rsl = slice(bi * t_len, (bi + 1) * t_len)
        c0, n0, m0 = bi * n_cache, bi * n_new, bi * n_mem
        v_all = jnp.concatenate([cv_ref[c0:c0 + n_cache, :], vn_ref[n0:n0 + n_new, :], zeros_pad], axis=0)
        o = jnp.dot(w[bi * rows:(bi + 1) * rows], v_all.astype(BF16), preferred_element_type=F32)
        for hh in range(SWA_Q_HEADS):
            col = CONV_WIDTH + hh * HEAD_DIM
            o_ref[rsl, col:col + HEAD_DIM] = o[hh * t_len:(hh + 1) * t_len].astype(o_ref.dtype)
        mv = cmv_ref[m0:m0 + n_mem, :].astype(BF16)
        ox = jnp.dot(wx[bi * xrows:(bi + 1) * xrows], mv, preferred_element_type=F32)
        for hx in range(X_HEADS):
            col = CONV_WIDTH + SWA_WIDTH + hx * HEAD_DIM
            o_ref[rsl, col:col + HEAD_DIM] = ox[hx * t_len:(hx + 1) * t_len].astype(o_ref.dtype)


def _mix_sample(sinks, q, k_rows, v_rows, cache_k, cache_v, qx, cache_mk, cache_mv, b, u, cache_conv, conv_w,
                bias_s, bsz, t_len):
    assert t_len == SUBLANES and bsz % SAMPLE_BATCH_TILE == 0 and SWA_KV_HEADS == 2
    nb = SAMPLE_BATCH_TILE
    n_cache, n_new, n_mem = SWA_KV_HEADS * WINDOW, SWA_KV_HEADS * t_len, X_HEADS * MEM_TOKENS
    row = lambda i: (i, 0)
    rows_of = lambda n, w: pl.BlockSpec((nb * n, w), row)
    return pl.pallas_call(
        functools.partial(_mix_sample_kernel, t_len),
        grid=(bsz // nb,),
        in_specs=[pl.BlockSpec(memory_space=pltpu.SMEM),
                  rows_of(t_len, SWA_WIDTH),
                  rows_of(n_new, HEAD_DIM), rows_of(n_new, HEAD_DIM),
                  rows_of(n_cache, HEAD_DIM), rows_of(n_cache, HEAD_DIM),
                  rows_of(t_len, X_WIDTH),
                  rows_of(n_mem, HEAD_DIM), rows_of(n_mem, HEAD_DIM),
                  rows_of(t_len, CONV_WIDTH), rows_of(t_len, CONV_WIDTH),
                  pl.BlockSpec((nb, CONV_K - 1, CONV_WIDTH), lambda i: (i, 0, 0)),
                  _resident((CONV_K, CONV_WIDTH)),
                  _resident((SWA_Q_HEADS * t_len, SAMPLE_KEY_ROWS))],
        out_specs=(rows_of(t_len, D_MODEL), rows_of(n_cache, HEAD_DIM), rows_of(n_cache, HEAD_DIM)),
        out_shape=(jax.ShapeDtypeStruct((bsz * t_len, D_MODEL), F32),
                   jax.ShapeDtypeStruct((bsz * n_cache, HEAD_DIM), F32),
                   jax.ShapeDtypeStruct((bsz * n_cache, HEAD_DIM), F32)),
        compiler_params=_params("arbitrary"),
        name="mix_sample",
    )(sinks, q, k_rows, v_rows, cache_k, cache_v, qx, cache_mk, cache_mv, b, u, cache_conv, conv_w, bias_s)


def _outproj_kernel(x_ref, y_ref, w_ref, g_ref, x1_ref, h_ref):
    for r0 in range(0, x_ref.shape[0], ROW_CHUNK):
        rows = slice(r0, r0 + ROW_CHUNK)
        x1 = x_ref[rows, :] + jnp.dot(y_ref[rows, :].astype(BF16), w_ref[...], preferred_element_type=F32)
        x1_ref[rows, :] = x1
        h_ref[rows, :] = _rms(x1, g_ref[...]).astype(BF16)


def _out_projection(x2d, y, w_out, g_mlp):
    n = x2d.shape[0]
    tile = min(ROW_TILE, n)
    row = lambda i: (i, 0)
    return pl.pallas_call(
        _outproj_kernel,
        grid=(n // tile,),
        in_specs=[pl.BlockSpec((tile, D_MODEL), row), pl.BlockSpec((tile, D_MODEL), row),
                  _resident((D_MODEL, D_MODEL)), _resident((1, D_MODEL))],
        out_specs=(pl.BlockSpec((tile, D_MODEL), row), pl.BlockSpec((tile, D_MODEL), row)),
        out_shape=(jax.ShapeDtypeStruct((n, D_MODEL), F32), jax.ShapeDtypeStruct((n, D_MODEL), BF16)),
        compiler_params=_params("arbitrary"),
        name="outproj",
    )(x2d, y, w_out, g_mlp)


def _mlp_kernel(x1_ref, h_ref, wu_ref, wd_ref, o_ref):
    @pl.when(pl.program_id(1) == 0)
    def _():
        o_ref[...] = x1_ref[...]

    a = jnp.maximum(jnp.dot(h_ref[...], wu_ref[...], preferred_element_type=F32), 0.0)
    o_ref[...] += jnp.dot((a * a).astype(BF16), wd_ref[...], preferred_element_type=F32)


def _mlp_cast_kernel(x1_hbm, h_ref, wu_ref, wd_ref, o_ref, wub_ref, wdb_ref, sem):
    j = pl.program_id(1)
    residual_copy = pltpu.make_async_copy(x1_hbm, o_ref, sem)

    @pl.when(j == 0)
    def _():
        residual_copy.start()

    wu = wu_ref[...].astype(BF16)
    wub_ref[...] = wu
    a = jnp.maximum(jnp.dot(h_ref[...], wu, preferred_element_type=F32), 0.0)
    a = (a * a).astype(BF16)

    @pl.when(j == 0)
    def _():
        residual_copy.wait()

    for c0 in range(0, D_MODEL, MLP_CAST_FF_TILE):
        cols = slice(c0, c0 + MLP_CAST_FF_TILE)
        wd = wd_ref[:, cols].astype(BF16)
        wdb_ref[:, cols] = wd
        o_ref[:, cols] += jnp.dot(a, wd, preferred_element_type=F32)


def _mlp(x1, h, w_up, w_down):
    n = x1.shape[0]
    tile = min(MLP_ROW_TILE, n)
    rows = pl.BlockSpec((tile, D_MODEL), lambda i, j: (i, 0))
    return pl.pallas_call(
        _mlp_kernel,
        grid=(n // tile, D_FF // MLP_FF_TILE),
        in_specs=[rows, rows,
                  pl.BlockSpec((D_MODEL, MLP_FF_TILE), lambda i, j: (0, j)),
                  pl.BlockSpec((MLP_FF_TILE, D_MODEL), lambda i, j: (j, 0))],
        out_specs=rows,
        out_shape=jax.ShapeDtypeStruct((n, D_MODEL), F32),
        compiler_params=_params("arbitrary", "arbitrary"),
        name="mlp",
    )(x1, h, w_up, w_down)


def _mlp_cast(x1, h, w_up, w_down):
    n = x1.shape[0]
    rows = pl.BlockSpec((n, D_MODEL), lambda i, j: (0, 0))
    up_spec = pl.BlockSpec((D_MODEL, MLP_CAST_FF_TILE), lambda i, j: (0, j))
    down_spec = pl.BlockSpec((MLP_CAST_FF_TILE, D_MODEL), lambda i, j: (j, 0))
    return pl.pallas_call(
        _mlp_cast_kernel,
        grid=(1, D_FF // MLP_CAST_FF_TILE),
        in_specs=[pl.BlockSpec(memory_space=pl.ANY), rows, up_spec, down_spec],
        out_specs=(rows, up_spec, down_spec),
        out_shape=(jax.ShapeDtypeStruct((n, D_MODEL), F32),
                   jax.ShapeDtypeStruct(w_up.shape, BF16), jax.ShapeDtypeStruct(w_down.shape, BF16)),
        scratch_shapes=[pltpu.SemaphoreType.DMA(())],
        compiler_params=_params("arbitrary", "arbitrary"),
        name="mlp_cast",
    )(x1, h, w_up, w_down)


def kernel(x_prompt, x_sample, mem_prompt, cache_conv, cache_swa_k, cache_swa_v, cache_mem_k, cache_mem_v,
           rel_bias_table, g_mix, w_in, conv_w, g_q_swa, g_k_swa, sinks, g_q_x, g_k_x, g_mem,
           w_mem_k, w_mem_v, w_out, g_mlp, w_up, w_down):
    depth = w_in.shape[0]
    bsz, seq, _ = x_prompt.shape
    dbsz, t_len, _ = x_sample.shape
    xp = x_prompt.reshape(bsz * seq, D_MODEL)
    xs = x_sample.reshape(dbsz * t_len, D_MODEL)
    mem2d = mem_prompt.reshape(bsz * MEM_TOKENS, D_MODEL)
    bias_p, bias_s = _bias_tables(rel_bias_table, t_len)

    outs = [[] for _ in range(8)]
    for l in range(depth):
        vec = lambda a: a[l].reshape(1, -1)
        wi, wo = w_in[l].astype(BF16), w_out[l].astype(BF16)
        wk, wv = w_mem_k[l].astype(BF16), w_mem_v[l].astype(BF16)
        proj = functools.partial(_projections, g_mix=vec(g_mix), w_in=wi, g_q=vec(g_q_swa), g_k=vec(g_k_swa),
                                 g_qx=vec(g_q_x))

        head_rows = lambda a: a.reshape(-1, HEAD_DIM)
        b, u, q, k, v, _, _, qx = proj(xs, q_dtype=F32)
        y, sk, sv = _mix_sample(
            sinks[l], q, head_rows(k), head_rows(v), head_rows(cache_swa_k[l]), head_rows(cache_swa_v[l]),
            qx, head_rows(cache_mem_k[l]), head_rows(cache_mem_v[l]),
            b, u, cache_conv[l], conv_w[l], bias_s, dbsz, t_len)
        x1, h = _out_projection(xs, y, wo, vec(g_mlp))
        xs, wu, wd = _mlp_cast(x1, h, w_up[l], w_down[l])
        outs[5].append(u.reshape(dbsz, t_len, CONV_WIDTH)[:, t_len - (CONV_K - 1):])
        outs[6].append(sk.reshape(dbsz, WINDOW, SWA_KV_HEADS, HEAD_DIM))
        outs[7].append(sv.reshape(dbsz, WINDOW, SWA_KV_HEADS, HEAD_DIM))

        mk, mv, mkb, mvb = _memory_kv(mem2d, vec(g_mem), wk, wv, vec(g_k_x))
        b, u, q, k, v, kb, vb, qx = proj(xp, q_dtype=BF16)
        x1, h = _mix_out_prompt(sinks[l], q, kb, vb, qx, mkb, mvb, b, u, conv_w[l], bias_p, xp, wo, vec(g_mlp),
                                bsz, seq)
        xp = _mlp(x1, h, wu, wd)
        outs[0].append(u.reshape(bsz, seq, CONV_WIDTH)[:, seq - (CONV_K - 1):])
        last_window = lambda a: a.reshape(bsz, seq, SWA_KV_WIDTH)[:, seq - WINDOW:].reshape(
            bsz, WINDOW, SWA_KV_HEADS, HEAD_DIM)
        outs[1].append(last_window(k))
        outs[2].append(last_window(v))
        outs[3].append(mk.reshape(bsz, MEM_TOKENS, X_HEADS, HEAD_DIM))
        outs[4].append(mv.reshape(bsz, MEM_TOKENS, X_HEADS, HEAD_DIM))

    return (xp.reshape(bsz, seq, D_MODEL), xs.reshape(dbsz, t_len, D_MODEL)) + tuple(jnp.stack(o) for o in outs)
```

```python
import functools
import math

import numpy as np
import jax
import jax.numpy as jnp
from jax import lax
from jax.experimental import pallas as pl
from jax.experimental.pallas import tpu as pltpu

D_MODEL = 2048
HEAD_DIM = 128
SWA_Q_HEADS = 8
SWA_KV_HEADS = 2
SWA_GROUP = SWA_Q_HEADS // SWA_KV_HEADS
SWA_WIDTH = SWA_Q_HEADS * HEAD_DIM
SWA_KV_WIDTH = SWA_KV_HEADS * HEAD_DIM
X_HEADS = 4
X_WIDTH = X_HEADS * HEAD_DIM
CONV_WIDTH = D_MODEL - SWA_WIDTH - X_WIDTH
CONV_K = 3
WINDOW = 128
NUM_BUCKETS = 32
MAX_DISTANCE = WINDOW
MEM_TOKENS = 256
D_FF = 4 * D_MODEL
EPS = 1e-6
NEG = -1e30
SCALE = HEAD_DIM ** -0.5

OFF_B = 0
OFF_C = CONV_WIDTH
OFF_H = 2 * CONV_WIDTH
OFF_Q = 3 * CONV_WIDTH
OFF_K = OFF_Q + SWA_WIDTH
OFF_V = OFF_K + SWA_KV_WIDTH
OFF_QX = OFF_V + SWA_KV_WIDTH
IN_WIDTH = OFF_QX + X_WIDTH

VMEM_LIMIT_V7X = 56 * 1024 * 1024
SUBLANES = 8

ROW_TILE = 512
ROW_CHUNK = 256
MLP_ROW_TILE = 512
MLP_FF_TILE = 1024
MLP_CAST_FF_TILE = 512
SAMPLE_BATCH_TILE = 8
MIX_BLOCKS = 4

BF16 = jnp.bfloat16
F32 = jnp.float32
NT_DIMS = (((1,), (1,)), ((), ()))


def _params(*sem):
    return pltpu.CompilerParams(dimension_semantics=sem, vmem_limit_bytes=VMEM_LIMIT_V7X)


def _resident(shape):
    nd = len(shape)
    return pl.BlockSpec(shape, lambda *_: (0,) * nd, pipeline_mode=pl.Buffered(1))


def _rms(x, g):
    ms = jnp.mean(x * x, axis=-1, keepdims=True)
    return x * lax.rsqrt(ms + EPS) * g


def _rel_bucket_np(dist):
    n = np.maximum(dist, 0)
    max_exact = NUM_BUCKETS // 2
    nf = np.maximum(n, 1).astype(np.float32)
    large = max_exact + (np.log(nf / np.float32(max_exact)) / np.float32(math.log(MAX_DISTANCE / max_exact))
                         * np.float32(NUM_BUCKETS - max_exact)).astype(np.int32)
    large = np.minimum(large, NUM_BUCKETS - 1)
    return np.where(n < max_exact, n, large).astype(np.int32)


SAMPLE_KEY_ROWS = 3 * WINDOW


def _sample_key_index(t_len):
    c = np.arange(SAMPLE_KEY_ROWS)
    n_cache = SWA_KV_HEADS * WINDOW
    assert n_cache + SWA_KV_HEADS * t_len <= SAMPLE_KEY_ROWS
    pos = np.where(c < n_cache, c // SWA_KV_HEADS, WINDOW + (c - n_cache) // SWA_KV_HEADS)
    return pos, c % SWA_KV_HEADS


def _bias_kernel(tab_ref, bp_ref, bs_ref, op_ref, os_ref):
    bp = bp_ref[...]
    bs = bs_ref[...]
    for hh in range(SWA_Q_HEADS):
        accp = jnp.zeros(bp.shape, F32)
        accs = jnp.zeros(bs.shape, F32)
        for k in range(NUM_BUCKETS):
            t = tab_ref[k * SWA_Q_HEADS + hh]
            accp = jnp.where(bp == k, t, accp)
            accs = jnp.where(bs == k, t, accs)
        h, g = divmod(hh, SWA_GROUP)
        op_ref[h, g * WINDOW:(g + 1) * WINDOW, :] = accp
        os_ref[hh * SUBLANES:(hh + 1) * SUBLANES, :] = accs


def _bias_tables(table, t_len):
    qi = np.arange(WINDOW)[:, None]
    kj = np.arange(2 * WINDOW)[None, :]
    bkt_p = _rel_bucket_np(WINDOW + qi - kj)
    key_pos, _ = _sample_key_index(t_len)
    bkt_s = _rel_bucket_np(np.arange(t_len)[:, None] + WINDOW - key_pos[None, :])
    return pl.pallas_call(
        _bias_kernel,
        out_shape=(jax.ShapeDtypeStruct((SWA_KV_HEADS, SWA_GROUP * WINDOW, 2 * WINDOW), F32),
                   jax.ShapeDtypeStruct((SWA_Q_HEADS * t_len, SAMPLE_KEY_ROWS), F32)),
        in_specs=[pl.BlockSpec(memory_space=pltpu.SMEM),
                  pl.BlockSpec(memory_space=pltpu.VMEM),
                  pl.BlockSpec(memory_space=pltpu.VMEM)],
        out_specs=(pl.BlockSpec(memory_space=pltpu.VMEM), pl.BlockSpec(memory_space=pltpu.VMEM)),
        name="bias",
    )(table.reshape(-1), jnp.asarray(bkt_p), jnp.asarray(bkt_s))


def _round_specs(weights, steps):
    specs = [pl.BlockSpec((w.shape[0] // steps, w.shape[1]), lambda i: (i, 0)) for w in weights]
    shapes = [jax.ShapeDtypeStruct(w.shape, BF16) for w in weights]
    return specs, shapes


def _round_slabs(srcs, dsts):
    for src, dst in zip(srcs, dsts):
        dst[...] = src[...].astype(BF16)


def _memkv_kernel(n_round, m_ref, g_ref, wk_ref, wv_ref, gk_ref, *refs):
    mk_ref, mv_ref, mkb_ref, mvb_ref = refs[n_round:n_round + 4]
    _round_slabs(refs[:n_round], refs[n_round + 4:])
    h = _rms(m_ref[...], g_ref[...]).astype(BF16)
    zk = jnp.dot(h, wk_ref[...].astype(BF16), preferred_element_type=F32)
    zv = jnp.dot(h, wv_ref[...].astype(BF16), preferred_element_type=F32)
    gk = gk_ref[...]
    for hx in range(X_HEADS):
        sl = slice(hx * HEAD_DIM, (hx + 1) * HEAD_DIM)
        mk = _rms(zk[:, sl], gk)
        mk_ref[:, sl] = mk
        mkb_ref[:, sl] = mk.astype(BF16)
    mv_ref[...] = zv
    mvb_ref[...] = zv.astype(BF16)


def _memory_kv(mem2d, g_mem, wk, wv, g_k_x, round_weights=()):
    n = mem2d.shape[0]
    tile = MEM_TOKENS
    row = lambda i: (i, 0)
    out = lambda dt: jax.ShapeDtypeStruct((n, X_WIDTH), dt)
    round_specs, round_shapes = _round_specs(round_weights, n // tile)
    return pl.pallas_call(
        functools.partial(_memkv_kernel, len(round_weights)),
        grid=(n // tile,),
        in_specs=[pl.BlockSpec((tile, D_MODEL), row), _resident((1, D_MODEL)),
                  _resident((D_MODEL, X_WIDTH)), _resident((D_MODEL, X_WIDTH)), _resident((1, HEAD_DIM)),
                  *round_specs],
        out_specs=(*(pl.BlockSpec((tile, X_WIDTH), row) for _ in range(4)), *round_specs),
        out_shape=(out(F32), out(F32), out(BF16), out(BF16), *round_shapes),
        compiler_params=_params("arbitrary"),
        name="memkv",
    )(mem2d, g_mem, wk, wv, g_k_x, *round_weights)


def _proj_kernel(n_round, x_ref, g_ref, w_ref, gq_ref, gk_ref, gx_ref, *refs):
    b_ref, u_ref, q_ref, k_ref, v_ref, kb_ref, vb_ref, qx_ref = refs[n_round:n_round + 8]
    _round_slabs(refs[:n_round], refs[n_round + 8:])
    for r0 in range(0, x_ref.shape[0], ROW_CHUNK):
        rows = slice(r0, r0 + ROW_CHUNK)
        h = _rms(x_ref[rows, :], g_ref[...]).astype(BF16)

        def seg(lo, width):
            return jnp.dot(h, w_ref[:, lo:lo + width], preferred_element_type=F32)

        b_ref[rows, :] = seg(OFF_B, CONV_WIDTH)
        u_ref[rows, :] = seg(OFF_C, CONV_WIDTH) * seg(OFF_H, CONV_WIDTH)

        def head_norm(z, g, n_heads, outs):
            for hh in range(n_heads):
                sl = slice(hh * HEAD_DIM, (hh + 1) * HEAD_DIM)
                y = _rms(z[:, sl], g)
                for o in outs:
                    o[rows, sl] = y.astype(o.dtype)

        head_norm(seg(OFF_Q, SWA_WIDTH), gq_ref[...], SWA_Q_HEADS, (q_ref,))
        head_norm(seg(OFF_K, SWA_KV_WIDTH), gk_ref[...], SWA_KV_HEADS, (k_ref, kb_ref))
        zv = seg(OFF_V, SWA_KV_WIDTH)
        v_ref[rows, :] = zv
        vb_ref[rows, :] = zv.astype(BF16)
        head_norm(seg(OFF_QX, X_WIDTH), gx_ref[...], X_HEADS, (qx_ref,))


def _projections(x2d, g_mix, w_in, g_q, g_k, g_qx, q_dtype, round_weights=()):
    n = x2d.shape[0]
    tile = min(ROW_TILE, n)
    row = lambda i: (i, 0)
    widths = (CONV_WIDTH, CONV_WIDTH, SWA_WIDTH, SWA_KV_WIDTH, SWA_KV_WIDTH, SWA_KV_WIDTH, SWA_KV_WIDTH, X_WIDTH)
    dtypes = (F32, F32, q_dtype, F32, F32, BF16, BF16, q_dtype)
    round_specs, round_shapes = _round_specs(round_weights, n // tile)
    return pl.pallas_call(
        functools.partial(_proj_kernel, len(round_weights)),
        grid=(n // tile,),
        in_specs=[pl.BlockSpec((tile, D_MODEL), row), _resident((1, D_MODEL)), _resident((D_MODEL, IN_WIDTH)),
                  _resident((1, HEAD_DIM)), _resident((1, HEAD_DIM)), _resident((1, HEAD_DIM)), *round_specs],
        out_specs=(*(pl.BlockSpec((tile, w), row) for w in widths), *round_specs),
        out_shape=(*(jax.ShapeDtypeStruct((n, w), dt) for w, dt in zip(widths, dtypes)), *round_shapes),
        compiler_params=_params("arbitrary"),
        name="proj",
    )(x2d, g_mix, w_in, g_q, g_k, g_qx, *round_weights)


def _sink_softmax(s, sink_col):
    m = jnp.maximum(jnp.max(s, axis=-1, keepdims=True), sink_col)
    p = jnp.exp(s - m)
    den = jnp.sum(p, axis=-1, keepdims=True) + jnp.exp(sink_col - m)
    return p * (1.0 / den)


def _softmax(s):
    m = jnp.max(s, axis=-1, keepdims=True)
    p = jnp.exp(s - m)
    return p * (1.0 / jnp.sum(p, axis=-1, keepdims=True))


def _mix_out_kernel(sink_ref, q_ref, kc_ref, kp_ref, vc_ref, vp_ref, qx_ref, mk_ref, mv_ref,
                    b_ref, uc_ref, up_ref, cw_ref, bias_ref, band_ref, x_ref, w_ref, g_ref, x1_ref, h_ref):
    has_prev = pl.program_id(1) > 0
    rows = MIX_BLOCKS * WINDOW
    blocks = [slice(j * WINDOW, (j + 1) * WINDOW) for j in range(MIX_BLOCKS)]

    def project(y_cols, col0, acc):
        y = jnp.concatenate(y_cols, axis=1).astype(BF16)
        return acc + jnp.dot(y, w_ref[col0:col0 + y.shape[1], :], preferred_element_type=F32)

    u = uc_ref[...]
    prev = jnp.where(has_prev, up_ref[...], 0.0)
    ext = jnp.concatenate([prev, u], axis=0)
    cw = cw_ref[...]
    conv = cw[0:1] * ext[SUBLANES - 2:SUBLANES - 2 + rows]
    conv = conv + cw[1:2] * ext[SUBLANES - 1:SUBLANES - 1 + rows]
    conv = conv + cw[2:3] * u
    x1 = project([b_ref[...] * conv], 0, x_ref[...])

    for h in range(SWA_KV_HEADS):
        ksl = slice(h * HEAD_DIM, (h + 1) * HEAD_DIM)
        outs = [[None] * MIX_BLOCKS for _ in range(SWA_GROUP)]
        for j, rsl in enumerate(blocks):
            min_band = jnp.where(has_prev, 0.5, 1.5) if j == 0 else 0.5
            if j == 0:
                k_all = jnp.concatenate([kp_ref[:, ksl], kc_ref[rsl, ksl]], axis=0)
                v_all = jnp.concatenate([vp_ref[:, ksl], vc_ref[rsl, ksl]], axis=0)
            else:
                k_all = kc_ref[(j - 1) * WINDOW:(j + 1) * WINDOW, ksl]
                v_all = vc_ref[(j - 1) * WINDOW:(j + 1) * WINDOW, ksl]
            for g in range(SWA_GROUP):
                hh = h * SWA_GROUP + g
                q = q_ref[rsl, hh * HEAD_DIM:(hh + 1) * HEAD_DIM]
                s = lax.dot_general(q, k_all, NT_DIMS, preferred_element_type=F32) * SCALE
                s = jnp.where(band_ref[...] > min_band, s + bias_ref[h, g * WINDOW:(g + 1) * WINDOW, :], NEG)
                w = _sink_softmax(s, sink_ref[hh]).astype(BF16)
                outs[g][j] = jnp.dot(w, v_all, preferred_element_type=F32).astype(BF16)
        x1 = project([jnp.concatenate(o, axis=0) for o in outs], CONV_WIDTH + h * SWA_GROUP * HEAD_DIM, x1)

    outs = [[None] * MIX_BLOCKS for _ in range(X_HEADS)]
    for j, rsl in enumerate(blocks):
        for hx in range(X_HEADS):
            sl = slice(hx * HEAD_DIM, (hx + 1) * HEAD_DIM)
            s = lax.dot_general(qx_ref[rsl, sl], mk_ref[:, sl], NT_DIMS, preferred_element_type=F32) * SCALE
            w = _softmax(s).astype(BF16)
            outs[hx][j] = jnp.dot(w, mv_ref[:, sl], preferred_element_type=F32).astype(BF16)
    x1 = project([jnp.concatenate(o, axis=0) for o in outs], CONV_WIDTH + SWA_WIDTH, x1)

    x1_ref[...] = x1
    h_ref[...] = _rms(x1, g_ref[...]).astype(BF16)


def _mix_out_prompt(sinks, q, kb, vb, qx, mkb, mvb, b, u, conv_w, bias_p, x2d, w_out, g_mlp, bsz, seq):
    rows = MIX_BLOCKS * WINDOW
    steps = seq // rows
    cur = lambda bi, i: (bi * steps + i, 0)
    prv = lambda bi, i: (jnp.maximum((bi * steps + i) * MIX_BLOCKS - 1, 0), 0)
    prv8 = lambda bi, i: (jnp.maximum((bi * steps + i) * (rows // SUBLANES) - 1, 0), 0)
    per_b = lambda bi, i: (bi, 0)
    dist = WINDOW + np.arange(WINDOW)[:, None] - np.arange(2 * WINDOW)[None, :]
    band = np.where((dist >= 0) & (dist < WINDOW), np.where(np.arange(2 * WINDOW)[None, :] < WINDOW, 1.0, 2.0), 0.0)
    return pl.pallas_call(
        _mix_out_kernel,
        grid=(bsz, steps),
        in_specs=[pl.BlockSpec(memory_space=pltpu.SMEM),
                  pl.BlockSpec((rows, SWA_WIDTH), cur),
                  pl.BlockSpec((rows, SWA_KV_WIDTH), cur), pl.BlockSpec((WINDOW, SWA_KV_WIDTH), prv),
                  pl.BlockSpec((rows, SWA_KV_WIDTH), cur), pl.BlockSpec((WINDOW, SWA_KV_WIDTH), prv),
                  pl.BlockSpec((rows, X_WIDTH), cur),
                  pl.BlockSpec((MEM_TOKENS, X_WIDTH), per_b), pl.BlockSpec((MEM_TOKENS, X_WIDTH), per_b),
                  pl.BlockSpec((rows, CONV_WIDTH), cur), pl.BlockSpec((rows, CONV_WIDTH), cur),
                  pl.BlockSpec((SUBLANES, CONV_WIDTH), prv8),
                  _resident((CONV_K, CONV_WIDTH)),
                  _resident((SWA_KV_HEADS, SWA_GROUP * WINDOW, 2 * WINDOW)),
                  _resident((WINDOW, 2 * WINDOW)),
                  pl.BlockSpec((rows, D_MODEL), cur),
                  _resident((D_MODEL, D_MODEL)), _resident((1, D_MODEL))],
        out_specs=(pl.BlockSpec((rows, D_MODEL), cur), pl.BlockSpec((rows, D_MODEL), cur)),
        out_shape=(jax.ShapeDtypeStruct((bsz * seq, D_MODEL), F32), jax.ShapeDtypeStruct((bsz * seq, D_MODEL), BF16)),
        compiler_params=_params("arbitrary", "arbitrary"),
        name="mix_out",
    )(sinks, q, kb, kb, vb, vb, qx, mkb, mvb, b, u, u, conv_w, bias_p, jnp.asarray(band, F32), x2d, w_out, g_mlp)


def _mix_sample_kernel(t_len, sink_ref, q_ref, kn_ref, vn_ref, ck_ref, cv_ref, qx_ref, cmk_ref, cmv_ref,
                       b_ref, u_ref, cc_ref, cw_ref, bias_ref, o_ref, sk_ref, sv_ref):
    nb = cc_ref.shape[0]
    n_cache = SWA_KV_HEADS * WINDOW
    n_new = SWA_KV_HEADS * t_len
    n_mem = X_HEADS * MEM_TOKENS
    log_t = int(math.log2(t_len))

    rows = SWA_Q_HEADS * t_len
    r = lax.broadcasted_iota(jnp.int32, (nb * rows, SAMPLE_KEY_ROWS), 0) & (rows - 1)
    c = lax.broadcasted_iota(jnp.int32, (nb * rows, SAMPLE_KEY_ROWS), 1)
    key_pos = jnp.where(c < n_cache, c >> 1, WINDOW + ((c - n_cache) >> 1))
    dist = (r & (t_len - 1)) + WINDOW - key_pos
    valid = (dist >= 0) & (dist < WINDOW) & ((c & (SWA_KV_HEADS - 1)) == (r >> int(math.log2(SWA_GROUP * t_len))))
    bias = jnp.concatenate([bias_ref[...]] * nb, axis=0)
    sink_col = jnp.concatenate([jnp.full((t_len, 1), sink_ref[hh], F32) for hh in range(SWA_Q_HEADS)] * nb, axis=0)
    xrows = X_HEADS * t_len
    xr = lax.broadcasted_iota(jnp.int32, (nb * xrows, n_mem), 0) & (xrows - 1)
    xc = lax.broadcasted_iota(jnp.int32, (nb * xrows, n_mem), 1)
    x_valid = (xc & (X_HEADS - 1)) == (xr >> log_t)
    cw = cw_ref[...]
    trow = lax.broadcasted_iota(jnp.int32, (t_len, CONV_WIDTH), 0)
    zeros_pad = jnp.zeros((SAMPLE_KEY_ROWS - n_cache - n_new, HEAD_DIM), F32)

    s_list, sx_list = [], []
    for bi in range(nb):
        rsl = slice(bi * t_len, (bi + 1) * t_len)
        c0, n0, m0 = bi * n_cache, bi * n_new, bi * n_mem

        for dst, cache, new in ((sk_ref, ck_ref, kn_ref), (sv_ref, cv_ref, vn_ref)):
            dst[c0:c0 + n_cache - n_new, :] = cache[c0 + n_new:c0 + n_cache, :]
            dst[c0 + n_cache - n_new:c0 + n_cache, :] = new[n0:n0 + n_new, :]

        u = u_ref[rsl, :]
        cc = cc_ref[bi]
        cc1 = jnp.broadcast_to(cc[1:2], u.shape)
        cc0 = jnp.broadcast_to(cc[0:1], u.shape)
        u_m1 = jnp.where(trow >= 1, pltpu.roll(u, 1, 0), cc1)
        u_m2 = jnp.where(trow >= 2, pltpu.roll(u, 2, 0), jnp.where(trow == 1, cc1, cc0))
        conv = cw[0:1] * u_m2
        conv = conv + cw[1:2] * u_m1
        conv = conv + cw[2:3] * u
        o_ref[rsl, 0:CONV_WIDTH] = (b_ref[rsl, :] * conv).astype(o_ref.dtype)

        qb = q_ref[rsl, :]
        q_rows = jnp.concatenate([qb[:, hh * HEAD_DIM:(hh + 1) * HEAD_DIM] for hh in range(SWA_Q_HEADS)], axis=0)
        k_all = jnp.concatenate([ck_ref[c0:c0 + n_cache, :], kn_ref[n0:n0 + n_new, :], zeros_pad], axis=0)
        s_list.append(lax.dot_general(q_rows.astype(BF16), k_all.astype(BF16), NT_DIMS, preferred_element_type=F32))
        qxb = qx_ref[rsl, :]
        qx_rows = jnp.concatenate([qxb[:, hx * HEAD_DIM:(hx + 1) * HEAD_DIM] for hx in range(X_HEADS)], axis=0)
        mk = cmk_ref[m0:m0 + n_mem, :].astype(BF16)
        sx_list.append(lax.dot_general(qx_rows.astype(BF16), mk, NT_DIMS, preferred_element_type=F32))

    s = jnp.concatenate(s_list, axis=0) * SCALE
    w = _sink_softmax(jnp.where(valid, s + bias, NEG), sink_col).astype(BF16)
    sx = jnp.concatenate(sx_list, axis=0) * SCALE
    wx = _softmax(jnp.where(x_valid, sx, NEG)).astype(BF16)

    for bi in range(nb):
        rsl = slice(bi * t_len, (bi + 1) * t_len)
        c0, n0, m0 = bi * n_cache, bi * n_new, bi * n_mem
        v_all = jnp.concatenate([cv_ref[c0:c0 + n_cache, :], vn_ref[n0:n0 + n_new, :], zeros_pad], axis=0)
        o = jnp.dot(w[bi * rows:(bi + 1) * rows], v_all.astype(BF16), preferred_element_type=F32)
        for hh in range(SWA_Q_HEADS):
            col = CONV_WIDTH + hh * HEAD_DIM
            o_ref[rsl, col:col + HEAD_DIM] = o[hh * t_len:(hh + 1) * t_len].astype(o_ref.dtype)
        mv = cmv_ref[m0:m0 + n_mem, :].astype(BF16)
        ox = jnp.dot(wx[bi * xrows:(bi + 1) * xrows], mv, preferred_element_type=F32)
        for hx in range(X_HEADS):
            col = CONV_WIDTH + SWA_WIDTH + hx * HEAD_DIM
            o_ref[rsl, col:col + HEAD_DIM] = ox[hx * t_len:(hx + 1) * t_len].astype(o_ref.dtype)


def _mix_sample(sinks, q, k_rows, v_rows, cache_k, cache_v, qx, cache_mk, cache_mv, b, u, cache_conv, conv_w,
                bias_s, bsz, t_len):
    assert t_len == SUBLANES and bsz % SAMPLE_BATCH_TILE == 0 and SWA_KV_HEADS == 2
    nb = SAMPLE_BATCH_TILE
    n_cache, n_new, n_mem = SWA_KV_HEADS * WINDOW, SWA_KV_HEADS * t_len, X_HEADS * MEM_TOKENS
    row = lambda i: (i, 0)
    rows_of = lambda n, w: pl.BlockSpec((nb * n, w), row)
    return pl.pallas_call(
        functools.partial(_mix_sample_kernel, t_len),
        grid=(bsz // nb,),
        in_specs=[pl.BlockSpec(memory_space=pltpu.SMEM),
                  rows_of(t_len, SWA_WIDTH),
                  rows_of(n_new, HEAD_DIM), rows_of(n_new, HEAD_DIM),
                  rows_of(n_cache, HEAD_DIM), rows_of(n_cache, HEAD_DIM),
                  rows_of(t_len, X_WIDTH),
                  rows_of(n_mem, HEAD_DIM), rows_of(n_mem, HEAD_DIM),
                  rows_of(t_len, CONV_WIDTH), rows_of(t_len, CONV_WIDTH),
                  pl.BlockSpec((nb, CONV_K - 1, CONV_WIDTH), lambda i: (i, 0, 0)),
                  _resident((CONV_K, CONV_WIDTH)),
                  _resident((SWA_Q_HEADS * t_len, SAMPLE_KEY_ROWS))],
        out_specs=(rows_of(t_len, D_MODEL), rows_of(n_cache, HEAD_DIM), rows_of(n_cache, HEAD_DIM)),
        out_shape=(jax.ShapeDtypeStruct((bsz * t_len, D_MODEL), F32),
                   jax.ShapeDtypeStruct((bsz * n_cache, HEAD_DIM), F32),
                   jax.ShapeDtypeStruct((bsz * n_cache, HEAD_DIM), F32)),
        compiler_params=_params("arbitrary"),
        name="mix_sample",
    )(sinks, q, k_rows, v_rows, cache_k, cache_v, qx, cache_mk, cache_mv, b, u, cache_conv, conv_w, bias_s)


def _outproj_kernel(x_ref, y_ref, w_ref, g_ref, x1_ref, h_ref):
    for r0 in range(0, x_ref.shape[0], ROW_CHUNK):
        rows = slice(r0, r0 + ROW_CHUNK)
        x1 = x_ref[rows, :] + jnp.dot(y_ref[rows, :].astype(BF16), w_ref[...], preferred_element_type=F32)
        x1_ref[rows, :] = x1
        h_ref[rows, :] = _rms(x1, g_ref[...]).astype(BF16)


def _out_projection(x2d, y, w_out, g_mlp):
    n = x2d.shape[0]
    tile = min(ROW_TILE, n)
    row = lambda i: (i, 0)
    return pl.pallas_call(
        _outproj_kernel,
        grid=(n // tile,),
        in_specs=[pl.BlockSpec((tile, D_MODEL), row), pl.BlockSpec((tile, D_MODEL), row),
                  _resident((D_MODEL, D_MODEL)), _resident((1, D_MODEL))],
        out_specs=(pl.BlockSpec((tile, D_MODEL), row), pl.BlockSpec((tile, D_MODEL), row)),
        out_shape=(jax.ShapeDtypeStruct((n, D_MODEL), F32), jax.ShapeDtypeStruct((n, D_MODEL), BF16)),
        compiler_params=_params("arbitrary"),
        name="outproj",
    )(x2d, y, w_out, g_mlp)


def _mlp_kernel(x1_ref, h_ref, wu_ref, wd_ref, o_ref):
    @pl.when(pl.program_id(1) == 0)
    def _():
        o_ref[...] = x1_ref[...]

    a = jnp.maximum(jnp.dot(h_ref[...], wu_ref[...], preferred_element_type=F32), 0.0)
    o_ref[...] += jnp.dot((a * a).astype(BF16), wd_ref[...], preferred_element_type=F32)


def _mlp_cast_kernel(x1_hbm, h_ref, wu_ref, wd_ref, o_ref, wub_ref, wdb_ref, sem):
    j = pl.program_id(1)
    residual_copy = pltpu.make_async_copy(x1_hbm, o_ref, sem)

    @pl.when(j == 0)
    def _():
        residual_copy.start()

    wu = wu_ref[...].astype(BF16)
    wub_ref[...] = wu
    a = jnp.maximum(jnp.dot(h_ref[...], wu, preferred_element_type=F32), 0.0)
    a = (a * a).astype(BF16)

    @pl.when(j == 0)
    def _():
        residual_copy.wait()

    for c0 in range(0, D_MODEL, MLP_CAST_FF_TILE):
        cols = slice(c0, c0 + MLP_CAST_FF_TILE)
        wd = wd_ref[:, cols].astype(BF16)
        wdb_ref[:, cols] = wd
        o_ref[:, cols] += jnp.dot(a, wd, preferred_element_type=F32)


def _mlp(x1, h, w_up, w_down):
    n = x1.shape[0]
    tile = min(MLP_ROW_TILE, n)
    rows = pl.BlockSpec((tile, D_MODEL), lambda i, j: (i, 0))
    return pl.pallas_call(
        _mlp_kernel,
        grid=(n // tile, D_FF // MLP_FF_TILE),
        in_specs=[rows, rows,
                  pl.BlockSpec((D_MODEL, MLP_FF_TILE), lambda i, j: (0, j)),
                  pl.BlockSpec((MLP_FF_TILE, D_MODEL), lambda i, j: (j, 0))],
        out_specs=rows,
        out_shape=jax.ShapeDtypeStruct((n, D_MODEL), F32),
        compiler_params=_params("arbitrary", "arbitrary"),
        name="mlp",
    )(x1, h, w_up, w_down)


def _mlp_cast(x1, h, w_up, w_down):
    n = x1.shape[0]
    rows = pl.BlockSpec((n, D_MODEL), lambda i, j: (0, 0))
    up_spec = pl.BlockSpec((D_MODEL, MLP_CAST_FF_TILE), lambda i, j: (0, j))
    down_spec = pl.BlockSpec((MLP_CAST_FF_TILE, D_MODEL), lambda i, j: (j, 0))
    return pl.pallas_call(
        _mlp_cast_kernel,
        grid=(1, D_FF // MLP_CAST_FF_TILE),
        in_specs=[pl.BlockSpec(memory_space=pl.ANY), rows, up_spec, down_spec],
        out_specs=(rows, up_spec, down_spec),
        out_shape=(jax.ShapeDtypeStruct((n, D_MODEL), F32),
                   jax.ShapeDtypeStruct(w_up.shape, BF16), jax.ShapeDtypeStruct(w_down.shape, BF16)),
        scratch_shapes=[pltpu.SemaphoreType.DMA(())],
        compiler_params=_params("arbitrary", "arbitrary"),
        name="mlp_cast",
    )(x1, h, w_up, w_down)


def kernel(x_prompt, x_sample, mem_prompt, cache_conv, cache_swa_k, cache_swa_v, cache_mem_k, cache_mem_v,
           rel_bias_table, g_mix, w_in, conv_w, g_q_swa, g_k_swa, sinks, g_q_x, g_k_x, g_mem,
           w_mem_k, w_mem_v, w_out, g_mlp, w_up, w_down):
    depth = w_in.shape[0]
    bsz, seq, _ = x_prompt.shape
    dbsz, t_len, _ = x_sample.shape
    xp = x_prompt.reshape(bsz * seq, D_MODEL)
    xs = x_sample.reshape(dbsz * t_len, D_MODEL)
    mem2d = mem_prompt.reshape(bsz * MEM_TOKENS, D_MODEL)
    bias_p, bias_s = _bias_tables(rel_bias_table, t_len)

    outs = [[] for _ in range(8)]
    for l in range(depth):
        vec = lambda a: a[l].reshape(1, -1)
        mk, mv, mkb, mvb, wi = _memory_kv(mem2d, vec(g_mem), w_mem_k[l], w_mem_v[l], vec(g_k_x),
                                          round_weights=(w_in[l],))
        proj = functools.partial(_projections, g_mix=vec(g_mix), w_in=wi, g_q=vec(g_q_swa), g_k=vec(g_k_swa),
                                 g_qx=vec(g_q_x))
        pb, pu, pq, pk, pv, pkb, pvb, pqx, wo = proj(xp, q_dtype=BF16, round_weights=(w_out[l],))

        head_rows = lambda a: a.reshape(-1, HEAD_DIM)
        b, u, q, k, v, _, _, qx = proj(xs, q_dtype=F32)
        y, sk, sv = _mix_sample(
            sinks[l], q, head_rows(k), head_rows(v), head_rows(cache_swa_k[l]), head_rows(cache_swa_v[l]),
            qx, head_rows(cache_mem_k[l]), head_rows(cache_mem_v[l]),
            b, u, cache_conv[l], conv_w[l], bias_s, dbsz, t_len)
        x1, h = _out_projection(xs, y, wo, vec(g_mlp))
        xs, wu, wd = _mlp_cast(x1, h, w_up[l], w_down[l])
        outs[5].append(u.reshape(dbsz, t_len, CONV_WIDTH)[:, t_len - (CONV_K - 1):])
        outs[6].append(sk.reshape(dbsz, WINDOW, SWA_KV_HEADS, HEAD_DIM))
        outs[7].append(sv.reshape(dbsz, WINDOW, SWA_KV_HEADS, HEAD_DIM))

        x1, h = _mix_out_prompt(sinks[l], pq, pkb, pvb, pqx, mkb, mvb, pb, pu, conv_w[l], bias_p, xp, wo,
                                vec(g_mlp), bsz, seq)
        xp = _mlp(x1, h, wu, wd)
        outs[0].append(pu.reshape(bsz, seq, CONV_WIDTH)[:, seq - (CONV_K - 1):])
        last_window = lambda a: a.reshape(bsz, seq, SWA_KV_WIDTH)[:, seq - WINDOW:].reshape(
            bsz, WINDOW, SWA_KV_HEADS, HEAD_DIM)
        outs[1].append(last_window(pk))
        outs[2].append(last_window(pv))
        outs[3].append(mk.reshape(bsz, MEM_TOKENS, X_HEADS, HEAD_DIM))
        outs[4].append(mv.reshape(bsz, MEM_TOKENS, X_HEADS, HEAD_DIM))

    return (xp.reshape(bsz, seq, D_MODEL), xs.reshape(dbsz, t_len, D_MODEL)) + tuple(jnp.stack(o) for o in outs)
```

```python
import functools
import math

import numpy as np
import jax
import jax.numpy as jnp
from jax import lax
from jax.experimental import pallas as pl
from jax.experimental.pallas import tpu as pltpu

D_MODEL = 2048
HEAD_DIM = 128
SWA_Q_HEADS = 8
SWA_KV_HEADS = 2
SWA_GROUP = SWA_Q_HEADS // SWA_KV_HEADS
SWA_WIDTH = SWA_Q_HEADS * HEAD_DIM
SWA_KV_WIDTH = SWA_KV_HEADS * HEAD_DIM
X_HEADS = 4
X_WIDTH = X_HEADS * HEAD_DIM
CONV_WIDTH = D_MODEL - SWA_WIDTH - X_WIDTH
CONV_K = 3
WINDOW = 128
NUM_BUCKETS = 32
MAX_DISTANCE = WINDOW
MEM_TOKENS = 256
D_FF = 4 * D_MODEL
EPS = 1e-6
NEG = -1e30
SCALE = HEAD_DIM ** -0.5

OFF_B = 0
OFF_C = CONV_WIDTH
OFF_H = 2 * CONV_WIDTH
OFF_Q = 3 * CONV_WIDTH
OFF_K = OFF_Q + SWA_WIDTH
OFF_V = OFF_K + SWA_KV_WIDTH
OFF_QX = OFF_V + SWA_KV_WIDTH
IN_WIDTH = OFF_QX + X_WIDTH

VMEM_LIMIT_V7X = 56 * 1024 * 1024
SUBLANES = 8

ROW_TILE = 512
ROW_CHUNK = 256
MLP_ROW_TILE = 512
MLP_FF_TILE = 1024
MLP_CAST_FF_TILE = 512
SAMPLE_BATCH_TILE = 8
MIX_BLOCKS = 4

BF16 = jnp.bfloat16
F32 = jnp.float32
NT_DIMS = (((1,), (1,)), ((), ()))


def _params(*sem):
    return pltpu.CompilerParams(dimension_semantics=sem, vmem_limit_bytes=VMEM_LIMIT_V7X)


def _resident(shape):
    nd = len(shape)
    return pl.BlockSpec(shape, lambda *_: (0,) * nd, pipeline_mode=pl.Buffered(1))


def _rms(x, g):
    ms = jnp.mean(x * x, axis=-1, keepdims=True)
    return x * lax.rsqrt(ms + EPS) * g


def _rel_bucket_np(dist):
    n = np.maximum(dist, 0)
    max_exact = NUM_BUCKETS // 2
    nf = np.maximum(n, 1).astype(np.float32)
    large = max_exact + (np.log(nf / np.float32(max_exact)) / np.float32(math.log(MAX_DISTANCE / max_exact))
                         * np.float32(NUM_BUCKETS - max_exact)).astype(np.int32)
    large = np.minimum(large, NUM_BUCKETS - 1)
    return np.where(n < max_exact, n, large).astype(np.int32)


SAMPLE_KEY_ROWS = 3 * WINDOW


def _sample_key_index(t_len):
    c = np.arange(SAMPLE_KEY_ROWS)
    n_cache = SWA_KV_HEADS * WINDOW
    assert n_cache + SWA_KV_HEADS * t_len <= SAMPLE_KEY_ROWS
    pos = np.where(c < n_cache, c // SWA_KV_HEADS, WINDOW + (c - n_cache) // SWA_KV_HEADS)
    return pos, c % SWA_KV_HEADS


def _bias_kernel(tab_ref, bp_ref, bs_ref, op_ref, os_ref):
    bp = bp_ref[...]
    bs = bs_ref[...]
    for hh in range(SWA_Q_HEADS):
        accp = jnp.zeros(bp.shape, F32)
        accs = jnp.zeros(bs.shape, F32)
        for k in range(NUM_BUCKETS):
            t = tab_ref[k * SWA_Q_HEADS + hh]
            accp = jnp.where(bp == k, t, accp)
            accs = jnp.where(bs == k, t, accs)
        h, g = divmod(hh, SWA_GROUP)
        op_ref[h, g * WINDOW:(g + 1) * WINDOW, :] = accp
        os_ref[hh * SUBLANES:(hh + 1) * SUBLANES, :] = accs


def _bias_tables(table, t_len):
    qi = np.arange(WINDOW)[:, None]
    kj = np.arange(2 * WINDOW)[None, :]
    bkt_p = _rel_bucket_np(WINDOW + qi - kj)
    key_pos, _ = _sample_key_index(t_len)
    bkt_s = _rel_bucket_np(np.arange(t_len)[:, None] + WINDOW - key_pos[None, :])
    return pl.pallas_call(
        _bias_kernel,
        out_shape=(jax.ShapeDtypeStruct((SWA_KV_HEADS, SWA_GROUP * WINDOW, 2 * WINDOW), F32),
                   jax.ShapeDtypeStruct((SWA_Q_HEADS * t_len, SAMPLE_KEY_ROWS), F32)),
        in_specs=[pl.BlockSpec(memory_space=pltpu.SMEM),
                  pl.BlockSpec(memory_space=pltpu.VMEM),
                  pl.BlockSpec(memory_space=pltpu.VMEM)],
        out_specs=(pl.BlockSpec(memory_space=pltpu.VMEM), pl.BlockSpec(memory_space=pltpu.VMEM)),
        name="bias",
    )(table.reshape(-1), jnp.asarray(bkt_p), jnp.asarray(bkt_s))


def _round_specs(weights, steps):
    specs = [pl.BlockSpec((w.shape[0] // steps, w.shape[1]), lambda i: (i, 0)) for w in weights]
    shapes = [jax.ShapeDtypeStruct(w.shape, BF16) for w in weights]
    return specs, shapes


def _round_slabs(srcs, dsts):
    for src, dst in zip(srcs, dsts):
        dst[...] = src[...].astype(BF16)


def _memkv_kernel(n_round, m_ref, g_ref, wk_ref, wv_ref, gk_ref, *refs):
    mk_ref, mv_ref, mkb_ref, mvb_ref = refs[n_round:n_round + 4]
    _round_slabs(refs[:n_round], refs[n_round + 4:])
    h = _rms(m_ref[...], g_ref[...]).astype(BF16)
    zk = jnp.dot(h, wk_ref[...].astype(BF16), preferred_element_type=F32)
    zv = jnp.dot(h, wv_ref[...].astype(BF16), preferred_element_type=F32)
    gk = gk_ref[...]
    tokens = m_ref.shape[0]
    for hx in range(X_HEADS):
        sl = slice(hx * HEAD_DIM, (hx + 1) * HEAD_DIM)
        head_rows = pl.ds(hx, tokens, stride=X_HEADS)
        mk = _rms(zk[:, sl], gk)
        mk_ref[head_rows, :] = mk
        mv_ref[head_rows, :] = zv[:, sl]
        mkb_ref[:, sl] = mk.astype(BF16)
    mvb_ref[...] = zv.astype(BF16)


def _memory_kv(mem2d, g_mem, wk, wv, g_k_x, round_weights=()):
    n = mem2d.shape[0]
    tile = MEM_TOKENS
    row = lambda i: (i, 0)
    head_rows = pl.BlockSpec((tile * X_HEADS, HEAD_DIM), row)
    round_specs, round_shapes = _round_specs(round_weights, n // tile)
    return pl.pallas_call(
        functools.partial(_memkv_kernel, len(round_weights)),
        grid=(n // tile,),
        in_specs=[pl.BlockSpec((tile, D_MODEL), row), _resident((1, D_MODEL)),
                  _resident((D_MODEL, X_WIDTH)), _resident((D_MODEL, X_WIDTH)), _resident((1, HEAD_DIM)),
                  *round_specs],
        out_specs=(head_rows, head_rows, pl.BlockSpec((tile, X_WIDTH), row), pl.BlockSpec((tile, X_WIDTH), row),
                   *round_specs),
        out_shape=(jax.ShapeDtypeStruct((n * X_HEADS, HEAD_DIM), F32),
                   jax.ShapeDtypeStruct((n * X_HEADS, HEAD_DIM), F32),
                   jax.ShapeDtypeStruct((n, X_WIDTH), BF16), jax.ShapeDtypeStruct((n, X_WIDTH), BF16),
                   *round_shapes),
        compiler_params=_params("arbitrary"),
        name="memkv",
    )(mem2d, g_mem, wk, wv, g_k_x, *round_weights)


def _proj_kernel(n_round, x_ref, g_ref, w_ref, gq_ref, gk_ref, gx_ref, *refs):
    b_ref, u_ref, q_ref, k_ref, v_ref, kb_ref, vb_ref, qx_ref = refs[n_round:n_round + 8]
    _round_slabs(refs[:n_round], refs[n_round + 8:])
    for r0 in range(0, x_ref.shape[0], ROW_CHUNK):
        rows = slice(r0, r0 + ROW_CHUNK)
        h = _rms(x_ref[rows, :], g_ref[...]).astype(BF16)

        def seg(lo, width):
            return jnp.dot(h, w_ref[:, lo:lo + width], preferred_element_type=F32)

        b_ref[rows, :] = seg(OFF_B, CONV_WIDTH)
        u_ref[rows, :] = seg(OFF_C, CONV_WIDTH) * seg(OFF_H, CONV_WIDTH)

        def head_norm(z, g, n_heads, out):
            for hh in range(n_heads):
                sl = slice(hh * HEAD_DIM, (hh + 1) * HEAD_DIM)
                out[rows, sl] = _rms(z[:, sl], g).astype(out.dtype)

        head_norm(seg(OFF_Q, SWA_WIDTH), gq_ref[...], SWA_Q_HEADS, q_ref)
        head_norm(seg(OFF_QX, X_WIDTH), gx_ref[...], X_HEADS, qx_ref)
        zk = seg(OFF_K, SWA_KV_WIDTH)
        zv = seg(OFF_V, SWA_KV_WIDTH)
        vb_ref[rows, :] = zv.astype(BF16)
        for hh in range(SWA_KV_HEADS):
            sl = slice(hh * HEAD_DIM, (hh + 1) * HEAD_DIM)
            head_rows = pl.ds(r0 * SWA_KV_HEADS + hh, ROW_CHUNK, stride=SWA_KV_HEADS)
            k = _rms(zk[:, sl], gk_ref[...])
            k_ref[head_rows, :] = k
            v_ref[head_rows, :] = zv[:, sl]
            kb_ref[rows, sl] = k.astype(BF16)


def _projections(x2d, g_mix, w_in, g_q, g_k, g_qx, q_dtype, round_weights=()):
    n = x2d.shape[0]
    tile = min(ROW_TILE, n)
    row = lambda i: (i, 0)
    outs = ((1, CONV_WIDTH, F32), (1, CONV_WIDTH, F32), (1, SWA_WIDTH, q_dtype),
            (SWA_KV_HEADS, HEAD_DIM, F32), (SWA_KV_HEADS, HEAD_DIM, F32),
            (1, SWA_KV_WIDTH, BF16), (1, SWA_KV_WIDTH, BF16), (1, X_WIDTH, q_dtype))
    round_specs, round_shapes = _round_specs(round_weights, n // tile)
    return pl.pallas_call(
        functools.partial(_proj_kernel, len(round_weights)),
        grid=(n // tile,),
        in_specs=[pl.BlockSpec((tile, D_MODEL), row), _resident((1, D_MODEL)), _resident((D_MODEL, IN_WIDTH)),
                  _resident((1, HEAD_DIM)), _resident((1, HEAD_DIM)), _resident((1, HEAD_DIM)), *round_specs],
        out_specs=(*(pl.BlockSpec((tile * r, w), row) for r, w, _ in outs), *round_specs),
        out_shape=(*(jax.ShapeDtypeStruct((n * r, w), dt) for r, w, dt in outs), *round_shapes),
        compiler_params=_params("arbitrary"),
        name="proj",
    )(x2d, g_mix, w_in, g_q, g_k, g_qx, *round_weights)


def _sink_softmax(s, sink_col):
    m = jnp.maximum(jnp.max(s, axis=-1, keepdims=True), sink_col)
    p = jnp.exp(s - m)
    den = jnp.sum(p, axis=-1, keepdims=True) + jnp.exp(sink_col - m)
    return p * (1.0 / den)


def _softmax(s):
    m = jnp.max(s, axis=-1, keepdims=True)
    p = jnp.exp(s - m)
    return p * (1.0 / jnp.sum(p, axis=-1, keepdims=True))


def _mix_out_kernel(sink_ref, q_ref, kc_ref, kp_ref, vc_ref, vp_ref, qx_ref, mk_ref, mv_ref,
                    b_ref, uc_ref, up_ref, cw_ref, bias_ref, band_ref, x_ref, w_ref, g_ref, x1_ref, h_ref):
    has_prev = pl.program_id(1) > 0
    rows = MIX_BLOCKS * WINDOW
    blocks = [slice(j * WINDOW, (j + 1) * WINDOW) for j in range(MIX_BLOCKS)]

    def project(y_cols, col0, acc):
        y = jnp.concatenate(y_cols, axis=1).astype(BF16)
        return acc + jnp.dot(y, w_ref[col0:col0 + y.shape[1], :], preferred_element_type=F32)

    u = uc_ref[...]
    prev = jnp.where(has_prev, up_ref[...], 0.0)
    ext = jnp.concatenate([prev, u], axis=0)
    cw = cw_ref[...]
    conv = cw[0:1] * ext[SUBLANES - 2:SUBLANES - 2 + rows]
    conv = conv + cw[1:2] * ext[SUBLANES - 1:SUBLANES - 1 + rows]
    conv = conv + cw[2:3] * u
    x1 = project([b_ref[...] * conv], 0, x_ref[...])

    for h in range(SWA_KV_HEADS):
        ksl = slice(h * HEAD_DIM, (h + 1) * HEAD_DIM)
        outs = [[None] * MIX_BLOCKS for _ in range(SWA_GROUP)]
        for j, rsl in enumerate(blocks):
            min_band = jnp.where(has_prev, 0.5, 1.5) if j == 0 else 0.5
            if j == 0:
                k_all = jnp.concatenate([kp_ref[:, ksl], kc_ref[rsl, ksl]], axis=0)
                v_all = jnp.concatenate([vp_ref[:, ksl], vc_ref[rsl, ksl]], axis=0)
            else:
                k_all = kc_ref[(j - 1) * WINDOW:(j + 1) * WINDOW, ksl]
                v_all = vc_ref[(j - 1) * WINDOW:(j + 1) * WINDOW, ksl]
            for g in range(SWA_GROUP):
                hh = h * SWA_GROUP + g
                q = q_ref[rsl, hh * HEAD_DIM:(hh + 1) * HEAD_DIM]
                s = lax.dot_general(q, k_all, NT_DIMS, preferred_element_type=F32) * SCALE
                s = jnp.where(band_ref[...] > min_band, s + bias_ref[h, g * WINDOW:(g + 1) * WINDOW, :], NEG)
                w = _sink_softmax(s, sink_ref[hh]).astype(BF16)
                outs[g][j] = jnp.dot(w, v_all, preferred_element_type=F32).astype(BF16)
        x1 = project([jnp.concatenate(o, axis=0) for o in outs], CONV_WIDTH + h * SWA_GROUP * HEAD_DIM, x1)

    outs = [[None] * MIX_BLOCKS for _ in range(X_HEADS)]
    for j, rsl in enumerate(blocks):
        for hx in range(X_HEADS):
            sl = slice(hx * HEAD_DIM, (hx + 1) * HEAD_DIM)
            s = lax.dot_general(qx_ref[rsl, sl], mk_ref[:, sl], NT_DIMS, preferred_element_type=F32) * SCALE
            w = _softmax(s).astype(BF16)
            outs[hx][j] = jnp.dot(w, mv_ref[:, sl], preferred_element_type=F32).astype(BF16)
    x1 = project([jnp.concatenate(o, axis=0) for o in outs], CONV_WIDTH + SWA_WIDTH, x1)

    x1_ref[...] = x1
    h_ref[...] = _rms(x1, g_ref[...]).astype(BF16)


def _mix_out_prompt(sinks, q, kb, vb, qx, mkb, mvb, b, u, conv_w, bias_p, x2d, w_out, g_mlp, bsz, seq):
    rows = MIX_BLOCKS * WINDOW
    steps = seq // rows
    cur = lambda bi, i: (bi * steps + i, 0)
    prv = lambda bi, i: (jnp.maximum((bi * steps + i) * MIX_BLOCKS - 1, 0), 0)
    prv8 = lambda bi, i: (jnp.maximum((bi * steps + i) * (rows // SUBLANES) - 1, 0), 0)
    per_b = lambda bi, i: (bi, 0)
    dist = WINDOW + np.arange(WINDOW)[:, None] - np.arange(2 * WINDOW)[None, :]
    band = np.where((dist >= 0) & (dist < WINDOW), np.where(np.arange(2 * WINDOW)[None, :] < WINDOW, 1.0, 2.0), 0.0)
    return pl.pallas_call(
        _mix_out_kernel,
        grid=(bsz, steps),
        in_specs=[pl.BlockSpec(memory_space=pltpu.SMEM),
                  pl.BlockSpec((rows, SWA_WIDTH), cur),
                  pl.BlockSpec((rows, SWA_KV_WIDTH), cur), pl.BlockSpec((WINDOW, SWA_KV_WIDTH), prv),
                  pl.BlockSpec((rows, SWA_KV_WIDTH), cur), pl.BlockSpec((WINDOW, SWA_KV_WIDTH), prv),
                  pl.BlockSpec((rows, X_WIDTH), cur),
                  pl.BlockSpec((MEM_TOKENS, X_WIDTH), per_b), pl.BlockSpec((MEM_TOKENS, X_WIDTH), per_b),
                  pl.BlockSpec((rows, CONV_WIDTH), cur), pl.BlockSpec((rows, CONV_WIDTH), cur),
                  pl.BlockSpec((SUBLANES, CONV_WIDTH), prv8),
                  _resident((CONV_K, CONV_WIDTH)),
                  _resident((SWA_KV_HEADS, SWA_GROUP * WINDOW, 2 * WINDOW)),
                  _resident((WINDOW, 2 * WINDOW)),
                  pl.BlockSpec((rows, D_MODEL), cur),
                  _resident((D_MODEL, D_MODEL)), _resident((1, D_MODEL))],
        out_specs=(pl.BlockSpec((rows, D_MODEL), cur), pl.BlockSpec((rows, D_MODEL), cur)),
        out_shape=(jax.ShapeDtypeStruct((bsz * seq, D_MODEL), F32), jax.ShapeDtypeStruct((bsz * seq, D_MODEL), BF16)),
        compiler_params=_params("arbitrary", "arbitrary"),
        name="mix_out",
    )(sinks, q, kb, kb, vb, vb, qx, mkb, mvb, b, u, u, conv_w, bias_p, jnp.asarray(band, F32), x2d, w_out, g_mlp)


def _mix_sample_kernel(t_len, sink_ref, q_ref, kn_ref, vn_ref, ck_ref, cv_ref, qx_ref, cmk_ref, cmv_ref,
                       b_ref, u_ref, cc_ref, cw_ref, bias_ref, o_ref, sk_ref, sv_ref):
    nb = cc_ref.shape[0]
    n_cache = SWA_KV_HEADS * WINDOW
    n_new = SWA_KV_HEADS * t_len
    n_mem = X_HEADS * MEM_TOKENS
    log_t = int(math.log2(t_len))

    rows = SWA_Q_HEADS * t_len
    r = lax.broadcasted_iota(jnp.int32, (nb * rows, SAMPLE_KEY_ROWS), 0) & (rows - 1)
    c = lax.broadcasted_iota(jnp.int32, (nb * rows, SAMPLE_KEY_ROWS), 1)
    key_pos = jnp.where(c < n_cache, c >> 1, WINDOW + ((c - n_cache) >> 1))
    dist = (r & (t_len - 1)) + WINDOW - key_pos
    valid = (dist >= 0) & (dist < WINDOW) & ((c & (SWA_KV_HEADS - 1)) == (r >> int(math.log2(SWA_GROUP * t_len))))
    bias = jnp.concatenate([bias_ref[...]] * nb, axis=0)
    sink_col = jnp.concatenate([jnp.full((t_len, 1), sink_ref[hh], F32) for hh in range(SWA_Q_HEADS)] * nb, axis=0)
    xrows = X_HEADS * t_len
    xr = lax.broadcasted_iota(jnp.int32, (nb * xrows, n_mem), 0) & (xrows - 1)
    xc = lax.broadcasted_iota(jnp.int32, (nb * xrows, n_mem), 1)
    x_valid = (xc & (X_HEADS - 1)) == (xr >> log_t)
    cw = cw_ref[...]
    trow = lax.broadcasted_iota(jnp.int32, (t_len, CONV_WIDTH), 0)
    zeros_pad = jnp.zeros((SAMPLE_KEY_ROWS - n_cache - n_new, HEAD_DIM), F32)

    s_list, sx_list = [], []
    for bi in range(nb):
        rsl = slice(bi * t_len, (bi + 1) * t_len)
        c0, n0, m0 = bi * n_cache, bi * n_new, bi * n_mem

        for dst, cache, new in ((sk_ref, ck_ref, kn_ref), (sv_ref, cv_ref, vn_ref)):
            dst[c0:c0 + n_cache - n_new, :] = cache[c0 + n_new:c0 + n_cache, :]
            dst[c0 + n_cache - n_new:c0 + n_cache, :] = new[n0:n0 + n_new, :]

        u = u_ref[rsl, :]
        cc = cc_ref[bi]
        cc1 = jnp.broadcast_to(cc[1:2], u.shape)
        cc0 = jnp.broadcast_to(cc[0:1], u.shape)
        u_m1 = jnp.where(trow >= 1, pltpu.roll(u, 1, 0), cc1)
        u_m2 = jnp.where(trow >= 2, pltpu.roll(u, 2, 0), jnp.where(trow == 1, cc1, cc0))
        conv = cw[0:1] * u_m2
        conv = conv + cw[1:2] * u_m1
        conv = conv + cw[2:3] * u
        o_ref[rsl, 0:CONV_WIDTH] = (b_ref[rsl, :] * conv).astype(o_ref.dtype)

        qb = q_ref[rsl, :]
        q_rows = jnp.concatenate([qb[:, hh * HEAD_DIM:(hh + 1) * HEAD_DIM] for hh in range(SWA_Q_HEADS)], axis=0)
        k_all = jnp.concatenate([ck_ref[c0:c0 + n_cache, :], kn_ref[n0:n0 + n_new, :], zeros_pad], axis=0)
        s_list.append(lax.dot_general(q_rows.astype(BF16), k_all.astype(BF16), NT_DIMS, preferred_element_type=F32))
        qxb = qx_ref[rsl, :]
        qx_rows = jnp.concatenate([qxb[:, hx * HEAD_DIM:(hx + 1) * HEAD_DIM] for hx in range(X_HEADS)], axis=0)
        mk = cmk_ref[m0:m0 + n_mem, :].astype(BF16)
        sx_list.append(lax.dot_general(qx_rows.astype(BF16), mk, NT_DIMS, preferred_element_type=F32))

    s = jnp.concatenate(s_list, axis=0) * SCALE
    w = _sink_softmax(jnp.where(valid, s + bias, NEG), sink_col).astype(BF16)
    sx = jnp.concatenate(sx_list, axis=0) * SCALE
    wx = _softmax(jnp.where(x_valid, sx, NEG)).astype(BF16)

    for bi in range(nb):
        rsl = slice(bi * t_len, (bi + 1) * t_len)
        c0, n0, m0 = bi * n_cache, bi * n_new, bi * n_mem
        v_all = jnp.concatenate([cv_ref[c0:c0 + n_cache, :], vn_ref[n0:n0 + n_new, :], zeros_pad], axis=0)
        o = jnp.dot(w[bi * rows:(bi + 1) * rows], v_all.astype(BF16), preferred_element_type=F32)
        for hh in range(SWA_Q_HEADS):
            col = CONV_WIDTH + hh * HEAD_DIM
            o_ref[rsl, col:col + HEAD_DIM] = o[hh * t_len:(hh + 1) * t_len].astype(o_ref.dtype)
        mv = cmv_ref[m0:m0 + n_mem, :].astype(BF16)
        ox = jnp.dot(wx[bi * xrows:(bi + 1) * xrows], mv, preferred_element_type=F32)
        for hx in range(X_HEADS):
            col = CONV_WIDTH + SWA_WIDTH + hx * HEAD_DIM
            o_ref[rsl, col:col + HEAD_DIM] = ox[hx * t_len:(hx + 1) * t_len].astype(o_ref.dtype)


def _mix_sample(sinks, q, k_rows, v_rows, cache_k, cache_v, qx, cache_mk, cache_mv, b, u, cache_conv, conv_w,
                bias_s, bsz, t_len):
    assert t_len == SUBLANES and bsz % SAMPLE_BATCH_TILE == 0 and SWA_KV_HEADS == 2
    nb = SAMPLE_BATCH_TILE
    n_cache, n_new, n_mem = SWA_KV_HEADS * WINDOW, SWA_KV_HEADS * t_len, X_HEADS * MEM_TOKENS
    row = lambda i: (i, 0)
    rows_of = lambda n, w: pl.BlockSpec((nb * n, w), row)
    return pl.pallas_call(
        functools.partial(_mix_sample_kernel, t_len),
        grid=(bsz // nb,),
        in_specs=[pl.BlockSpec(memory_space=pltpu.SMEM),
                  rows_of(t_len, SWA_WIDTH),
                  rows_of(n_new, HEAD_DIM), rows_of(n_new, HEAD_DIM),
                  rows_of(n_cache, HEAD_DIM), rows_of(n_cache, HEAD_DIM),
                  rows_of(t_len, X_WIDTH),
                  rows_of(n_mem, HEAD_DIM), rows_of(n_mem, HEAD_DIM),
                  rows_of(t_len, CONV_WIDTH), rows_of(t_len, CONV_WIDTH),
                  pl.BlockSpec((nb, CONV_K - 1, CONV_WIDTH), lambda i: (i, 0, 0)),
                  _resident((CONV_K, CONV_WIDTH)),
                  _resident((SWA_Q_HEADS * t_len, SAMPLE_KEY_ROWS))],
        out_specs=(rows_of(t_len, D_MODEL), rows_of(n_cache, HEAD_DIM), rows_of(n_cache, HEAD_DIM)),
        out_shape=(jax.ShapeDtypeStruct((bsz * t_len, D_MODEL), F32),
                   jax.ShapeDtypeStruct((bsz * n_cache, HEAD_DIM), F32),
                   jax.ShapeDtypeStruct((bsz * n_cache, HEAD_DIM), F32)),
        compiler_params=_params("arbitrary"),
        name="mix_sample",
    )(sinks, q, k_rows, v_rows, cache_k, cache_v, qx, cache_mk, cache_mv, b, u, cache_conv, conv_w, bias_s)


def _outproj_kernel(x_ref, y_ref, w_ref, g_ref, x1_ref, h_ref):
    for r0 in range(0, x_ref.shape[0], ROW_CHUNK):
        rows = slice(r0, r0 + ROW_CHUNK)
        x1 = x_ref[rows, :] + jnp.dot(y_ref[rows, :].astype(BF16), w_ref[...], preferred_element_type=F32)
        x1_ref[rows, :] = x1
        h_ref[rows, :] = _rms(x1, g_ref[...]).astype(BF16)


def _out_projection(x2d, y, w_out, g_mlp):
    n = x2d.shape[0]
    tile = min(ROW_TILE, n)
    row = lambda i: (i, 0)
    return pl.pallas_call(
        _outproj_kernel,
        grid=(n // tile,),
        in_specs=[pl.BlockSpec((tile, D_MODEL), row), pl.BlockSpec((tile, D_MODEL), row),
                  _resident((D_MODEL, D_MODEL)), _resident((1, D_MODEL))],
        out_specs=(pl.BlockSpec((tile, D_MODEL), row), pl.BlockSpec((tile, D_MODEL), row)),
        out_shape=(jax.ShapeDtypeStruct((n, D_MODEL), F32), jax.ShapeDtypeStruct((n, D_MODEL), BF16)),
        compiler_params=_params("arbitrary"),
        name="outproj",
    )(x2d, y, w_out, g_mlp)


def _mlp_kernel(x1_ref, h_ref, wu_ref, wd_ref, o_ref):
    @pl.when(pl.program_id(1) == 0)
    def _():
        o_ref[...] = x1_ref[...]

    a = jnp.maximum(jnp.dot(h_ref[...], wu_ref[...], preferred_element_type=F32), 0.0)
    o_ref[...] += jnp.dot((a * a).astype(BF16), wd_ref[...], preferred_element_type=F32)


def _mlp_cast_kernel(x1_hbm, h_ref, wu_ref, wd_ref, o_ref, wub_ref, wdb_ref, sem):
    j = pl.program_id(1)
    residual_copy = pltpu.make_async_copy(x1_hbm, o_ref, sem)

    @pl.when(j == 0)
    def _():
        residual_copy.start()

    wu = wu_ref[...].astype(BF16)
    wub_ref[...] = wu
    a = jnp.maximum(jnp.dot(h_ref[...], wu, preferred_element_type=F32), 0.0)
    a = (a * a).astype(BF16)

    @pl.when(j == 0)
    def _():
        residual_copy.wait()

    for c0 in range(0, D_MODEL, MLP_CAST_FF_TILE):
        cols = slice(c0, c0 + MLP_CAST_FF_TILE)
        wd = wd_ref[:, cols].astype(BF16)
        wdb_ref[:, cols] = wd
        o_ref[:, cols] += jnp.dot(a, wd, preferred_element_type=F32)


def _mlp(x1, h, w_up, w_down):
    n = x1.shape[0]
    tile = min(MLP_ROW_TILE, n)
    rows = pl.BlockSpec((tile, D_MODEL), lambda i, j: (i, 0))
    return pl.pallas_call(
        _mlp_kernel,
        grid=(n // tile, D_FF // MLP_FF_TILE),
        in_specs=[rows, rows,
                  pl.BlockSpec((D_MODEL, MLP_FF_TILE), lambda i, j: (0, j)),
                  pl.BlockSpec((MLP_FF_TILE, D_MODEL), lambda i, j: (j, 0))],
        out_specs=rows,
        out_shape=jax.ShapeDtypeStruct((n, D_MODEL), F32),
        compiler_params=_params("arbitrary", "arbitrary"),
        name="mlp",
    )(x1, h, w_up, w_down)


def _mlp_cast(x1, h, w_up, w_down):
    n = x1.shape[0]
    rows = pl.BlockSpec((n, D_MODEL), lambda i, j: (0, 0))
    up_spec = pl.BlockSpec((D_MODEL, MLP_CAST_FF_TILE), lambda i, j: (0, j))
    down_spec = pl.BlockSpec((MLP_CAST_FF_TILE, D_MODEL), lambda i, j: (j, 0))
    return pl.pallas_call(
        _mlp_cast_kernel,
        grid=(1, D_FF // MLP_CAST_FF_TILE),
        in_specs=[pl.BlockSpec(memory_space=pl.ANY), rows, up_spec, down_spec],
        out_specs=(rows, up_spec, down_spec),
        out_shape=(jax.ShapeDtypeStruct((n, D_MODEL), F32),
                   jax.ShapeDtypeStruct(w_up.shape, BF16), jax.ShapeDtypeStruct(w_down.shape, BF16)),
        scratch_shapes=[pltpu.SemaphoreType.DMA(())],
        compiler_params=_params("arbitrary", "arbitrary"),
        name="mlp_cast",
    )(x1, h, w_up, w_down)


def kernel(x_prompt, x_sample, mem_prompt, cache_conv, cache_swa_k, cache_swa_v, cache_mem_k, cache_mem_v,
           rel_bias_table, g_mix, w_in, conv_w, g_q_swa, g_k_swa, sinks, g_q_x, g_k_x, g_mem,
           w_mem_k, w_mem_v, w_out, g_mlp, w_up, w_down):
    depth = w_in.shape[0]
    bsz, seq, _ = x_prompt.shape
    dbsz, t_len, _ = x_sample.shape
    xp = x_prompt.reshape(bsz * seq, D_MODEL)
    xs = x_sample.reshape(dbsz * t_len, D_MODEL)
    mem2d = mem_prompt.reshape(bsz * MEM_TOKENS, D_MODEL)
    bias_p, bias_s = _bias_tables(rel_bias_table, t_len)

    outs = [[] for _ in range(8)]
    for l in range(depth):
        vec = lambda a: a[l].reshape(1, -1)
        mk, mv, mkb, mvb, wi = _memory_kv(mem2d, vec(g_mem), w_mem_k[l], w_mem_v[l], vec(g_k_x),
                                          round_weights=(w_in[l],))
        proj = functools.partial(_projections, g_mix=vec(g_mix), w_in=wi, g_q=vec(g_q_swa), g_k=vec(g_k_swa),
                                 g_qx=vec(g_q_x))
        pb, pu, pq, pk, pv, pkb, pvb, pqx, wo = proj(xp, q_dtype=BF16, round_weights=(w_out[l],))

        head_rows = lambda a: a.reshape(-1, HEAD_DIM)
        b, u, q, k, v, _, _, qx = proj(xs, q_dtype=F32)
        y, sk, sv = _mix_sample(
            sinks[l], q, k, v, head_rows(cache_swa_k[l]), head_rows(cache_swa_v[l]),
            qx, head_rows(cache_mem_k[l]), head_rows(cache_mem_v[l]),
            b, u, cache_conv[l], conv_w[l], bias_s, dbsz, t_len)
        x1, h = _out_projection(xs, y, wo, vec(g_mlp))
        xs, wu, wd = _mlp_cast(x1, h, w_up[l], w_down[l])
        outs[5].append(u.reshape(dbsz, t_len, CONV_WIDTH)[:, t_len - (CONV_K - 1):])
        outs[6].append(sk.reshape(dbsz, WINDOW, SWA_KV_HEADS, HEAD_DIM))
        outs[7].append(sv.reshape(dbsz, WINDOW, SWA_KV_HEADS, HEAD_DIM))

        x1, h = _mix_out_prompt(sinks[l], pq, pkb, pvb, pqx, mkb, mvb, pb, pu, conv_w[l], bias_p, xp, wo,
                                vec(g_mlp), bsz, seq)
        xp = _mlp(x1, h, wu, wd)
        outs[0].append(pu.reshape(bsz, seq, CONV_WIDTH)[:, seq - (CONV_K - 1):])
        last_window = lambda a: a.reshape(bsz, seq, SWA_KV_HEADS, HEAD_DIM)[:, seq - WINDOW:]
        outs[1].append(last_window(pk))
        outs[2].append(last_window(pv))
        outs[3].append(mk.reshape(bsz, MEM_TOKENS, X_HEADS, HEAD_DIM))
        outs[4].append(mv.reshape(bsz, MEM_TOKENS, X_HEADS, HEAD_DIM))

    return (xp.reshape(bsz, seq, D_MODEL), xs.reshape(dbsz, t_len, D_MODEL)) + tuple(jnp.stack(o) for o in outs)
```

```python
import functools
import math

import numpy as np
import jax
import jax.numpy as jnp
from jax import lax
from jax.experimental import pallas as pl
from jax.experimental.pallas import tpu as pltpu

D_MODEL = 2048
HEAD_DIM = 128
SWA_Q_HEADS = 8
SWA_KV_HEADS = 2
SWA_GROUP = SWA_Q_HEADS // SWA_KV_HEADS
SWA_WIDTH = SWA_Q_HEADS * HEAD_DIM
SWA_KV_WIDTH = SWA_KV_HEADS * HEAD_DIM
X_HEADS = 4
X_WIDTH = X_HEADS * HEAD_DIM
CONV_WIDTH = D_MODEL - SWA_WIDTH - X_WIDTH
CONV_K = 3
WINDOW = 128
NUM_BUCKETS = 32
MAX_DISTANCE = WINDOW
MEM_TOKENS = 256
D_FF = 4 * D_MODEL
EPS = 1e-6
NEG = -1e30
SCALE = HEAD_DIM ** -0.5
LOG2E = math.log2(math.e)
SCALE_LOG2 = SCALE * LOG2E

OFF_B = 0
OFF_C = CONV_WIDTH
OFF_H = 2 * CONV_WIDTH
OFF_Q = 3 * CONV_WIDTH
OFF_K = OFF_Q + SWA_WIDTH
OFF_V = OFF_K + SWA_KV_WIDTH
OFF_QX = OFF_V + SWA_KV_WIDTH
IN_WIDTH = OFF_QX + X_WIDTH

VMEM_LIMIT_V7X = 56 * 1024 * 1024
SUBLANES = 8

ROW_TILE = 512
ROW_CHUNK = 256
MLP_ROW_TILE = 512
MLP_FF_TILE = 1024
MLP_CAST_FF_TILE = 512
SAMPLE_BATCH_TILE = 8
MIX_BLOCKS = 4

BF16 = jnp.bfloat16
F32 = jnp.float32
NT_DIMS = (((1,), (1,)), ((), ()))


def _params(*sem):
    return pltpu.CompilerParams(dimension_semantics=sem, vmem_limit_bytes=VMEM_LIMIT_V7X)


def _resident(shape):
    nd = len(shape)
    return pl.BlockSpec(shape, lambda *_: (0,) * nd, pipeline_mode=pl.Buffered(1))


def _rms(x, g):
    ms = jnp.mean(x * x, axis=-1, keepdims=True)
    return x * lax.rsqrt(ms + EPS) * g


def _rel_bucket_np(dist):
    n = np.maximum(dist, 0)
    max_exact = NUM_BUCKETS // 2
    nf = np.maximum(n, 1).astype(np.float32)
    large = max_exact + (np.log(nf / np.float32(max_exact)) / np.float32(math.log(MAX_DISTANCE / max_exact))
                         * np.float32(NUM_BUCKETS - max_exact)).astype(np.int32)
    large = np.minimum(large, NUM_BUCKETS - 1)
    return np.where(n < max_exact, n, large).astype(np.int32)


SAMPLE_KEY_ROWS = 3 * WINDOW


def _sample_key_index(t_len):
    c = np.arange(SAMPLE_KEY_ROWS)
    n_cache = SWA_KV_HEADS * WINDOW
    assert n_cache + SWA_KV_HEADS * t_len <= SAMPLE_KEY_ROWS
    pos = np.where(c < n_cache, c // SWA_KV_HEADS, WINDOW + (c - n_cache) // SWA_KV_HEADS)
    return pos, c % SWA_KV_HEADS


def _bias_kernel(tab_ref, bp_ref, bs_ref, op_ref, os_ref):
    bp = bp_ref[...]
    bs = bs_ref[...]
    for hh in range(SWA_Q_HEADS):
        accp = jnp.zeros(bp.shape, F32)
        accs = jnp.zeros(bs.shape, F32)
        for k in range(NUM_BUCKETS):
            t = tab_ref[k * SWA_Q_HEADS + hh] * LOG2E
            accp = jnp.where(bp == k, t, accp)
            accs = jnp.where(bs == k, t, accs)
        h, g = divmod(hh, SWA_GROUP)
        op_ref[h, g * WINDOW:(g + 1) * WINDOW, :] = accp
        os_ref[hh * SUBLANES:(hh + 1) * SUBLANES, :] = accs


def _bias_tables(table, t_len):
    qi = np.arange(WINDOW)[:, None]
    kj = np.arange(2 * WINDOW)[None, :]
    bkt_p = _rel_bucket_np(WINDOW + qi - kj)
    key_pos, _ = _sample_key_index(t_len)
    bkt_s = _rel_bucket_np(np.arange(t_len)[:, None] + WINDOW - key_pos[None, :])
    return pl.pallas_call(
        _bias_kernel,
        out_shape=(jax.ShapeDtypeStruct((SWA_KV_HEADS, SWA_GROUP * WINDOW, 2 * WINDOW), F32),
                   jax.ShapeDtypeStruct((SWA_Q_HEADS * t_len, SAMPLE_KEY_ROWS), F32)),
        in_specs=[pl.BlockSpec(memory_space=pltpu.SMEM),
                  pl.BlockSpec(memory_space=pltpu.VMEM),
                  pl.BlockSpec(memory_space=pltpu.VMEM)],
        out_specs=(pl.BlockSpec(memory_space=pltpu.VMEM), pl.BlockSpec(memory_space=pltpu.VMEM)),
        name="bias",
    )(table.reshape(-1), jnp.asarray(bkt_p), jnp.asarray(bkt_s))


def _round_specs(weights, steps):
    specs = [pl.BlockSpec((w.shape[0] // steps, w.shape[1]), lambda i: (i, 0)) for w in weights]
    shapes = [jax.ShapeDtypeStruct(w.shape, BF16) for w in weights]
    return specs, shapes


def _round_slabs(srcs, dsts):
    for src, dst in zip(srcs, dsts):
        dst[...] = src[...].astype(BF16)


def _memkv_kernel(n_round, m_ref, g_ref, wk_ref, wv_ref, gk_ref, *refs):
    mk_ref, mv_ref, mkb_ref, mvb_ref = refs[n_round:n_round + 4]
    _round_slabs(refs[:n_round], refs[n_round + 4:])
    h = _rms(m_ref[...], g_ref[...]).astype(BF16)
    zk = jnp.dot(h, wk_ref[...].astype(BF16), preferred_element_type=F32)
    zv = jnp.dot(h, wv_ref[...].astype(BF16), preferred_element_type=F32)
    gk = gk_ref[...]
    tokens = m_ref.shape[0]
    for hx in range(X_HEADS):
        sl = slice(hx * HEAD_DIM, (hx + 1) * HEAD_DIM)
        head_rows = pl.ds(hx, tokens, stride=X_HEADS)
        mk = _rms(zk[:, sl], gk)
        mk_ref[head_rows, :] = mk
        mv_ref[head_rows, :] = zv[:, sl]
        mkb_ref[:, sl] = mk.astype(BF16)
    mvb_ref[...] = zv.astype(BF16)


def _memory_kv(mem2d, g_mem, wk, wv, g_k_x, round_weights=()):
    n = mem2d.shape[0]
    tile = MEM_TOKENS
    row = lambda i: (i, 0)
    head_rows = pl.BlockSpec((tile * X_HEADS, HEAD_DIM), row)
    round_specs, round_shapes = _round_specs(round_weights, n // tile)
    return pl.pallas_call(
        functools.partial(_memkv_kernel, len(round_weights)),
        grid=(n // tile,),
        in_specs=[pl.BlockSpec((tile, D_MODEL), row), _resident((1, D_MODEL)),
                  _resident((D_MODEL, X_WIDTH)), _resident((D_MODEL, X_WIDTH)), _resident((1, HEAD_DIM)),
                  *round_specs],
        out_specs=(head_rows, head_rows, pl.BlockSpec((tile, X_WIDTH), row), pl.BlockSpec((tile, X_WIDTH), row),
                   *round_specs),
        out_shape=(jax.ShapeDtypeStruct((n * X_HEADS, HEAD_DIM), F32),
                   jax.ShapeDtypeStruct((n * X_HEADS, HEAD_DIM), F32),
                   jax.ShapeDtypeStruct((n, X_WIDTH), BF16), jax.ShapeDtypeStruct((n, X_WIDTH), BF16),
                   *round_shapes),
        compiler_params=_params("arbitrary"),
        name="memkv",
    )(mem2d, g_mem, wk, wv, g_k_x, *round_weights)


def _proj_kernel(n_round, x_ref, g_ref, w_ref, gq_ref, gk_ref, gx_ref, *refs):
    b_ref, u_ref, q_ref, k_ref, v_ref, kb_ref, vb_ref, qx_ref = refs[n_round:n_round + 8]
    _round_slabs(refs[:n_round], refs[n_round + 8:])
    for r0 in range(0, x_ref.shape[0], ROW_CHUNK):
        rows = slice(r0, r0 + ROW_CHUNK)
        h = _rms(x_ref[rows, :], g_ref[...]).astype(BF16)

        def seg(lo, width):
            return jnp.dot(h, w_ref[:, lo:lo + width], preferred_element_type=F32)

        b_ref[rows, :] = seg(OFF_B, CONV_WIDTH)
        u_ref[rows, :] = seg(OFF_C, CONV_WIDTH) * seg(OFF_H, CONV_WIDTH)

        def head_norm(z, g, n_heads, out):
            for hh in range(n_heads):
                sl = slice(hh * HEAD_DIM, (hh + 1) * HEAD_DIM)
                out[rows, sl] = _rms(z[:, sl], g).astype(out.dtype)

        head_norm(seg(OFF_Q, SWA_WIDTH), gq_ref[...], SWA_Q_HEADS, q_ref)
        head_norm(seg(OFF_QX, X_WIDTH), gx_ref[...], X_HEADS, qx_ref)
        zk = seg(OFF_K, SWA_KV_WIDTH)
        zv = seg(OFF_V, SWA_KV_WIDTH)
        vb_ref[rows, :] = zv.astype(BF16)
        for hh in range(SWA_KV_HEADS):
            sl = slice(hh * HEAD_DIM, (hh + 1) * HEAD_DIM)
            head_rows = pl.ds(r0 * SWA_KV_HEADS + hh, ROW_CHUNK, stride=SWA_KV_HEADS)
            k = _rms(zk[:, sl], gk_ref[...])
            k_ref[head_rows, :] = k
            v_ref[head_rows, :] = zv[:, sl]
            kb_ref[rows, sl] = k.astype(BF16)


def _projections(x2d, g_mix, w_in, g_q, g_k, g_qx, q_dtype, round_weights=()):
    n = x2d.shape[0]
    tile = min(ROW_TILE, n)
    row = lambda i: (i, 0)
    outs = ((1, CONV_WIDTH, F32), (1, CONV_WIDTH, F32), (1, SWA_WIDTH, q_dtype),
            (SWA_KV_HEADS, HEAD_DIM, F32), (SWA_KV_HEADS, HEAD_DIM, F32),
            (1, SWA_KV_WIDTH, BF16), (1, SWA_KV_WIDTH, BF16), (1, X_WIDTH, q_dtype))
    round_specs, round_shapes = _round_specs(round_weights, n // tile)
    return pl.pallas_call(
        functools.partial(_proj_kernel, len(round_weights)),
        grid=(n // tile,),
        in_specs=[pl.BlockSpec((tile, D_MODEL), row), _resident((1, D_MODEL)), _resident((D_MODEL, IN_WIDTH)),
                  _resident((1, HEAD_DIM)), _resident((1, HEAD_DIM)), _resident((1, HEAD_DIM)), *round_specs],
        out_specs=(*(pl.BlockSpec((tile * r, w), row) for r, w, _ in outs), *round_specs),
        out_shape=(*(jax.ShapeDtypeStruct((n * r, w), dt) for r, w, dt in outs), *round_shapes),
        compiler_params=_params("arbitrary"),
        name="proj",
    )(x2d, g_mix, w_in, g_q, g_k, g_qx, *round_weights)


def _softmax_parts(s, sink=None):
    m = jnp.max(s, axis=-1, keepdims=True)
    if sink is not None:
        m = jnp.maximum(m, sink)
    p = jnp.exp2(s - m)
    den = jnp.sum(p, axis=-1, keepdims=True)
    if sink is not None:
        den = den + jnp.exp2(sink - m)
    return p, 1.0 / den


def _mix_out_kernel(sink_ref, q_ref, kc_ref, kp_ref, vc_ref, vp_ref, qx_ref, mk_ref, mv_ref,
                    b_ref, uc_ref, up_ref, cw_ref, bias_ref, band_ref, x_ref, w_ref, g_ref, x1_ref, h_ref):
    has_prev = pl.program_id(1) > 0
    rows = MIX_BLOCKS * WINDOW
    blocks = [slice(j * WINDOW, (j + 1) * WINDOW) for j in range(MIX_BLOCKS)]

    def project(y_cols, col0, acc):
        y = jnp.concatenate(y_cols, axis=1).astype(BF16)
        return acc + jnp.dot(y, w_ref[col0:col0 + y.shape[1], :], preferred_element_type=F32)

    u = uc_ref[...]
    prev = jnp.where(has_prev, up_ref[...], 0.0)
    ext = jnp.concatenate([prev, u], axis=0)
    cw = cw_ref[...]
    conv = cw[0:1] * ext[SUBLANES - 2:SUBLANES - 2 + rows]
    conv = conv + cw[1:2] * ext[SUBLANES - 1:SUBLANES - 1 + rows]
    conv = conv + cw[2:3] * u
    x1 = project([b_ref[...] * conv], 0, x_ref[...])

    for h in range(SWA_KV_HEADS):
        ksl = slice(h * HEAD_DIM, (h + 1) * HEAD_DIM)
        outs = [[None] * MIX_BLOCKS for _ in range(SWA_GROUP)]
        for j, rsl in enumerate(blocks):
            min_band = jnp.where(has_prev, 0.5, 1.5) if j == 0 else 0.5
            if j == 0:
                k_all = jnp.concatenate([kp_ref[:, ksl], kc_ref[rsl, ksl]], axis=0)
                v_all = jnp.concatenate([vp_ref[:, ksl], vc_ref[rsl, ksl]], axis=0)
            else:
                k_all = kc_ref[(j - 1) * WINDOW:(j + 1) * WINDOW, ksl]
                v_all = vc_ref[(j - 1) * WINDOW:(j + 1) * WINDOW, ksl]
            for g in range(SWA_GROUP):
                hh = h * SWA_GROUP + g
                q = q_ref[rsl, hh * HEAD_DIM:(hh + 1) * HEAD_DIM]
                s = lax.dot_general(q, k_all, NT_DIMS, preferred_element_type=F32) * SCALE_LOG2
                s = jnp.where(band_ref[...] > min_band, s + bias_ref[h, g * WINDOW:(g + 1) * WINDOW, :], NEG)
                p, inv = _softmax_parts(s, sink_ref[hh] * LOG2E)
                outs[g][j] = (jnp.dot(p.astype(BF16), v_all, preferred_element_type=F32) * inv).astype(BF16)
        x1 = project([jnp.concatenate(o, axis=0) for o in outs], CONV_WIDTH + h * SWA_GROUP * HEAD_DIM, x1)

    outs = [[None] * MIX_BLOCKS for _ in range(X_HEADS)]
    for j, rsl in enumerate(blocks):
        for hx in range(X_HEADS):
            sl = slice(hx * HEAD_DIM, (hx + 1) * HEAD_DIM)
            s = lax.dot_general(qx_ref[rsl, sl], mk_ref[:, sl], NT_DIMS, preferred_element_type=F32) * SCALE_LOG2
            p, inv = _softmax_parts(s)
            outs[hx][j] = (jnp.dot(p.astype(BF16), mv_ref[:, sl], preferred_element_type=F32) * inv).astype(BF16)
    x1 = project([jnp.concatenate(o, axis=0) for o in outs], CONV_WIDTH + SWA_WIDTH, x1)

    x1_ref[...] = x1
    h_ref[...] = _rms(x1, g_ref[...]).astype(BF16)


def _mix_out_prompt(sinks, q, kb, vb, qx, mkb, mvb, b, u, conv_w, bias_p, x2d, w_out, g_mlp, bsz, seq):
    rows = MIX_BLOCKS * WINDOW
    steps = seq // rows
    cur = lambda bi, i: (bi * steps + i, 0)
    prv = lambda bi, i: (jnp.maximum((bi * steps + i) * MIX_BLOCKS - 1, 0), 0)
    prv8 = lambda bi, i: (jnp.maximum((bi * steps + i) * (rows // SUBLANES) - 1, 0), 0)
    per_b = lambda bi, i: (bi, 0)
    dist = WINDOW + np.arange(WINDOW)[:, None] - np.arange(2 * WINDOW)[None, :]
    band = np.where((dist >= 0) & (dist < WINDOW), np.where(np.arange(2 * WINDOW)[None, :] < WINDOW, 1.0, 2.0), 0.0)
    return pl.pallas_call(
        _mix_out_kernel,
        grid=(bsz, steps),
        in_specs=[pl.BlockSpec(memory_space=pltpu.SMEM),
                  pl.BlockSpec((rows, SWA_WIDTH), cur),
                  pl.BlockSpec((rows, SWA_KV_WIDTH), cur), pl.BlockSpec((WINDOW, SWA_KV_WIDTH), prv),
                  pl.BlockSpec((rows, SWA_KV_WIDTH), cur), pl.BlockSpec((WINDOW, SWA_KV_WIDTH), prv),
                  pl.BlockSpec((rows, X_WIDTH), cur),
                  pl.BlockSpec((MEM_TOKENS, X_WIDTH), per_b), pl.BlockSpec((MEM_TOKENS, X_WIDTH), per_b),
                  pl.BlockSpec((rows, CONV_WIDTH), cur), pl.BlockSpec((rows, CONV_WIDTH), cur),
                  pl.BlockSpec((SUBLANES, CONV_WIDTH), prv8),
                  _resident((CONV_K, CONV_WIDTH)),
                  _resident((SWA_KV_HEADS, SWA_GROUP * WINDOW, 2 * WINDOW)),
                  _resident((WINDOW, 2 * WINDOW)),
                  pl.BlockSpec((rows, D_MODEL), cur),
                  _resident((D_MODEL, D_MODEL)), _resident((1, D_MODEL))],
        out_specs=(pl.BlockSpec((rows, D_MODEL), cur), pl.BlockSpec((rows, D_MODEL), cur)),
        out_shape=(jax.ShapeDtypeStruct((bsz * seq, D_MODEL), F32), jax.ShapeDtypeStruct((bsz * seq, D_MODEL), BF16)),
        compiler_params=_params("arbitrary", "arbitrary"),
        name="mix_out",
    )(sinks, q, kb, kb, vb, vb, qx, mkb, mvb, b, u, u, conv_w, bias_p, jnp.asarray(band, F32), x2d, w_out, g_mlp)


def _mix_sample_kernel(t_len, sink_ref, q_ref, kn_ref, vn_ref, ck_ref, cv_ref, qx_ref, cmk_ref, cmv_ref,
                       b_ref, u_ref, cc_ref, cw_ref, bias_ref, o_ref, sk_ref, sv_ref):
    nb = cc_ref.shape[0]
    n_cache = SWA_KV_HEADS * WINDOW
    n_new = SWA_KV_HEADS * t_len
    n_mem = X_HEADS * MEM_TOKENS
    log_t = int(math.log2(t_len))

    rows = SWA_Q_HEADS * t_len
    r = lax.broadcasted_iota(jnp.int32, (nb * rows, SAMPLE_KEY_ROWS), 0) & (rows - 1)
    c = lax.broadcasted_iota(jnp.int32, (nb * rows, SAMPLE_KEY_ROWS), 1)
    key_pos = jnp.where(c < n_cache, c >> 1, WINDOW + ((c - n_cache) >> 1))
    dist = (r & (t_len - 1)) + WINDOW - key_pos
    valid = (dist >= 0) & (dist < WINDOW) & ((c & (SWA_KV_HEADS - 1)) == (r >> int(math.log2(SWA_GROUP * t_len))))
    bias = jnp.concatenate([bias_ref[...]] * nb, axis=0)
    sink_col = jnp.concatenate([jnp.full((t_len, 1), sink_ref[hh], F32) for hh in range(SWA_Q_HEADS)] * nb, axis=0)
    xrows = X_HEADS * t_len
    xr = lax.broadcasted_iota(jnp.int32, (nb * xrows, n_mem), 0) & (xrows - 1)
    xc = lax.broadcasted_iota(jnp.int32, (nb * xrows, n_mem), 1)
    x_valid = (xc & (X_HEADS - 1)) == (xr >> log_t)
    cw = cw_ref[...]
    trow = lax.broadcasted_iota(jnp.int32, (t_len, CONV_WIDTH), 0)
    zeros_pad = jnp.zeros((SAMPLE_KEY_ROWS - n_cache - n_new, HEAD_DIM), F32)

    s_list, sx_list = [], []
    for bi in range(nb):
        rsl = slice(bi * t_len, (bi + 1) * t_len)
        c0, n0, m0 = bi * n_cache, bi * n_new, bi * n_mem

        for dst, cache, new in ((sk_ref, ck_ref, kn_ref), (sv_ref, cv_ref, vn_ref)):
            dst[c0:c0 + n_cache - n_new, :] = cache[c0 + n_new:c0 + n_cache, :]
            dst[c0 + n_cache - n_new:c0 + n_cache, :] = new[n0:n0 + n_new, :]

        u = u_ref[rsl, :]
        cc = cc_ref[bi]
        cc1 = jnp.broadcast_to(cc[1:2], u.shape)
        cc0 = jnp.broadcast_to(cc[0:1], u.shape)
        u_m1 = jnp.where(trow >= 1, pltpu.roll(u, 1, 0), cc1)
        u_m2 = jnp.where(trow >= 2, pltpu.roll(u, 2, 0), jnp.where(trow == 1, cc1, cc0))
        conv = cw[0:1] * u_m2
        conv = conv + cw[1:2] * u_m1
        conv = conv + cw[2:3] * u
        o_ref[rsl, 0:CONV_WIDTH] = (b_ref[rsl, :] * conv).astype(o_ref.dtype)

        qb = q_ref[rsl, :]
        q_rows = jnp.concatenate([qb[:, hh * HEAD_DIM:(hh + 1) * HEAD_DIM] for hh in range(SWA_Q_HEADS)], axis=0)
        k_all = jnp.concatenate([ck_ref[c0:c0 + n_cache, :], kn_ref[n0:n0 + n_new, :], zeros_pad], axis=0)
        s_list.append(lax.dot_general(q_rows.astype(BF16), k_all.astype(BF16), NT_DIMS, preferred_element_type=F32))
        qxb = qx_ref[rsl, :]
        qx_rows = jnp.concatenate([qxb[:, hx * HEAD_DIM:(hx + 1) * HEAD_DIM] for hx in range(X_HEADS)], axis=0)
        mk = cmk_ref[m0:m0 + n_mem, :].astype(BF16)
        sx_list.append(lax.dot_general(qx_rows.astype(BF16), mk, NT_DIMS, preferred_element_type=F32))

    s = jnp.concatenate(s_list, axis=0) * SCALE_LOG2
    w, inv = _softmax_parts(jnp.where(valid, s + bias, NEG), sink_col * LOG2E)
    w = w.astype(BF16)
    sx = jnp.concatenate(sx_list, axis=0) * SCALE_LOG2
    wx, invx = _softmax_parts(jnp.where(x_valid, sx, NEG))
    wx = wx.astype(BF16)

    for bi in range(nb):
        rsl = slice(bi * t_len, (bi + 1) * t_len)
        c0, n0, m0 = bi * n_cache, bi * n_new, bi * n_mem
        v_all = jnp.concatenate([cv_ref[c0:c0 + n_cache, :], vn_ref[n0:n0 + n_new, :], zeros_pad], axis=0)
        srows = slice(bi * rows, (bi + 1) * rows)
        o = jnp.dot(w[srows], v_all.astype(BF16), preferred_element_type=F32) * inv[srows]
        for hh in range(SWA_Q_HEADS):
            col = CONV_WIDTH + hh * HEAD_DIM
            o_ref[rsl, col:col + HEAD_DIM] = o[hh * t_len:(hh + 1) * t_len].astype(o_ref.dtype)
        mv = cmv_ref[m0:m0 + n_mem, :].astype(BF16)
        xsl = slice(bi * xrows, (bi + 1) * xrows)
        ox = jnp.dot(wx[xsl], mv, preferred_element_type=F32) * invx[xsl]
        for hx in range(X_HEADS):
            col = CONV_WIDTH + SWA_WIDTH + hx * HEAD_DIM
            o_ref[rsl, col:col + HEAD_DIM] = ox[hx * t_len:(hx + 1) * t_len].astype(o_ref.dtype)


def _mix_sample(sinks, q, k_rows, v_rows, cache_k, cache_v, qx, cache_mk, cache_mv, b, u, cache_conv, conv_w,
                bias_s, bsz, t_len):
    assert t_len == SUBLANES and bsz % SAMPLE_BATCH_TILE == 0 and SWA_KV_HEADS == 2
    nb = SAMPLE_BATCH_TILE
    n_cache, n_new, n_mem = SWA_KV_HEADS * WINDOW, SWA_KV_HEADS * t_len, X_HEADS * MEM_TOKENS
    row = lambda i: (i, 0)
    rows_of = lambda n, w: pl.BlockSpec((nb * n, w), row)
    return pl.pallas_call(
        functools.partial(_mix_sample_kernel, t_len),
        grid=(bsz // nb,),
        in_specs=[pl.BlockSpec(memory_space=pltpu.SMEM),
                  rows_of(t_len, SWA_WIDTH),
                  rows_of(n_new, HEAD_DIM), rows_of(n_new, HEAD_DIM),
                  rows_of(n_cache, HEAD_DIM), rows_of(n_cache, HEAD_DIM),
                  rows_of(t_len, X_WIDTH),
                  rows_of(n_mem, HEAD_DIM), rows_of(n_mem, HEAD_DIM),
                  rows_of(t_len, CONV_WIDTH), rows_of(t_len, CONV_WIDTH),
                  pl.BlockSpec((nb, CONV_K - 1, CONV_WIDTH), lambda i: (i, 0, 0)),
                  _resident((CONV_K, CONV_WIDTH)),
                  _resident((SWA_Q_HEADS * t_len, SAMPLE_KEY_ROWS))],
        out_specs=(rows_of(t_len, D_MODEL), rows_of(n_cache, HEAD_DIM), rows_of(n_cache, HEAD_DIM)),
        out_shape=(jax.ShapeDtypeStruct((bsz * t_len, D_MODEL), F32),
                   jax.ShapeDtypeStruct((bsz * n_cache, HEAD_DIM), F32),
                   jax.ShapeDtypeStruct((bsz * n_cache, HEAD_DIM), F32)),
        compiler_params=_params("arbitrary"),
        name="mix_sample",
    )(sinks, q, k_rows, v_rows, cache_k, cache_v, qx, cache_mk, cache_mv, b, u, cache_conv, conv_w, bias_s)


def _outproj_kernel(x_ref, y_ref, w_ref, g_ref, x1_ref, h_ref):
    for r0 in range(0, x_ref.shape[0], ROW_CHUNK):
        rows = slice(r0, r0 + ROW_CHUNK)
        x1 = x_ref[rows, :] + jnp.dot(y_ref[rows, :].astype(BF16), w_ref[...], preferred_element_type=F32)
        x1_ref[rows, :] = x1
        h_ref[rows, :] = _rms(x1, g_ref[...]).astype(BF16)


def _out_projection(x2d, y, w_out, g_mlp):
    n = x2d.shape[0]
    tile = min(ROW_TILE, n)
    row = lambda i: (i, 0)
    return pl.pallas_call(
        _outproj_kernel,
        grid=(n // tile,),
        in_specs=[pl.BlockSpec((tile, D_MODEL), row), pl.BlockSpec((tile, D_MODEL), row),
                  _resident((D_MODEL, D_MODEL)), _resident((1, D_MODEL))],
        out_specs=(pl.BlockSpec((tile, D_MODEL), row), pl.BlockSpec((tile, D_MODEL), row)),
        out_shape=(jax.ShapeDtypeStruct((n, D_MODEL), F32), jax.ShapeDtypeStruct((n, D_MODEL), BF16)),
        compiler_params=_params("arbitrary"),
        name="outproj",
    )(x2d, y, w_out, g_mlp)


def _mlp_kernel(x1_ref, h_ref, wu_ref, wd_ref, o_ref):
    @pl.when(pl.program_id(1) == 0)
    def _():
        o_ref[...] = x1_ref[...]

    a = jnp.maximum(jnp.dot(h_ref[...], wu_ref[...], preferred_element_type=F32), 0.0)
    o_ref[...] += jnp.dot((a * a).astype(BF16), wd_ref[...], preferred_element_type=F32)


def _mlp_cast_kernel(x1_hbm, h_ref, wu_ref, wd_ref, o_ref, wub_ref, wdb_ref, sem):
    j = pl.program_id(1)
    residual_copy = pltpu.make_async_copy(x1_hbm, o_ref, sem)

    @pl.when(j == 0)
    def _():
        residual_copy.start()

    wu = wu_ref[...].astype(BF16)
    wub_ref[...] = wu
    a = jnp.maximum(jnp.dot(h_ref[...], wu, preferred_element_type=F32), 0.0)
    a = (a * a).astype(BF16)

    @pl.when(j == 0)
    def _():
        residual_copy.wait()

    for c0 in range(0, D_MODEL, MLP_CAST_FF_TILE):
        cols = slice(c0, c0 + MLP_CAST_FF_TILE)
        wd = wd_ref[:, cols].astype(BF16)
        wdb_ref[:, cols] = wd
        o_ref[:, cols] += jnp.dot(a, wd, preferred_element_type=F32)


def _mlp(x1, h, w_up, w_down):
    n = x1.shape[0]
    tile = min(MLP_ROW_TILE, n)
    rows = pl.BlockSpec((tile, D_MODEL), lambda i, j: (i, 0))
    return pl.pallas_call(
        _mlp_kernel,
        grid=(n // tile, D_FF // MLP_FF_TILE),
        in_specs=[rows, rows,
                  pl.BlockSpec((D_MODEL, MLP_FF_TILE), lambda i, j: (0, j)),
                  pl.BlockSpec((MLP_FF_TILE, D_MODEL), lambda i, j: (j, 0))],
        out_specs=rows,
        out_shape=jax.ShapeDtypeStruct((n, D_MODEL), F32),
        compiler_params=_params("arbitrary", "arbitrary"),
        name="mlp",
    )(x1, h, w_up, w_down)


def _mlp_cast(x1, h, w_up, w_down):
    n = x1.shape[0]
    rows = pl.BlockSpec((n, D_MODEL), lambda i, j: (0, 0))
    up_spec = pl.BlockSpec((D_MODEL, MLP_CAST_FF_TILE), lambda i, j: (0, j))
    down_spec = pl.BlockSpec((MLP_CAST_FF_TILE, D_MODEL), lambda i, j: (j, 0))
    return pl.pallas_call(
        _mlp_cast_kernel,
        grid=(1, D_FF // MLP_CAST_FF_TILE),
        in_specs=[pl.BlockSpec(memory_space=pl.ANY), rows, up_spec, down_spec],
        out_specs=(rows, up_spec, down_spec),
        out_shape=(jax.ShapeDtypeStruct((n, D_MODEL), F32),
                   jax.ShapeDtypeStruct(w_up.shape, BF16), jax.ShapeDtypeStruct(w_down.shape, BF16)),
        scratch_shapes=[pltpu.SemaphoreType.DMA(())],
        compiler_params=_params("arbitrary", "arbitrary"),
        name="mlp_cast",
    )(x1, h, w_up, w_down)


def kernel(x_prompt, x_sample, mem_prompt, cache_conv, cache_swa_k, cache_swa_v, cache_mem_k, cache_mem_v,
           rel_bias_table, g_mix, w_in, conv_w, g_q_swa, g_k_swa, sinks, g_q_x, g_k_x, g_mem,
           w_mem_k, w_mem_v, w_out, g_mlp, w_up, w_down):
    depth = w_in.shape[0]
    bsz, seq, _ = x_prompt.shape
    dbsz, t_len, _ = x_sample.shape
    xp = x_prompt.reshape(bsz * seq, D_MODEL)
    xs = x_sample.reshape(dbsz * t_len, D_MODEL)
    mem2d = mem_prompt.reshape(bsz * MEM_TOKENS, D_MODEL)
    bias_p, bias_s = _bias_tables(rel_bias_table, t_len)

    outs = [[] for _ in range(8)]
    for l in range(depth):
        vec = lambda a: a[l].reshape(1, -1)
        mk, mv, mkb, mvb, wi = _memory_kv(mem2d, vec(g_mem), w_mem_k[l], w_mem_v[l], vec(g_k_x),
                                          round_weights=(w_in[l],))
        proj = functools.partial(_projections, g_mix=vec(g_mix), w_in=wi, g_q=vec(g_q_swa), g_k=vec(g_k_swa),
                                 g_qx=vec(g_q_x))
        pb, pu, pq, pk, pv, pkb, pvb, pqx, wo = proj(xp, q_dtype=BF16, round_weights=(w_out[l],))

        head_rows = lambda a: a.reshape(-1, HEAD_DIM)
        b, u, q, k, v, _, _, qx = proj(xs, q_dtype=F32)
        y, sk, sv = _mix_sample(
            sinks[l], q, k, v, head_rows(cache_swa_k[l]), head_rows(cache_swa_v[l]),
            qx, head_rows(cache_mem_k[l]), head_rows(cache_mem_v[l]),
            b, u, cache_conv[l], conv_w[l], bias_s, dbsz, t_len)
        x1, h = _out_projection(xs, y, wo, vec(g_mlp))
        xs, wu, wd = _mlp_cast(x1, h, w_up[l], w_down[l])
        outs[5].append(u.reshape(dbsz, t_len, CONV_WIDTH)[:, t_len - (CONV_K - 1):])
        outs[6].append(sk.reshape(dbsz, WINDOW, SWA_KV_HEADS, HEAD_DIM))
        outs[7].append(sv.reshape(dbsz, WINDOW, SWA_KV_HEADS, HEAD_DIM))

        x1, h = _mix_out_prompt(sinks[l], pq, pkb, pvb, pqx, mkb, mvb, pb, pu, conv_w[l], bias_p, xp, wo,
                                vec(g_mlp), bsz, seq)
        xp = _mlp(x1, h, wu, wd)
        outs[0].append(pu.reshape(bsz, seq, CONV_WIDTH)[:, seq - (CONV_K - 1):])
        last_window = lambda a: a.reshape(bsz, seq, SWA_KV_HEADS, HEAD_DIM)[:, seq - WINDOW:]
        outs[1].append(last_window(pk))
        outs[2].append(last_window(pv))
        outs[3].append(mk.reshape(bsz, MEM_TOKENS, X_HEADS, HEAD_DIM))
        outs[4].append(mv.reshape(bsz, MEM_TOKENS, X_HEADS, HEAD_DIM))

    return (xp.reshape(bsz, seq, D_MODEL), xs.reshape(dbsz, t_len, D_MODEL)) + tuple(jnp.stack(o) for o in outs)
```

```python
import functools
import math

import numpy as np
import jax
import jax.numpy as jnp
from jax import lax
from jax.experimental import pallas as pl
from jax.experimental.pallas import tpu as pltpu

D_MODEL = 2048
HEAD_DIM = 128
SWA_Q_HEADS = 8
SWA_KV_HEADS = 2
SWA_GROUP = SWA_Q_HEADS // SWA_KV_HEADS
SWA_WIDTH = SWA_Q_HEADS * HEAD_DIM
SWA_KV_WIDTH = SWA_KV_HEADS * HEAD_DIM
X_HEADS = 4
X_WIDTH = X_HEADS * HEAD_DIM
CONV_WIDTH = D_MODEL - SWA_WIDTH - X_WIDTH
CONV_K = 3
WINDOW = 128
NUM_BUCKETS = 32
MAX_DISTANCE = WINDOW
MEM_TOKENS = 256
D_FF = 4 * D_MODEL
EPS = 1e-6
NEG = -1e30
SCALE = HEAD_DIM ** -0.5
LOG2E = math.log2(math.e)
SCALE_LOG2 = SCALE * LOG2E

OFF_B = 0
OFF_C = CONV_WIDTH
OFF_H = 2 * CONV_WIDTH
OFF_Q = 3 * CONV_WIDTH
OFF_K = OFF_Q + SWA_WIDTH
OFF_V = OFF_K + SWA_KV_WIDTH
OFF_QX = OFF_V + SWA_KV_WIDTH
IN_WIDTH = OFF_QX + X_WIDTH

VMEM_LIMIT_V7X = 56 * 1024 * 1024
SUBLANES = 8

ROW_TILE = 512
ROW_CHUNK = 256
MLP_ROW_TILE = 512
MLP_FF_TILE = 2048
MLP_FF_CHUNK = 1024
VMEM_LIMIT_MLP_V7X = 62 * 1024 * 1024
MLP_CAST_FF_TILE = 512
SAMPLE_BATCH_TILE = 8
MIX_BLOCKS = 4

BF16 = jnp.bfloat16
F32 = jnp.float32
NT_DIMS = (((1,), (1,)), ((), ()))


def _params(*sem, vmem_limit=VMEM_LIMIT_V7X):
    return pltpu.CompilerParams(dimension_semantics=sem, vmem_limit_bytes=vmem_limit)


def _resident(shape):
    nd = len(shape)
    return pl.BlockSpec(shape, lambda *_: (0,) * nd, pipeline_mode=pl.Buffered(1))


def _rms(x, g):
    ms = jnp.mean(x * x, axis=-1, keepdims=True)
    return x * lax.rsqrt(ms + EPS) * g


def _rel_bucket_np(dist):
    n = np.maximum(dist, 0)
    max_exact = NUM_BUCKETS // 2
    nf = np.maximum(n, 1).astype(np.float32)
    large = max_exact + (np.log(nf / np.float32(max_exact)) / np.float32(math.log(MAX_DISTANCE / max_exact))
                         * np.float32(NUM_BUCKETS - max_exact)).astype(np.int32)
    large = np.minimum(large, NUM_BUCKETS - 1)
    return np.where(n < max_exact, n, large).astype(np.int32)


SAMPLE_KEY_ROWS = 3 * WINDOW


def _sample_key_positions(t_len):
    c = np.arange(SAMPLE_KEY_ROWS)
    n_cache = SWA_KV_HEADS * WINDOW
    assert n_cache + SWA_KV_HEADS * t_len <= SAMPLE_KEY_ROWS
    return np.where(c < n_cache, c // SWA_KV_HEADS, WINDOW + (c - n_cache) // SWA_KV_HEADS)


def _bias_kernel(tab_ref, bp_ref, bs_ref, op_ref, os_ref):
    bp = bp_ref[...]
    bs = bs_ref[...]
    for hh in range(SWA_Q_HEADS):
        accp = jnp.zeros(bp.shape, F32)
        accs = jnp.zeros(bs.shape, F32)
        for k in range(NUM_BUCKETS):
            t = tab_ref[k * SWA_Q_HEADS + hh] * LOG2E
            accp = jnp.where(bp == k, t, accp)
            accs = jnp.where(bs == k, t, accs)
        h, g = divmod(hh, SWA_GROUP)
        op_ref[h, g * WINDOW:(g + 1) * WINDOW, :] = accp
        os_ref[hh * SUBLANES:(hh + 1) * SUBLANES, :] = accs


def _bias_tables(table, t_len):
    qi = np.arange(WINDOW)[:, None]
    kj = np.arange(2 * WINDOW)[None, :]
    bkt_p = _rel_bucket_np(WINDOW + qi - kj)
    key_pos = _sample_key_positions(t_len)
    bkt_s = _rel_bucket_np(np.arange(t_len)[:, None] + WINDOW - key_pos[None, :])
    return pl.pallas_call(
        _bias_kernel,
        out_shape=(jax.ShapeDtypeStruct((SWA_KV_HEADS, SWA_GROUP * WINDOW, 2 * WINDOW), F32),
                   jax.ShapeDtypeStruct((SWA_Q_HEADS * t_len, SAMPLE_KEY_ROWS), F32)),
        in_specs=[pl.BlockSpec(memory_space=pltpu.SMEM),
                  pl.BlockSpec(memory_space=pltpu.VMEM),
                  pl.BlockSpec(memory_space=pltpu.VMEM)],
        out_specs=(pl.BlockSpec(memory_space=pltpu.VMEM), pl.BlockSpec(memory_space=pltpu.VMEM)),
        name="bias",
    )(table.reshape(-1), jnp.asarray(bkt_p), jnp.asarray(bkt_s))


def _round_specs(weights, steps):
    specs = [pl.BlockSpec((w.shape[0] // steps, w.shape[1]), lambda i: (i, 0)) for w in weights]
    shapes = [jax.ShapeDtypeStruct(w.shape, BF16) for w in weights]
    return specs, shapes


def _round_slabs(srcs, dsts):
    for src, dst in zip(srcs, dsts):
        dst[...] = src[...].astype(BF16)


def _memkv_kernel(n_round, m_ref, g_ref, wk_ref, wv_ref, gk_ref, *refs):
    mk_ref, mv_ref, mkb_ref, mvb_ref = refs[n_round:n_round + 4]
    _round_slabs(refs[:n_round], refs[n_round + 4:])
    h = _rms(m_ref[...], g_ref[...]).astype(BF16)
    zk = jnp.dot(h, wk_ref[...].astype(BF16), preferred_element_type=F32)
    zv = jnp.dot(h, wv_ref[...].astype(BF16), preferred_element_type=F32)
    gk = gk_ref[...]
    tokens = m_ref.shape[0]
    for hx in range(X_HEADS):
        sl = slice(hx * HEAD_DIM, (hx + 1) * HEAD_DIM)
        head_rows = pl.ds(hx, tokens, stride=X_HEADS)
        mk = _rms(zk[:, sl], gk)
        mk_ref[head_rows, :] = mk
        mv_ref[head_rows, :] = zv[:, sl]
        mkb_ref[:, sl] = mk.astype(BF16)
    mvb_ref[...] = zv.astype(BF16)


def _memory_kv(mem2d, g_mem, wk, wv, g_k_x, round_weights=()):
    n = mem2d.shape[0]
    tile = MEM_TOKENS
    row = lambda i: (i, 0)
    head_rows = pl.BlockSpec((tile * X_HEADS, HEAD_DIM), row)
    round_specs, round_shapes = _round_specs(round_weights, n // tile)
    return pl.pallas_call(
        functools.partial(_memkv_kernel, len(round_weights)),
        grid=(n // tile,),
        in_specs=[pl.BlockSpec((tile, D_MODEL), row), _resident((1, D_MODEL)),
                  _resident((D_MODEL, X_WIDTH)), _resident((D_MODEL, X_WIDTH)), _resident((1, HEAD_DIM)),
                  *round_specs],
        out_specs=(head_rows, head_rows, pl.BlockSpec((tile, X_WIDTH), row), pl.BlockSpec((tile, X_WIDTH), row),
                   *round_specs),
        out_shape=(jax.ShapeDtypeStruct((n * X_HEADS, HEAD_DIM), F32),
                   jax.ShapeDtypeStruct((n * X_HEADS, HEAD_DIM), F32),
                   jax.ShapeDtypeStruct((n, X_WIDTH), BF16), jax.ShapeDtypeStruct((n, X_WIDTH), BF16),
                   *round_shapes),
        compiler_params=_params("arbitrary"),
        name="memkv",
    )(mem2d, g_mem, wk, wv, g_k_x, *round_weights)


def _proj_kernel(n_round, conv_tiles_per_seq, x_ref, g_ref, w_ref, gq_ref, gk_ref, gx_ref, *refs):
    if conv_tiles_per_seq is not None:
        cw_ref, *refs, carry_ref = refs
    b_ref, u_ref, q_ref, k_ref, v_ref, kb_ref, vb_ref, qx_ref = refs[n_round:n_round + 8]
    _round_slabs(refs[:n_round], refs[n_round + 8:])
    u_before = None
    for r0 in range(0, x_ref.shape[0], ROW_CHUNK):
        r1 = r0 + ROW_CHUNK
        rows = slice(r0, r1)
        h = _rms(x_ref[rows, :], g_ref[...]).astype(BF16)

        def seg(lo, width):
            return jnp.dot(h, w_ref[:, lo:lo + width], preferred_element_type=F32)

        zb = seg(OFF_B, CONV_WIDTH)
        u = seg(OFF_C, CONV_WIDTH) * seg(OFF_H, CONV_WIDTH)
        u_ref[rows, :] = u
        if conv_tiles_per_seq is None:
            b_ref[rows, :] = zb
        else:
            if u_before is None:
                prev = jnp.where(pl.program_id(0) % conv_tiles_per_seq > 0, carry_ref[...], 0.0)
            else:
                prev = u_before[ROW_CHUNK - SUBLANES:]
            ext = jnp.concatenate([prev, u], axis=0)
            cw = cw_ref[...]
            conv = cw[0:1] * ext[SUBLANES - 2:SUBLANES - 2 + ROW_CHUNK]
            conv = conv + cw[1:2] * ext[SUBLANES - 1:SUBLANES - 1 + ROW_CHUNK]
            conv = conv + cw[2:3] * u
            b_ref[rows, :] = (zb * conv).astype(b_ref.dtype)
            u_before = u

        def head_norm(z, g, n_heads, out):
            for hh in range(n_heads):
                sl = slice(hh * HEAD_DIM, (hh + 1) * HEAD_DIM)
                out[rows, sl] = _rms(z[:, sl], g).astype(out.dtype)

        head_norm(seg(OFF_Q, SWA_WIDTH), gq_ref[...], SWA_Q_HEADS, q_ref)
        head_norm(seg(OFF_QX, X_WIDTH), gx_ref[...], X_HEADS, qx_ref)
        zk = seg(OFF_K, SWA_KV_WIDTH)
        zv = seg(OFF_V, SWA_KV_WIDTH)
        vb_ref[rows, :] = zv.astype(BF16)
        for hh in range(SWA_KV_HEADS):
            sl = slice(hh * HEAD_DIM, (hh + 1) * HEAD_DIM)
            head_rows = pl.ds(r0 * SWA_KV_HEADS + hh, r1 - r0, stride=SWA_KV_HEADS)
            k = _rms(zk[:, sl], gk_ref[...])
            k_ref[head_rows, :] = k
            v_ref[head_rows, :] = zv[:, sl]
            kb_ref[rows, sl] = k.astype(BF16)
    if conv_tiles_per_seq is not None:
        carry_ref[...] = u_before[ROW_CHUNK - SUBLANES:]


def _projections(x2d, g_mix, w_in, g_q, g_k, g_qx, q_dtype, round_weights=(), conv=None):
    n = x2d.shape[0]
    tile = min(ROW_TILE, n)
    row = lambda i: (i, 0)
    outs = ((1, CONV_WIDTH, F32 if conv is None else BF16), (1, CONV_WIDTH, F32), (1, SWA_WIDTH, q_dtype),
            (SWA_KV_HEADS, HEAD_DIM, F32), (SWA_KV_HEADS, HEAD_DIM, F32),
            (1, SWA_KV_WIDTH, BF16), (1, SWA_KV_WIDTH, BF16), (1, X_WIDTH, q_dtype))
    round_specs, round_shapes = _round_specs(round_weights, n // tile)
    conv_args, conv_specs, conv_scratch, tiles_per_seq = (), (), (), None
    if conv is not None:
        conv_w, seq = conv
        conv_args, conv_specs = (conv_w,), (_resident((CONV_K, CONV_WIDTH)),)
        conv_scratch, tiles_per_seq = (pltpu.VMEM((SUBLANES, CONV_WIDTH), F32),), seq // tile
    return pl.pallas_call(
        functools.partial(_proj_kernel, len(round_weights), tiles_per_seq),
        grid=(n // tile,),
        in_specs=[pl.BlockSpec((tile, D_MODEL), row), _resident((1, D_MODEL)), _resident((D_MODEL, IN_WIDTH)),
                  _resident((1, HEAD_DIM)), _resident((1, HEAD_DIM)), _resident((1, HEAD_DIM)),
                  *conv_specs, *round_specs],
        out_specs=(*(pl.BlockSpec((tile * r, w), row) for r, w, _ in outs), *round_specs),
        out_shape=(*(jax.ShapeDtypeStruct((n * r, w), dt) for r, w, dt in outs), *round_shapes),
        scratch_shapes=conv_scratch,
        compiler_params=_params("arbitrary"),
        name="proj",
    )(x2d, g_mix, w_in, g_q, g_k, g_qx, *conv_args, *round_weights)


def _softmax_parts(s, sink=None):
    m = jnp.max(s, axis=-1, keepdims=True)
    if sink is not None:
        m = jnp.maximum(m, sink)
    p = jnp.exp2(s - m)
    den = jnp.sum(p, axis=-1, keepdims=True)
    if sink is not None:
        den = den + jnp.exp2(sink - m)
    return p, 1.0 / den


def _mix_out_kernel(sink_ref, q_ref, kc_ref, kp_ref, vc_ref, vp_ref, qx_ref, mk_ref, mv_ref,
                    yc_ref, bias_ref, band_ref, x_ref, w_ref, g_ref, x1_ref, h_ref):
    has_prev = pl.program_id(1) > 0
    blocks = [slice(j * WINDOW, (j + 1) * WINDOW) for j in range(MIX_BLOCKS)]

    def project(y_cols, col0, acc):
        y = jnp.concatenate(y_cols, axis=1).astype(BF16)
        return acc + jnp.dot(y, w_ref[col0:col0 + y.shape[1], :], preferred_element_type=F32)

    x1 = project([yc_ref[...]], 0, x_ref[...])

    for h in range(SWA_KV_HEADS):
        ksl = slice(h * HEAD_DIM, (h + 1) * HEAD_DIM)
        outs = [[None] * MIX_BLOCKS for _ in range(SWA_GROUP)]
        for j, rsl in enumerate(blocks):
            min_band = jnp.where(has_prev, 0.5, 1.5) if j == 0 else 0.5
            if j == 0:
                k_all = jnp.concatenate([kp_ref[:, ksl], kc_ref[rsl, ksl]], axis=0)
                v_all = jnp.concatenate([vp_ref[:, ksl], vc_ref[rsl, ksl]], axis=0)
            else:
                k_all = kc_ref[(j - 1) * WINDOW:(j + 1) * WINDOW, ksl]
                v_all = vc_ref[(j - 1) * WINDOW:(j + 1) * WINDOW, ksl]
            for g in range(SWA_GROUP):
                hh = h * SWA_GROUP + g
                q = q_ref[rsl, hh * HEAD_DIM:(hh + 1) * HEAD_DIM]
                s = lax.dot_general(q, k_all, NT_DIMS, preferred_element_type=F32) * SCALE_LOG2
                s = jnp.where(band_ref[...] > min_band, s + bias_ref[h, g * WINDOW:(g + 1) * WINDOW, :], NEG)
                p, inv = _softmax_parts(s, sink_ref[hh] * LOG2E)
                outs[g][j] = (jnp.dot(p.astype(BF16), v_all, preferred_element_type=F32) * inv).astype(BF16)
        x1 = project([jnp.concatenate(o, axis=0) for o in outs], CONV_WIDTH + h * SWA_GROUP * HEAD_DIM, x1)

    outs = [[None] * MIX_BLOCKS for _ in range(X_HEADS)]
    for j, rsl in enumerate(blocks):
        for hx in range(X_HEADS):
            sl = slice(hx * HEAD_DIM, (hx + 1) * HEAD_DIM)
            s = lax.dot_general(qx_ref[rsl, sl], mk_ref[:, sl], NT_DIMS, preferred_element_type=F32) * SCALE_LOG2
            p, inv = _softmax_parts(s)
            outs[hx][j] = (jnp.dot(p.astype(BF16), mv_ref[:, sl], preferred_element_type=F32) * inv).astype(BF16)
    x1 = project([jnp.concatenate(o, axis=0) for o in outs], CONV_WIDTH + SWA_WIDTH, x1)

    x1_ref[...] = x1
    h_ref[...] = _rms(x1, g_ref[...]).astype(BF16)


def _mix_out_prompt(sinks, q, kb, vb, qx, mkb, mvb, y_conv, bias_p, x2d, w_out, g_mlp, bsz, seq):
    rows = MIX_BLOCKS * WINDOW
    steps = seq // rows
    cur = lambda bi, i: (bi * steps + i, 0)
    prv = lambda bi, i: (jnp.maximum((bi * steps + i) * MIX_BLOCKS - 1, 0), 0)
    per_b = lambda bi, i: (bi, 0)
    dist = WINDOW + np.arange(WINDOW)[:, None] - np.arange(2 * WINDOW)[None, :]
    band = np.where((dist >= 0) & (dist < WINDOW), np.where(np.arange(2 * WINDOW)[None, :] < WINDOW, 1.0, 2.0), 0.0)
    return pl.pallas_call(
        _mix_out_kernel,
        grid=(bsz, steps),
        in_specs=[pl.BlockSpec(memory_space=pltpu.SMEM),
                  pl.BlockSpec((rows, SWA_WIDTH), cur),
                  pl.BlockSpec((rows, SWA_KV_WIDTH), cur), pl.BlockSpec((WINDOW, SWA_KV_WIDTH), prv),
                  pl.BlockSpec((rows, SWA_KV_WIDTH), cur), pl.BlockSpec((WINDOW, SWA_KV_WIDTH), prv),
                  pl.BlockSpec((rows, X_WIDTH), cur),
                  pl.BlockSpec((MEM_TOKENS, X_WIDTH), per_b), pl.BlockSpec((MEM_TOKENS, X_WIDTH), per_b),
                  pl.BlockSpec((rows, CONV_WIDTH), cur),
                  _resident((SWA_KV_HEADS, SWA_GROUP * WINDOW, 2 * WINDOW)),
                  _resident((WINDOW, 2 * WINDOW)),
                  pl.BlockSpec((rows, D_MODEL), cur),
                  _resident((D_MODEL, D_MODEL)), _resident((1, D_MODEL))],
        out_specs=(pl.BlockSpec((rows, D_MODEL), cur), pl.BlockSpec((rows, D_MODEL), cur)),
        out_shape=(jax.ShapeDtypeStruct((bsz * seq, D_MODEL), F32), jax.ShapeDtypeStruct((bsz * seq, D_MODEL), BF16)),
        compiler_params=_params("arbitrary", "arbitrary"),
        name="mix_out",
    )(sinks, q, kb, kb, vb, vb, qx, mkb, mvb, y_conv, bias_p, jnp.asarray(band, F32), x2d, w_out, g_mlp)


def _mix_sample_kernel(t_len, sink_ref, q_ref, kn_ref, vn_ref, ck_ref, cv_ref, qx_ref, cmk_ref, cmv_ref,
                       b_ref, u_ref, cc_ref, cw_ref, bias_ref, x_ref, w_ref, g_ref,
                       x1_ref, h_ref, sk_ref, sv_ref, o_ref):
    nb = cc_ref.shape[0]
    n_cache = SWA_KV_HEADS * WINDOW
    n_new = SWA_KV_HEADS * t_len
    n_mem = X_HEADS * MEM_TOKENS
    log_t = int(math.log2(t_len))

    rows = SWA_Q_HEADS * t_len
    r = lax.broadcasted_iota(jnp.int32, (nb * rows, SAMPLE_KEY_ROWS), 0) & (rows - 1)
    c = lax.broadcasted_iota(jnp.int32, (nb * rows, SAMPLE_KEY_ROWS), 1)
    key_pos = jnp.where(c < n_cache, c >> 1, WINDOW + ((c - n_cache) >> 1))
    dist = (r & (t_len - 1)) + WINDOW - key_pos
    valid = (dist >= 0) & (dist < WINDOW) & ((c & (SWA_KV_HEADS - 1)) == (r >> int(math.log2(SWA_GROUP * t_len))))
    bias = jnp.concatenate([bias_ref[...]] * nb, axis=0)
    sink_col = jnp.concatenate([jnp.full((t_len, 1), sink_ref[hh], F32) for hh in range(SWA_Q_HEADS)] * nb, axis=0)
    xrows = X_HEADS * t_len
    xr = lax.broadcasted_iota(jnp.int32, (nb * xrows, n_mem), 0) & (xrows - 1)
    xc = lax.broadcasted_iota(jnp.int32, (nb * xrows, n_mem), 1)
    x_valid = (xc & (X_HEADS - 1)) == (xr >> log_t)
    cw = cw_ref[...]
    trow = lax.broadcasted_iota(jnp.int32, (t_len, CONV_WIDTH), 0)
    zeros_pad = jnp.zeros((SAMPLE_KEY_ROWS - n_cache - n_new, HEAD_DIM), F32)

    s_list, sx_list = [], []
    for bi in range(nb):
        rsl = slice(bi * t_len, (bi + 1) * t_len)
        c0, n0, m0 = bi * n_cache, bi * n_new, bi * n_mem

        for dst, cache, new in ((sk_ref, ck_ref, kn_ref), (sv_ref, cv_ref, vn_ref)):
            dst[c0:c0 + n_cache - n_new, :] = cache[c0 + n_new:c0 + n_cache, :]
            dst[c0 + n_cache - n_new:c0 + n_cache, :] = new[n0:n0 + n_new, :]

        u = u_ref[rsl, :]
        cc = cc_ref[bi]
        cc1 = jnp.broadcast_to(cc[1:2], u.shape)
        cc0 = jnp.broadcast_to(cc[0:1], u.shape)
        u_m1 = jnp.where(trow >= 1, pltpu.roll(u, 1, 0), cc1)
        u_m2 = jnp.where(trow >= 2, pltpu.roll(u, 2, 0), jnp.where(trow == 1, cc1, cc0))
        conv = cw[0:1] * u_m2
        conv = conv + cw[1:2] * u_m1
        conv = conv + cw[2:3] * u
        o_ref[rsl, 0:CONV_WIDTH] = (b_ref[rsl, :] * conv).astype(o_ref.dtype)

        qb = q_ref[rsl, :]
        q_rows = jnp.concatenate([qb[:, hh * HEAD_DIM:(hh + 1) * HEAD_DIM] for hh in range(SWA_Q_HEADS)], axis=0)
        k_all = jnp.concatenate([ck_ref[c0:c0 + n_cache, :], kn_ref[n0:n0 + n_new, :], zeros_pad], axis=0)
        s_list.append(lax.dot_general(q_rows.astype(BF16), k_all.astype(BF16), NT_DIMS, preferred_element_type=F32))
        qxb = qx_ref[rsl, :]
        qx_rows = jnp.concatenate([qxb[:, hx * HEAD_DIM:(hx + 1) * HEAD_DIM] for hx in range(X_HEADS)], axis=0)
        mk = cmk_ref[m0:m0 + n_mem, :].astype(BF16)
        sx_list.append(lax.dot_general(qx_rows.astype(BF16), mk, NT_DIMS, preferred_element_type=F32))

    s = jnp.concatenate(s_list, axis=0) * SCALE_LOG2
    w, inv = _softmax_parts(jnp.where(valid, s + bias, NEG), sink_col * LOG2E)
    w = w.astype(BF16)
    sx = jnp.concatenate(sx_list, axis=0) * SCALE_LOG2
    wx, invx = _softmax_parts(jnp.where(x_valid, sx, NEG))
    wx = wx.astype(BF16)

    for bi in range(nb):
        rsl = slice(bi * t_len, (bi + 1) * t_len)
        c0, n0, m0 = bi * n_cache, bi * n_new, bi * n_mem
        v_all = jnp.concatenate([cv_ref[c0:c0 + n_cache, :], vn_ref[n0:n0 + n_new, :], zeros_pad], axis=0)
        srows = slice(bi * rows, (bi + 1) * rows)
        o = jnp.dot(w[srows], v_all.astype(BF16), preferred_element_type=F32) * inv[srows]
        for hh in range(SWA_Q_HEADS):
            col = CONV_WIDTH + hh * HEAD_DIM
            o_ref[rsl, col:col + HEAD_DIM] = o[hh * t_len:(hh + 1) * t_len].astype(o_ref.dtype)
        mv = cmv_ref[m0:m0 + n_mem, :].astype(BF16)
        xsl = slice(bi * xrows, (bi + 1) * xrows)
        ox = jnp.dot(wx[xsl], mv, preferred_element_type=F32) * invx[xsl]
        for hx in range(X_HEADS):
            col = CONV_WIDTH + SWA_WIDTH + hx * HEAD_DIM
            o_ref[rsl, col:col + HEAD_DIM] = ox[hx * t_len:(hx + 1) * t_len].astype(o_ref.dtype)

    x1 = x_ref[...] + jnp.dot(o_ref[...].astype(BF16), w_ref[...], preferred_element_type=F32)
    x1_ref[...] = x1
    h_ref[...] = _rms(x1, g_ref[...]).astype(BF16)


def _mix_out_sample(sinks, q, k_rows, v_rows, cache_k, cache_v, qx, cache_mk, cache_mv, b, u, cache_conv, conv_w,
                    bias_s, x2d, w_out, g_mlp, bsz, t_len):
    assert t_len == SUBLANES and bsz % SAMPLE_BATCH_TILE == 0 and SWA_KV_HEADS == 2
    nb = SAMPLE_BATCH_TILE
    n_cache, n_new, n_mem = SWA_KV_HEADS * WINDOW, SWA_KV_HEADS * t_len, X_HEADS * MEM_TOKENS
    row = lambda i: (i, 0)
    rows_of = lambda n, w: pl.BlockSpec((nb * n, w), row)
    return pl.pallas_call(
        functools.partial(_mix_sample_kernel, t_len),
        grid=(bsz // nb,),
        in_specs=[pl.BlockSpec(memory_space=pltpu.SMEM),
                  rows_of(t_len, SWA_WIDTH),
                  rows_of(n_new, HEAD_DIM), rows_of(n_new, HEAD_DIM),
                  rows_of(n_cache, HEAD_DIM), rows_of(n_cache, HEAD_DIM),
                  rows_of(t_len, X_WIDTH),
                  rows_of(n_mem, HEAD_DIM), rows_of(n_mem, HEAD_DIM),
                  rows_of(t_len, CONV_WIDTH), rows_of(t_len, CONV_WIDTH),
                  pl.BlockSpec((nb, CONV_K - 1, CONV_WIDTH), lambda i: (i, 0, 0)),
                  _resident((CONV_K, CONV_WIDTH)),
                  _resident((SWA_Q_HEADS * t_len, SAMPLE_KEY_ROWS)),
                  rows_of(t_len, D_MODEL), _resident((D_MODEL, D_MODEL)), _resident((1, D_MODEL))],
        out_specs=(rows_of(t_len, D_MODEL), rows_of(t_len, D_MODEL),
                   rows_of(n_cache, HEAD_DIM), rows_of(n_cache, HEAD_DIM)),
        out_shape=(jax.ShapeDtypeStruct((bsz * t_len, D_MODEL), F32),
                   jax.ShapeDtypeStruct((bsz * t_len, D_MODEL), BF16),
                   jax.ShapeDtypeStruct((bsz * n_cache, HEAD_DIM), F32),
                   jax.ShapeDtypeStruct((bsz * n_cache, HEAD_DIM), F32)),
        scratch_shapes=[pltpu.VMEM((nb * t_len, D_MODEL), F32)],
        compiler_params=_params("arbitrary"),
        name="mix_out_sample",
    )(sinks, q, k_rows, v_rows, cache_k, cache_v, qx, cache_mk, cache_mv, b, u, cache_conv, conv_w, bias_s,
      x2d, w_out, g_mlp)


def _mlp_kernel(x1_ref, h_ref, wu_ref, wd_ref, o_ref):
    @pl.when(pl.program_id(1) == 0)
    def _():
        o_ref[...] = x1_ref[...]

    for c0 in range(0, wu_ref.shape[1], MLP_FF_CHUNK):
        ff = slice(c0, c0 + MLP_FF_CHUNK)
        a = jnp.maximum(jnp.dot(h_ref[...], wu_ref[:, ff], preferred_element_type=F32), 0.0)
        o_ref[...] += jnp.dot((a * a).astype(BF16), wd_ref[ff, :], preferred_element_type=F32)


def _mlp_cast_kernel(x1_hbm, h_ref, wu_ref, wd_ref, o_ref, wub_ref, wdb_ref, sem):
    j = pl.program_id(1)
    residual_copy = pltpu.make_async_copy(x1_hbm, o_ref, sem)

    @pl.when(j == 0)
    def _():
        residual_copy.start()

    wu = wu_ref[...].astype(BF16)
    wub_ref[...] = wu
    a = jnp.maximum(jnp.dot(h_ref[...], wu, preferred_element_type=F32), 0.0)
    a = (a * a).astype(BF16)

    @pl.when(j == 0)
    def _():
        residual_copy.wait()

    for c0 in range(0, D_MODEL, MLP_CAST_FF_TILE):
        cols = slice(c0, c0 + MLP_CAST_FF_TILE)
        wd = wd_ref[:, cols].astype(BF16)
        wdb_ref[:, cols] = wd
        o_ref[:, cols] += jnp.dot(a, wd, preferred_element_type=F32)


def _mlp(x1, h, w_up, w_down):
    n = x1.shape[0]
    tile = min(MLP_ROW_TILE, n)
    rows = pl.BlockSpec((tile, D_MODEL), lambda i, j: (i, 0))
    return pl.pallas_call(
        _mlp_kernel,
        grid=(n // tile, D_FF // MLP_FF_TILE),
        in_specs=[rows, rows,
                  pl.BlockSpec((D_MODEL, MLP_FF_TILE), lambda i, j: (0, j)),
                  pl.BlockSpec((MLP_FF_TILE, D_MODEL), lambda i, j: (j, 0))],
        out_specs=rows,
        out_shape=jax.ShapeDtypeStruct((n, D_MODEL), F32),
        compiler_params=_params("arbitrary", "arbitrary", vmem_limit=VMEM_LIMIT_MLP_V7X),
        name="mlp",
    )(x1, h, w_up, w_down)


def _mlp_cast(x1, h, w_up, w_down):
    n = x1.shape[0]
    rows = pl.BlockSpec((n, D_MODEL), lambda i, j: (0, 0))
    up_spec = pl.BlockSpec((D_MODEL, MLP_CAST_FF_TILE), lambda i, j: (0, j))
    down_spec = pl.BlockSpec((MLP_CAST_FF_TILE, D_MODEL), lambda i, j: (j, 0))
    return pl.pallas_call(
        _mlp_cast_kernel,
        grid=(1, D_FF // MLP_CAST_FF_TILE),
        in_specs=[pl.BlockSpec(memory_space=pl.ANY), rows, up_spec, down_spec],
        out_specs=(rows, up_spec, down_spec),
        out_shape=(jax.ShapeDtypeStruct((n, D_MODEL), F32),
                   jax.ShapeDtypeStruct(w_up.shape, BF16), jax.ShapeDtypeStruct(w_down.shape, BF16)),
        scratch_shapes=[pltpu.SemaphoreType.DMA(())],
        compiler_params=_params("arbitrary", "arbitrary"),
        name="mlp_cast",
    )(x1, h, w_up, w_down)


def kernel(x_prompt, x_sample, mem_prompt, cache_conv, cache_swa_k, cache_swa_v, cache_mem_k, cache_mem_v,
           rel_bias_table, g_mix, w_in, conv_w, g_q_swa, g_k_swa, sinks, g_q_x, g_k_x, g_mem,
           w_mem_k, w_mem_v, w_out, g_mlp, w_up, w_down):
    depth = w_in.shape[0]
    bsz, seq, _ = x_prompt.shape
    dbsz, t_len, _ = x_sample.shape
    xp = x_prompt.reshape(bsz * seq, D_MODEL)
    xs = x_sample.reshape(dbsz * t_len, D_MODEL)
    mem2d = mem_prompt.reshape(bsz * MEM_TOKENS, D_MODEL)
    bias_p, bias_s = _bias_tables(rel_bias_table, t_len)

    outs = [[] for _ in range(8)]
    for l in range(depth):
        vec = lambda a: a[l].reshape(1, -1)
        mk, mv, mkb, mvb, wi = _memory_kv(mem2d, vec(g_mem), w_mem_k[l], w_mem_v[l], vec(g_k_x),
                                          round_weights=(w_in[l],))
        proj = functools.partial(_projections, g_mix=vec(g_mix), w_in=wi, g_q=vec(g_q_swa), g_k=vec(g_k_swa),
                                 g_qx=vec(g_q_x))
        pyc, pu, pq, pk, pv, pkb, pvb, pqx, wo = proj(xp, q_dtype=BF16, round_weights=(w_out[l],),
                                                      conv=(conv_w[l], seq))

        head_rows = lambda a: a.reshape(-1, HEAD_DIM)
        b, u, q, k, v, _, _, qx = proj(xs, q_dtype=F32)
        x1, h, sk, sv = _mix_out_sample(
            sinks[l], q, k, v, head_rows(cache_swa_k[l]), head_rows(cache_swa_v[l]),
            qx, head_rows(cache_mem_k[l]), head_rows(cache_mem_v[l]),
            b, u, cache_conv[l], conv_w[l], bias_s, xs, wo, vec(g_mlp), dbsz, t_len)
        xs, wu, wd = _mlp_cast(x1, h, w_up[l], w_down[l])
        outs[5].append(u.reshape(dbsz, t_len, CONV_WIDTH)[:, t_len - (CONV_K - 1):])
        outs[6].append(sk.reshape(dbsz, WINDOW, SWA_KV_HEADS, HEAD_DIM))
        outs[7].append(sv.reshape(dbsz, WINDOW, SWA_KV_HEADS, HEAD_DIM))

        x1, h = _mix_out_prompt(sinks[l], pq, pkb, pvb, pqx, mkb, mvb, pyc, bias_p, xp, wo, vec(g_mlp), bsz, seq)
        xp = _mlp(x1, h, wu, wd)
        outs[0].append(pu.reshape(bsz, seq, CONV_WIDTH)[:, seq - (CONV_K - 1):])
        last_window = lambda a: a.reshape(bsz, seq, SWA_KV_HEADS, HEAD_DIM)[:, seq - WINDOW:]
        outs[1].append(last_window(pk))
        outs[2].append(last_window(pv))
        outs[3].append(mk.reshape(bsz, MEM_TOKENS, X_HEADS, HEAD_DIM))
        outs[4].append(mv.reshape(bsz, MEM_TOKENS, X_HEADS, HEAD_DIM))

    return (xp.reshape(bsz, seq, D_MODEL), xs.reshape(dbsz, t_len, D_MODEL)) + tuple(jnp.stack(o) for o in outs)
```

```python
import functools
import math

import numpy as np
import jax
import jax.numpy as jnp
from jax import lax
from jax.experimental import pallas as pl
from jax.experimental.pallas import tpu as pltpu

D_MODEL = 2048
HEAD_DIM = 128
SWA_Q_HEADS = 8
SWA_KV_HEADS = 2
SWA_GROUP = SWA_Q_HEADS // SWA_KV_HEADS
SWA_WIDTH = SWA_Q_HEADS * HEAD_DIM
SWA_KV_WIDTH = SWA_KV_HEADS * HEAD_DIM
X_HEADS = 4
X_WIDTH = X_HEADS * HEAD_DIM
CONV_WIDTH = D_MODEL - SWA_WIDTH - X_WIDTH
CONV_K = 3
WINDOW = 128
NUM_BUCKETS = 32
MAX_DISTANCE = WINDOW
MEM_TOKENS = 256
D_FF = 4 * D_MODEL
EPS = 1e-6
NEG = -1e30
SCALE = HEAD_DIM ** -0.5
LOG2E = math.log2(math.e)
SCALE_LOG2 = SCALE * LOG2E

OFF_B = 0
OFF_C = CONV_WIDTH
OFF_H = 2 * CONV_WIDTH
OFF_Q = 3 * CONV_WIDTH
OFF_K = OFF_Q + SWA_WIDTH
OFF_V = OFF_K + SWA_KV_WIDTH
OFF_QX = OFF_V + SWA_KV_WIDTH
IN_WIDTH = OFF_QX + X_WIDTH

VMEM_LIMIT_V7X = 56 * 1024 * 1024
SUBLANES = 8

ROW_TILE = 512
ROW_CHUNK = 256
MLP_ROW_TILE = 512
MLP_FF_TILE = 2048
MLP_FF_CHUNK = 1024
VMEM_LIMIT_MLP_V7X = 62 * 1024 * 1024
MLP_CAST_FF_TILE = 512
SAMPLE_BATCH_TILE = 8
MIX_BLOCKS = 4

BF16 = jnp.bfloat16
F32 = jnp.float32
NT_DIMS = (((1,), (1,)), ((), ()))


def _params(*sem, vmem_limit=VMEM_LIMIT_V7X):
    return pltpu.CompilerParams(dimension_semantics=sem, vmem_limit_bytes=vmem_limit)


def _resident(shape):
    nd = len(shape)
    return pl.BlockSpec(shape, lambda *_: (0,) * nd, pipeline_mode=pl.Buffered(1))


def _rms(x, g):
    ms = jnp.mean(x * x, axis=-1, keepdims=True)
    return x * lax.rsqrt(ms + EPS) * g


def _rel_bucket_np(dist):
    n = np.maximum(dist, 0)
    max_exact = NUM_BUCKETS // 2
    nf = np.maximum(n, 1).astype(np.float32)
    large = max_exact + (np.log(nf / np.float32(max_exact)) / np.float32(math.log(MAX_DISTANCE / max_exact))
                         * np.float32(NUM_BUCKETS - max_exact)).astype(np.int32)
    large = np.minimum(large, NUM_BUCKETS - 1)
    return np.where(n < max_exact, n, large).astype(np.int32)


SAMPLE_KEY_ROWS = 3 * WINDOW


def _sample_key_positions(t_len):
    c = np.arange(SAMPLE_KEY_ROWS)
    n_cache = SWA_KV_HEADS * WINDOW
    assert n_cache + SWA_KV_HEADS * t_len <= SAMPLE_KEY_ROWS
    return np.where(c < n_cache, c // SWA_KV_HEADS, WINDOW + (c - n_cache) // SWA_KV_HEADS)


def _bias_kernel(tab_ref, bp_ref, bs_ref, op_ref, os_ref):
    bp = bp_ref[...]
    bs = bs_ref[...]
    for hh in range(SWA_Q_HEADS):
        accp = jnp.zeros(bp.shape, F32)
        accs = jnp.zeros(bs.shape, F32)
        for k in range(NUM_BUCKETS):
            t = tab_ref[k * SWA_Q_HEADS + hh] * LOG2E
            accp = jnp.where(bp == k, t, accp)
            accs = jnp.where(bs == k, t, accs)
        h, g = divmod(hh, SWA_GROUP)
        op_ref[h, g * WINDOW:(g + 1) * WINDOW, :] = accp
        os_ref[hh * SUBLANES:(hh + 1) * SUBLANES, :] = accs


def _bias_tables(table, t_len):
    qi = np.arange(WINDOW)[:, None]
    kj = np.arange(2 * WINDOW)[None, :]
    bkt_p = _rel_bucket_np(WINDOW + qi - kj)
    key_pos = _sample_key_positions(t_len)
    bkt_s = _rel_bucket_np(np.arange(t_len)[:, None] + WINDOW - key_pos[None, :])
    return pl.pallas_call(
        _bias_kernel,
        out_shape=(jax.ShapeDtypeStruct((SWA_KV_HEADS, SWA_GROUP * WINDOW, 2 * WINDOW), F32),
                   jax.ShapeDtypeStruct((SWA_Q_HEADS * t_len, SAMPLE_KEY_ROWS), F32)),
        in_specs=[pl.BlockSpec(memory_space=pltpu.SMEM),
                  pl.BlockSpec(memory_space=pltpu.VMEM),
                  pl.BlockSpec(memory_space=pltpu.VMEM)],
        out_specs=(pl.BlockSpec(memory_space=pltpu.VMEM), pl.BlockSpec(memory_space=pltpu.VMEM)),
        name="bias",
    )(table.reshape(-1), jnp.asarray(bkt_p), jnp.asarray(bkt_s))


def _round_specs(weights, steps):
    specs = [pl.BlockSpec((w.shape[0] // steps, w.shape[1]), lambda i: (i, 0)) for w in weights]
    shapes = [jax.ShapeDtypeStruct(w.shape, BF16) for w in weights]
    return specs, shapes


def _round_slabs(srcs, dsts):
    for src, dst in zip(srcs, dsts):
        dst[...] = src[...].astype(BF16)


def _memkv_kernel(n_round, m_ref, g_ref, wk_ref, wv_ref, gk_ref, *refs):
    mk_ref, mv_ref, mkb_ref, mvb_ref = refs[n_round:n_round + 4]
    _round_slabs(refs[:n_round], refs[n_round + 4:])
    h = _rms(m_ref[...], g_ref[...]).astype(BF16)
    zk = jnp.dot(h, wk_ref[...].astype(BF16), preferred_element_type=F32)
    zv = jnp.dot(h, wv_ref[...].astype(BF16), preferred_element_type=F32)
    gk = gk_ref[...]
    tokens = m_ref.shape[0]
    for hx in range(X_HEADS):
        sl = slice(hx * HEAD_DIM, (hx + 1) * HEAD_DIM)
        head_rows = pl.ds(hx, tokens, stride=X_HEADS)
        mk = _rms(zk[:, sl], gk)
        mk_ref[head_rows, :] = mk
        mv_ref[head_rows, :] = zv[:, sl]
        mkb_ref[:, sl] = mk.astype(BF16)
    mvb_ref[...] = zv.astype(BF16)


def _memory_kv(mem2d, g_mem, wk, wv, g_k_x, round_weights=()):
    n = mem2d.shape[0]
    tile = MEM_TOKENS
    row = lambda i: (i, 0)
    head_rows = pl.BlockSpec((tile * X_HEADS, HEAD_DIM), row)
    round_specs, round_shapes = _round_specs(round_weights, n // tile)
    return pl.pallas_call(
        functools.partial(_memkv_kernel, len(round_weights)),
        grid=(n // tile,),
        in_specs=[pl.BlockSpec((tile, D_MODEL), row), _resident((1, D_MODEL)),
                  _resident((D_MODEL, X_WIDTH)), _resident((D_MODEL, X_WIDTH)), _resident((1, HEAD_DIM)),
                  *round_specs],
        out_specs=(head_rows, head_rows, pl.BlockSpec((tile, X_WIDTH), row), pl.BlockSpec((tile, X_WIDTH), row),
                   *round_specs),
        out_shape=(jax.ShapeDtypeStruct((n * X_HEADS, HEAD_DIM), F32),
                   jax.ShapeDtypeStruct((n * X_HEADS, HEAD_DIM), F32),
                   jax.ShapeDtypeStruct((n, X_WIDTH), BF16), jax.ShapeDtypeStruct((n, X_WIDTH), BF16),
                   *round_shapes),
        compiler_params=_params("arbitrary"),
        name="memkv",
    )(mem2d, g_mem, wk, wv, g_k_x, *round_weights)


def _proj_kernel(n_round, x_ref, g_ref, w_ref, gq_ref, gk_ref, gx_ref, *refs):
    b_ref, u_ref, q_ref, k_ref, v_ref, kb_ref, vb_ref, qx_ref = refs[n_round:n_round + 8]
    _round_slabs(refs[:n_round], refs[n_round + 8:])
    for r0 in range(0, x_ref.shape[0], ROW_CHUNK):
        r1 = r0 + ROW_CHUNK
        rows = slice(r0, r1)
        h = _rms(x_ref[rows, :], g_ref[...]).astype(BF16)

        def seg(lo, width):
            return jnp.dot(h, w_ref[:, lo:lo + width], preferred_element_type=F32)

        b_ref[rows, :] = seg(OFF_B, CONV_WIDTH)
        u_ref[rows, :] = seg(OFF_C, CONV_WIDTH) * seg(OFF_H, CONV_WIDTH)

        def head_norm(z, g, n_heads, out):
            for hh in range(n_heads):
                sl = slice(hh * HEAD_DIM, (hh + 1) * HEAD_DIM)
                out[rows, sl] = _rms(z[:, sl], g).astype(out.dtype)

        head_norm(seg(OFF_Q, SWA_WIDTH), gq_ref[...], SWA_Q_HEADS, q_ref)
        head_norm(seg(OFF_QX, X_WIDTH), gx_ref[...], X_HEADS, qx_ref)
        zk = seg(OFF_K, SWA_KV_WIDTH)
        zv = seg(OFF_V, SWA_KV_WIDTH)
        vb_ref[rows, :] = zv.astype(BF16)
        for hh in range(SWA_KV_HEADS):
            sl = slice(hh * HEAD_DIM, (hh + 1) * HEAD_DIM)
            head_rows = pl.ds(r0 * SWA_KV_HEADS + hh, r1 - r0, stride=SWA_KV_HEADS)
            k = _rms(zk[:, sl], gk_ref[...])
            k_ref[head_rows, :] = k
            v_ref[head_rows, :] = zv[:, sl]
            kb_ref[rows, sl] = k.astype(BF16)


def _projections(x2d, g_mix, w_in, g_q, g_k, g_qx, q_dtype, round_weights=()):
    n = x2d.shape[0]
    tile = min(ROW_TILE, n)
    row = lambda i: (i, 0)
    outs = ((1, CONV_WIDTH, F32), (1, CONV_WIDTH, F32), (1, SWA_WIDTH, q_dtype),
            (SWA_KV_HEADS, HEAD_DIM, F32), (SWA_KV_HEADS, HEAD_DIM, F32),
            (1, SWA_KV_WIDTH, BF16), (1, SWA_KV_WIDTH, BF16), (1, X_WIDTH, q_dtype))
    round_specs, round_shapes = _round_specs(round_weights, n // tile)
    return pl.pallas_call(
        functools.partial(_proj_kernel, len(round_weights)),
        grid=(n // tile,),
        in_specs=[pl.BlockSpec((tile, D_MODEL), row), _resident((1, D_MODEL)), _resident((D_MODEL, IN_WIDTH)),
                  _resident((1, HEAD_DIM)), _resident((1, HEAD_DIM)), _resident((1, HEAD_DIM)), *round_specs],
        out_specs=(*(pl.BlockSpec((tile * r, w), row) for r, w, _ in outs), *round_specs),
        out_shape=(*(jax.ShapeDtypeStruct((n * r, w), dt) for r, w, dt in outs), *round_shapes),
        compiler_params=_params("arbitrary"),
        name="proj",
    )(x2d, g_mix, w_in, g_q, g_k, g_qx, *round_weights)


def _softmax_parts(s, sink=None):
    m = jnp.max(s, axis=-1, keepdims=True)
    if sink is not None:
        m = jnp.maximum(m, sink)
    p = jnp.exp2(s - m)
    den = jnp.sum(p, axis=-1, keepdims=True)
    if sink is not None:
        den = den + jnp.exp2(sink - m)
    return p, 1.0 / den


def _mix_out_kernel(sink_ref, q_ref, kc_ref, kp_ref, vc_ref, vp_ref, qx_ref, mk_ref, mv_ref,
                    b_ref, uc_ref, up_ref, cw_ref, bias_ref, band_ref, x_ref, w_ref, g_ref, x1_ref, h_ref):
    has_prev = pl.program_id(1) > 0
    rows = MIX_BLOCKS * WINDOW
    blocks = [slice(j * WINDOW, (j + 1) * WINDOW) for j in range(MIX_BLOCKS)]

    def project(y_cols, col0, acc):
        y = jnp.concatenate(y_cols, axis=1).astype(BF16)
        return acc + jnp.dot(y, w_ref[col0:col0 + y.shape[1], :], preferred_element_type=F32)

    u = uc_ref[...]
    prev = jnp.where(has_prev, up_ref[...], 0.0)
    ext = jnp.concatenate([prev, u], axis=0)
    cw = cw_ref[...]
    conv = cw[0:1] * ext[SUBLANES - 2:SUBLANES - 2 + rows]
    conv = conv + cw[1:2] * ext[SUBLANES - 1:SUBLANES - 1 + rows]
    conv = conv + cw[2:3] * u
    x1 = project([b_ref[...] * conv], 0, x_ref[...])

    for h in range(SWA_KV_HEADS):
        ksl = slice(h * HEAD_DIM, (h + 1) * HEAD_DIM)
        outs = [[None] * MIX_BLOCKS for _ in range(SWA_GROUP)]
        for j, rsl in enumerate(blocks):
            min_band = jnp.where(has_prev, 0.5, 1.5) if j == 0 else 0.5
            if j == 0:
                k_all = jnp.concatenate([kp_ref[:, ksl], kc_ref[rsl, ksl]], axis=0)
                v_all = jnp.concatenate([vp_ref[:, ksl], vc_ref[rsl, ksl]], axis=0)
            else:
                k_all = kc_ref[(j - 1) * WINDOW:(j + 1) * WINDOW, ksl]
                v_all = vc_ref[(j - 1) * WINDOW:(j + 1) * WINDOW, ksl]
            for g in range(SWA_GROUP):
                hh = h * SWA_GROUP + g
                q = q_ref[rsl, hh * HEAD_DIM:(hh + 1) * HEAD_DIM]
                s = lax.dot_general(q, k_all, NT_DIMS, preferred_element_type=F32) * SCALE_LOG2
                s = jnp.where(band_ref[...] > min_band, s + bias_ref[h, g * WINDOW:(g + 1) * WINDOW, :], NEG)
                p, inv = _softmax_parts(s, sink_ref[hh] * LOG2E)
                outs[g][j] = (jnp.dot(p.astype(BF16), v_all, preferred_element_type=F32) * inv).astype(BF16)
        x1 = project([jnp.concatenate(o, axis=0) for o in outs], CONV_WIDTH + h * SWA_GROUP * HEAD_DIM, x1)

    outs = [[None] * MIX_BLOCKS for _ in range(X_HEADS)]
    for j, rsl in enumerate(blocks):
        for hx in range(X_HEADS):
            sl = slice(hx * HEAD_DIM, (hx + 1) * HEAD_DIM)
            s = lax.dot_general(qx_ref[rsl, sl], mk_ref[:, sl], NT_DIMS, preferred_element_type=F32) * SCALE_LOG2
            p, inv = _softmax_parts(s)
            outs[hx][j] = (jnp.dot(p.astype(BF16), mv_ref[:, sl], preferred_element_type=F32) * inv).astype(BF16)
    x1 = project([jnp.concatenate(o, axis=0) for o in outs], CONV_WIDTH + SWA_WIDTH, x1)

    x1_ref[...] = x1
    h_ref[...] = _rms(x1, g_ref[...]).astype(BF16)


def _mix_out_prompt(sinks, q, kb, vb, qx, mkb, mvb, b, u, conv_w, bias_p, x2d, w_out, g_mlp, bsz, seq):
    rows = MIX_BLOCKS * WINDOW
    steps = seq // rows
    cur = lambda bi, i: (bi * steps + i, 0)
    prv = lambda bi, i: (jnp.maximum((bi * steps + i) * MIX_BLOCKS - 1, 0), 0)
    prv8 = lambda bi, i: (jnp.maximum((bi * steps + i) * (rows // SUBLANES) - 1, 0), 0)
    per_b = lambda bi, i: (bi, 0)
    dist = WINDOW + np.arange(WINDOW)[:, None] - np.arange(2 * WINDOW)[None, :]
    band = np.where((dist >= 0) & (dist < WINDOW), np.where(np.arange(2 * WINDOW)[None, :] < WINDOW, 1.0, 2.0), 0.0)
    return pl.pallas_call(
        _mix_out_kernel,
        grid=(bsz, steps),
        in_specs=[pl.BlockSpec(memory_space=pltpu.SMEM),
                  pl.BlockSpec((rows, SWA_WIDTH), cur),
                  pl.BlockSpec((rows, SWA_KV_WIDTH), cur), pl.BlockSpec((WINDOW, SWA_KV_WIDTH), prv),
                  pl.BlockSpec((rows, SWA_KV_WIDTH), cur), pl.BlockSpec((WINDOW, SWA_KV_WIDTH), prv),
                  pl.BlockSpec((rows, X_WIDTH), cur),
                  pl.BlockSpec((MEM_TOKENS, X_WIDTH), per_b), pl.BlockSpec((MEM_TOKENS, X_WIDTH), per_b),
                  pl.BlockSpec((rows, CONV_WIDTH), cur), pl.BlockSpec((rows, CONV_WIDTH), cur),
                  pl.BlockSpec((SUBLANES, CONV_WIDTH), prv8),
                  _resident((CONV_K, CONV_WIDTH)),
                  _resident((SWA_KV_HEADS, SWA_GROUP * WINDOW, 2 * WINDOW)),
                  _resident((WINDOW, 2 * WINDOW)),
                  pl.BlockSpec((rows, D_MODEL), cur),
                  _resident((D_MODEL, D_MODEL)), _resident((1, D_MODEL))],
        out_specs=(pl.BlockSpec((rows, D_MODEL), cur), pl.BlockSpec((rows, D_MODEL), cur)),
        out_shape=(jax.ShapeDtypeStruct((bsz * seq, D_MODEL), F32), jax.ShapeDtypeStruct((bsz * seq, D_MODEL), BF16)),
        compiler_params=_params("arbitrary", "arbitrary"),
        name="mix_out",
    )(sinks, q, kb, kb, vb, vb, qx, mkb, mvb, b, u, u, conv_w, bias_p, jnp.asarray(band, F32), x2d, w_out, g_mlp)


def _mix_sample_kernel(t_len, sink_ref, q_ref, kn_ref, vn_ref, ck_ref, cv_ref, qx_ref, cmk_ref, cmv_ref,
                       b_ref, u_ref, cc_ref, cw_ref, bias_ref, x_ref, w_ref, g_ref,
                       x1_ref, h_ref, sk_ref, sv_ref, o_ref):
    nb = cc_ref.shape[0]
    n_cache = SWA_KV_HEADS * WINDOW
    n_new = SWA_KV_HEADS * t_len
    n_mem = X_HEADS * MEM_TOKENS
    log_t = int(math.log2(t_len))

    rows = SWA_Q_HEADS * t_len
    r = lax.broadcasted_iota(jnp.int32, (nb * rows, SAMPLE_KEY_ROWS), 0) & (rows - 1)
    c = lax.broadcasted_iota(jnp.int32, (nb * rows, SAMPLE_KEY_ROWS), 1)
    key_pos = jnp.where(c < n_cache, c >> 1, WINDOW + ((c - n_cache) >> 1))
    dist = (r & (t_len - 1)) + WINDOW - key_pos
    valid = (dist >= 0) & (dist < WINDOW) & ((c & (SWA_KV_HEADS - 1)) == (r >> int(math.log2(SWA_GROUP * t_len))))
    bias = jnp.concatenate([bias_ref[...]] * nb, axis=0)
    sink_col = jnp.concatenate([jnp.full((t_len, 1), sink_ref[hh], F32) for hh in range(SWA_Q_HEADS)] * nb, axis=0)
    xrows = X_HEADS * t_len
    xr = lax.broadcasted_iota(jnp.int32, (nb * xrows, n_mem), 0) & (xrows - 1)
    xc = lax.broadcasted_iota(jnp.int32, (nb * xrows, n_mem), 1)
    x_valid = (xc & (X_HEADS - 1)) == (xr >> log_t)
    cw = cw_ref[...]
    trow = lax.broadcasted_iota(jnp.int32, (t_len, CONV_WIDTH), 0)
    zeros_pad = jnp.zeros((SAMPLE_KEY_ROWS - n_cache - n_new, HEAD_DIM), F32)

    s_list, sx_list = [], []
    for bi in range(nb):
        rsl = slice(bi * t_len, (bi + 1) * t_len)
        c0, n0, m0 = bi * n_cache, bi * n_new, bi * n_mem

        for dst, cache, new in ((sk_ref, ck_ref, kn_ref), (sv_ref, cv_ref, vn_ref)):
            dst[c0:c0 + n_cache - n_new, :] = cache[c0 + n_new:c0 + n_cache, :]
            dst[c0 + n_cache - n_new:c0 + n_cache, :] = new[n0:n0 + n_new, :]

        u = u_ref[rsl, :]
        cc = cc_ref[bi]
        cc1 = jnp.broadcast_to(cc[1:2], u.shape)
        cc0 = jnp.broadcast_to(cc[0:1], u.shape)
        u_m1 = jnp.where(trow >= 1, pltpu.roll(u, 1, 0), cc1)
        u_m2 = jnp.where(trow >= 2, pltpu.roll(u, 2, 0), jnp.where(trow == 1, cc1, cc0))
        conv = cw[0:1] * u_m2
        conv = conv + cw[1:2] * u_m1
        conv = conv + cw[2:3] * u
        o_ref[rsl, 0:CONV_WIDTH] = (b_ref[rsl, :] * conv).astype(o_ref.dtype)

        qb = q_ref[rsl, :]
        q_rows = jnp.concatenate([qb[:, hh * HEAD_DIM:(hh + 1) * HEAD_DIM] for hh in range(SWA_Q_HEADS)], axis=0)
        k_all = jnp.concatenate([ck_ref[c0:c0 + n_cache, :], kn_ref[n0:n0 + n_new, :], zeros_pad], axis=0)
        s_list.append(lax.dot_general(q_rows.astype(BF16), k_all.astype(BF16), NT_DIMS, preferred_element_type=F32))
        qxb = qx_ref[rsl, :]
        qx_rows = jnp.concatenate([qxb[:, hx * HEAD_DIM:(hx + 1) * HEAD_DIM] for hx in range(X_HEADS)], axis=0)
        mk = cmk_ref[m0:m0 + n_mem, :].astype(BF16)
        sx_list.append(lax.dot_general(qx_rows.astype(BF16), mk, NT_DIMS, preferred_element_type=F32))

    s = jnp.concatenate(s_list, axis=0) * SCALE_LOG2
    w, inv = _softmax_parts(jnp.where(valid, s + bias, NEG), sink_col * LOG2E)
    w = w.astype(BF16)
    sx = jnp.concatenate(sx_list, axis=0) * SCALE_LOG2
    wx, invx = _softmax_parts(jnp.where(x_valid, sx, NEG))
    wx = wx.astype(BF16)

    for bi in range(nb):
        rsl = slice(bi * t_len, (bi + 1) * t_len)
        c0, n0, m0 = bi * n_cache, bi * n_new, bi * n_mem
        v_all = jnp.concatenate([cv_ref[c0:c0 + n_cache, :], vn_ref[n0:n0 + n_new, :], zeros_pad], axis=0)
        srows = slice(bi * rows, (bi + 1) * rows)
        o = jnp.dot(w[srows], v_all.astype(BF16), preferred_element_type=F32) * inv[srows]
        for hh in range(SWA_Q_HEADS):
            col = CONV_WIDTH + hh * HEAD_DIM
            o_ref[rsl, col:col + HEAD_DIM] = o[hh * t_len:(hh + 1) * t_len].astype(o_ref.dtype)
        mv = cmv_ref[m0:m0 + n_mem, :].astype(BF16)
        xsl = slice(bi * xrows, (bi + 1) * xrows)
        ox = jnp.dot(wx[xsl], mv, preferred_element_type=F32) * invx[xsl]
        for hx in range(X_HEADS):
            col = CONV_WIDTH + SWA_WIDTH + hx * HEAD_DIM
            o_ref[rsl, col:col + HEAD_DIM] = ox[hx * t_len:(hx + 1) * t_len].astype(o_ref.dtype)

    x1 = x_ref[...] + jnp.dot(o_ref[...].astype(BF16), w_ref[...], preferred_element_type=F32)
    x1_ref[...] = x1
    h_ref[...] = _rms(x1, g_ref[...]).astype(BF16)


def _mix_out_sample(sinks, q, k_rows, v_rows, cache_k, cache_v, qx, cache_mk, cache_mv, b, u, cache_conv, conv_w,
                    bias_s, x2d, w_out, g_mlp, bsz, t_len):
    assert t_len == SUBLANES and bsz % SAMPLE_BATCH_TILE == 0 and SWA_KV_HEADS == 2
    nb = SAMPLE_BATCH_TILE
    n_cache, n_new, n_mem = SWA_KV_HEADS * WINDOW, SWA_KV_HEADS * t_len, X_HEADS * MEM_TOKENS
    row = lambda i: (i, 0)
    rows_of = lambda n, w: pl.BlockSpec((nb * n, w), row)
    return pl.pallas_call(
        functools.partial(_mix_sample_kernel, t_len),
        grid=(bsz // nb,),
        in_specs=[pl.BlockSpec(memory_space=pltpu.SMEM),
                  rows_of(t_len, SWA_WIDTH),
                  rows_of(n_new, HEAD_DIM), rows_of(n_new, HEAD_DIM),
                  rows_of(n_cache, HEAD_DIM), rows_of(n_cache, HEAD_DIM),
                  rows_of(t_len, X_WIDTH),
                  rows_of(n_mem, HEAD_DIM), rows_of(n_mem, HEAD_DIM),
                  rows_of(t_len, CONV_WIDTH), rows_of(t_len, CONV_WIDTH),
                  pl.BlockSpec((nb, CONV_K - 1, CONV_WIDTH), lambda i: (i, 0, 0)),
                  _resident((CONV_K, CONV_WIDTH)),
                  _resident((SWA_Q_HEADS * t_len, SAMPLE_KEY_ROWS)),
                  rows_of(t_len, D_MODEL), _resident((D_MODEL, D_MODEL)), _resident((1, D_MODEL))],
        out_specs=(rows_of(t_len, D_MODEL), rows_of(t_len, D_MODEL),
                   rows_of(n_cache, HEAD_DIM), rows_of(n_cache, HEAD_DIM)),
        out_shape=(jax.ShapeDtypeStruct((bsz * t_len, D_MODEL), F32),
                   jax.ShapeDtypeStruct((bsz * t_len, D_MODEL), BF16),
                   jax.ShapeDtypeStruct((bsz * n_cache, HEAD_DIM), F32),
                   jax.ShapeDtypeStruct((bsz * n_cache, HEAD_DIM), F32)),
        scratch_shapes=[pltpu.VMEM((nb * t_len, D_MODEL), F32)],
        compiler_params=_params("arbitrary"),
        name="mix_out_sample",
    )(sinks, q, k_rows, v_rows, cache_k, cache_v, qx, cache_mk, cache_mv, b, u, cache_conv, conv_w, bias_s,
      x2d, w_out, g_mlp)


def _mlp_kernel(x1_ref, h_ref, wu_ref, wd_ref, o_ref):
    @pl.when(pl.program_id(1) == 0)
    def _():
        o_ref[...] = x1_ref[...]

    chunks = [slice(c0, c0 + MLP_FF_CHUNK) for c0 in range(0, wu_ref.shape[1], MLP_FF_CHUNK)]
    acts = []
    for ff in chunks:
        a = jnp.maximum(jnp.dot(h_ref[...], wu_ref[:, ff], preferred_element_type=F32), 0.0)
        acts.append((a * a).astype(BF16))
    o_ref[...] += jnp.dot(jnp.concatenate(acts, axis=1), wd_ref[...], preferred_element_type=F32)


def _mlp_cast_kernel(x1_hbm, h_ref, wu_ref, wd_ref, o_ref, wub_ref, wdb_ref, sem):
    j = pl.program_id(1)
    residual_copy = pltpu.make_async_copy(x1_hbm, o_ref, sem)

    @pl.when(j == 0)
    def _():
        residual_copy.start()

    wu = wu_ref[...].astype(BF16)
    wub_ref[...] = wu
    a = jnp.maximum(jnp.dot(h_ref[...], wu, preferred_element_type=F32), 0.0)
    a = (a * a).astype(BF16)

    @pl.when(j == 0)
    def _():
        residual_copy.wait()

    for c0 in range(0, D_MODEL, MLP_CAST_FF_TILE):
        cols = slice(c0, c0 + MLP_CAST_FF_TILE)
        wd = wd_ref[:, cols].astype(BF16)
        wdb_ref[:, cols] = wd
        o_ref[:, cols] += jnp.dot(a, wd, preferred_element_type=F32)


def _mlp(x1, h, w_up, w_down):
    n = x1.shape[0]
    tile = min(MLP_ROW_TILE, n)
    rows = pl.BlockSpec((tile, D_MODEL), lambda i, j: (i, 0))
    return pl.pallas_call(
        _mlp_kernel,
        grid=(n // tile, D_FF // MLP_FF_TILE),
        in_specs=[rows, rows,
                  pl.BlockSpec((D_MODEL, MLP_FF_TILE), lambda i, j: (0, j)),
                  pl.BlockSpec((MLP_FF_TILE, D_MODEL), lambda i, j: (j, 0))],
        out_specs=rows,
        out_shape=jax.ShapeDtypeStruct((n, D_MODEL), F32),
        compiler_params=_params("arbitrary", "arbitrary", vmem_limit=VMEM_LIMIT_MLP_V7X),
        name="mlp",
    )(x1, h, w_up, w_down)


def _mlp_cast(x1, h, w_up, w_down):
    n = x1.shape[0]
    rows = pl.BlockSpec((n, D_MODEL), lambda i, j: (0, 0))
    up_spec = pl.BlockSpec((D_MODEL, MLP_CAST_FF_TILE), lambda i, j: (0, j))
    down_spec = pl.BlockSpec((MLP_CAST_FF_TILE, D_MODEL), lambda i, j: (j, 0))
    return pl.pallas_call(
        _mlp_cast_kernel,
        grid=(1, D_FF // MLP_CAST_FF_TILE),
        in_specs=[pl.BlockSpec(memory_space=pl.ANY), rows, up_spec, down_spec],
        out_specs=(rows, up_spec, down_spec),
        out_shape=(jax.ShapeDtypeStruct((n, D_MODEL), F32),
                   jax.ShapeDtypeStruct(w_up.shape, BF16), jax.ShapeDtypeStruct(w_down.shape, BF16)),
        scratch_shapes=[pltpu.SemaphoreType.DMA(())],
        compiler_params=_params("arbitrary", "arbitrary"),
        name="mlp_cast",
    )(x1, h, w_up, w_down)


def kernel(x_prompt, x_sample, mem_prompt, cache_conv, cache_swa_k, cache_swa_v, cache_mem_k, cache_mem_v,
           rel_bias_table, g_mix, w_in, conv_w, g_q_swa, g_k_swa, sinks, g_q_x, g_k_x, g_mem,
           w_mem_k, w_mem_v, w_out, g_mlp, w_up, w_down):
    depth = w_in.shape[0]
    bsz, seq, _ = x_prompt.shape
    dbsz, t_len, _ = x_sample.shape
    xp = x_prompt.reshape(bsz * seq, D_MODEL)
    xs = x_sample.reshape(dbsz * t_len, D_MODEL)
    mem2d = mem_prompt.reshape(bsz * MEM_TOKENS, D_MODEL)
    bias_p, bias_s = _bias_tables(rel_bias_table, t_len)

    outs = [[] for _ in range(8)]
    for l in range(depth):
        vec = lambda a: a[l].reshape(1, -1)
        mk, mv, mkb, mvb, wi = _memory_kv(mem2d, vec(g_mem), w_mem_k[l], w_mem_v[l], vec(g_k_x),
                                          round_weights=(w_in[l],))
        proj = functools.partial(_projections, g_mix=vec(g_mix), w_in=wi, g_q=vec(g_q_swa), g_k=vec(g_k_swa),
                                 g_qx=vec(g_q_x))
        pb, pu, pq, pk, pv, pkb, pvb, pqx, wo = proj(xp, q_dtype=BF16, round_weights=(w_out[l],))

        head_rows = lambda a: a.reshape(-1, HEAD_DIM)
        b, u, q, k, v, _, _, qx = proj(xs, q_dtype=F32)
        x1, h, sk, sv = _mix_out_sample(
            sinks[l], q, k, v, head_rows(cache_swa_k[l]), head_rows(cache_swa_v[l]),
            qx, head_rows(cache_mem_k[l]), head_rows(cache_mem_v[l]),
            b, u, cache_conv[l], conv_w[l], bias_s, xs, wo, vec(g_mlp), dbsz, t_len)
        xs, wu, wd = _mlp_cast(x1, h, w_up[l], w_down[l])
        outs[5].append(u.reshape(dbsz, t_len, CONV_WIDTH)[:, t_len - (CONV_K - 1):])
        outs[6].append(sk.reshape(dbsz, WINDOW, SWA_KV_HEADS, HEAD_DIM))
        outs[7].append(sv.reshape(dbsz, WINDOW, SWA_KV_HEADS, HEAD_DIM))

        x1, h = _mix_out_prompt(sinks[l], pq, pkb, pvb, pqx, mkb, mvb, pb, pu, conv_w[l], bias_p, xp, wo,
                                vec(g_mlp), bsz, seq)
        xp = _mlp(x1, h, wu, wd)
        outs[0].append(pu.reshape(bsz, seq, CONV_WIDTH)[:, seq - (CONV_K - 1):])
        last_window = lambda a: a.reshape(bsz, seq, SWA_KV_HEADS, HEAD_DIM)[:, seq - WINDOW:]
        outs[1].append(last_window(pk))
        outs[2].append(last_window(pv))
        outs[3].append(mk.reshape(bsz, MEM_TOKENS, X_HEADS, HEAD_DIM))
        outs[4].append(mv.reshape(bsz, MEM_TOKENS, X_HEADS, HEAD_DIM))

    return (xp.reshape(bsz, seq, D_MODEL), xs.reshape(dbsz, t_len, D_MODEL)) + tuple(jnp.stack(o) for o in outs)
```

```python
import functools
import math

import numpy as np
import jax
import jax.numpy as jnp
from jax import lax
from jax.experimental import pallas as pl
from jax.experimental.pallas import tpu as pltpu

D_MODEL = 2048
HEAD_DIM = 128
SWA_Q_HEADS = 8
SWA_KV_HEADS = 2
SWA_GROUP = SWA_Q_HEADS // SWA_KV_HEADS
SWA_WIDTH = SWA_Q_HEADS * HEAD_DIM
SWA_KV_WIDTH = SWA_KV_HEADS * HEAD_DIM
X_HEADS = 4
X_WIDTH = X_HEADS * HEAD_DIM
CONV_WIDTH = D_MODEL - SWA_WIDTH - X_WIDTH
CONV_K = 3
WINDOW = 128
NUM_BUCKETS = 32
MAX_DISTANCE = WINDOW
MEM_TOKENS = 256
D_FF = 4 * D_MODEL
EPS = 1e-6
NEG = -1e30
SCALE = HEAD_DIM ** -0.5
LOG2E = math.log2(math.e)
SCALE_LOG2 = SCALE * LOG2E

OFF_B = 0
OFF_C = CONV_WIDTH
OFF_H = 2 * CONV_WIDTH
OFF_Q = 3 * CONV_WIDTH
OFF_K = OFF_Q + SWA_WIDTH
OFF_V = OFF_K + SWA_KV_WIDTH
OFF_QX = OFF_V + SWA_KV_WIDTH
IN_WIDTH = OFF_QX + X_WIDTH

VMEM_LIMIT_V7X = 56 * 1024 * 1024
VMEM_LIMIT_MLP_V7X = 62 * 1024 * 1024
SUBLANES = 8

ROW_TILE = 512
ROW_CHUNK = 256
MLP_ROW_TILE = 512
MLP_FF_TILE = 2048
MLP_CAST_FF_TILE = 512
SAMPLE_BATCH_TILE = 8
MIX_BLOCKS = 4

BF16 = jnp.bfloat16
F32 = jnp.float32
NT_DIMS = (((1,), (1,)), ((), ()))


def _params(*sem, vmem_limit=VMEM_LIMIT_V7X):
    return pltpu.CompilerParams(dimension_semantics=sem, vmem_limit_bytes=vmem_limit)


def _resident(shape):
    nd = len(shape)
    return pl.BlockSpec(shape, lambda *_: (0,) * nd, pipeline_mode=pl.Buffered(1))


def _rms(x, g):
    ms = jnp.mean(x * x, axis=-1, keepdims=True)
    return x * lax.rsqrt(ms + EPS) * g


def _rel_bucket_np(dist):
    n = np.maximum(dist, 0)
    max_exact = NUM_BUCKETS // 2
    nf = np.maximum(n, 1).astype(np.float32)
    large = max_exact + (np.log(nf / np.float32(max_exact)) / np.float32(math.log(MAX_DISTANCE / max_exact))
                         * np.float32(NUM_BUCKETS - max_exact)).astype(np.int32)
    large = np.minimum(large, NUM_BUCKETS - 1)
    return np.where(n < max_exact, n, large).astype(np.int32)


SAMPLE_KEY_ROWS = 3 * WINDOW


def _sample_key_positions(t_len):
    c = np.arange(SAMPLE_KEY_ROWS)
    n_cache = SWA_KV_HEADS * WINDOW
    assert n_cache + SWA_KV_HEADS * t_len <= SAMPLE_KEY_ROWS
    return np.where(c < n_cache, c // SWA_KV_HEADS, WINDOW + (c - n_cache) // SWA_KV_HEADS)


def _bias_kernel(tab_ref, bp_ref, bs_ref, op_ref, os_ref):
    bp = bp_ref[...]
    bs = bs_ref[...]
    for hh in range(SWA_Q_HEADS):
        accp = jnp.zeros(bp.shape, F32)
        accs = jnp.zeros(bs.shape, F32)
        for k in range(NUM_BUCKETS):
            t = tab_ref[k * SWA_Q_HEADS + hh] * LOG2E
            accp = jnp.where(bp == k, t, accp)
            accs = jnp.where(bs == k, t, accs)
        h, g = divmod(hh, SWA_GROUP)
        op_ref[h, g * WINDOW:(g + 1) * WINDOW, :] = accp
        os_ref[hh * SUBLANES:(hh + 1) * SUBLANES, :] = accs


def _bias_tables(table, t_len):
    qi = np.arange(WINDOW)[:, None]
    kj = np.arange(2 * WINDOW)[None, :]
    bkt_p = _rel_bucket_np(WINDOW + qi - kj)
    key_pos = _sample_key_positions(t_len)
    bkt_s = _rel_bucket_np(np.arange(t_len)[:, None] + WINDOW - key_pos[None, :])
    return pl.pallas_call(
        _bias_kernel,
        out_shape=(jax.ShapeDtypeStruct((SWA_KV_HEADS, SWA_GROUP * WINDOW, 2 * WINDOW), F32),
                   jax.ShapeDtypeStruct((SWA_Q_HEADS * t_len, SAMPLE_KEY_ROWS), F32)),
        in_specs=[pl.BlockSpec(memory_space=pltpu.SMEM),
                  pl.BlockSpec(memory_space=pltpu.VMEM),
                  pl.BlockSpec(memory_space=pltpu.VMEM)],
        out_specs=(pl.BlockSpec(memory_space=pltpu.VMEM), pl.BlockSpec(memory_space=pltpu.VMEM)),
        name="bias",
    )(table.reshape(-1), jnp.asarray(bkt_p), jnp.asarray(bkt_s))


def _round_specs(weights, steps):
    specs = [pl.BlockSpec((w.shape[0] // steps, w.shape[1]), lambda i: (i, 0)) for w in weights]
    shapes = [jax.ShapeDtypeStruct(w.shape, BF16) for w in weights]
    return specs, shapes


def _round_slabs(srcs, dsts):
    for src, dst in zip(srcs, dsts):
        dst[...] = src[...].astype(BF16)


def _memkv_kernel(n_round, m_ref, g_ref, wk_ref, wv_ref, gk_ref, *refs):
    mk_ref, mv_ref, mkb_ref, mvb_ref = refs[n_round:n_round + 4]
    _round_slabs(refs[:n_round], refs[n_round + 4:])
    h = _rms(m_ref[...], g_ref[...]).astype(BF16)
    zk = jnp.dot(h, wk_ref[...].astype(BF16), preferred_element_type=F32)
    zv = jnp.dot(h, wv_ref[...].astype(BF16), preferred_element_type=F32)
    gk = gk_ref[...]
    tokens = m_ref.shape[0]
    for hx in range(X_HEADS):
        sl = slice(hx * HEAD_DIM, (hx + 1) * HEAD_DIM)
        head_rows = pl.ds(hx, tokens, stride=X_HEADS)
        mk = _rms(zk[:, sl], gk)
        mk_ref[head_rows, :] = mk
        mv_ref[head_rows, :] = zv[:, sl]
        mkb_ref[:, sl] = mk.astype(BF16)
    mvb_ref[...] = zv.astype(BF16)


def _memory_kv(mem2d, g_mem, wk, wv, g_k_x, round_weights=()):
    n = mem2d.shape[0]
    tile = MEM_TOKENS
    row = lambda i: (i, 0)
    head_rows = pl.BlockSpec((tile * X_HEADS, HEAD_DIM), row)
    round_specs, round_shapes = _round_specs(round_weights, n // tile)
    return pl.pallas_call(
        functools.partial(_memkv_kernel, len(round_weights)),
        grid=(n // tile,),
        in_specs=[pl.BlockSpec((tile, D_MODEL), row), _resident((1, D_MODEL)),
                  _resident((D_MODEL, X_WIDTH)), _resident((D_MODEL, X_WIDTH)), _resident((1, HEAD_DIM)),
                  *round_specs],
        out_specs=(head_rows, head_rows, pl.BlockSpec((tile, X_WIDTH), row), pl.BlockSpec((tile, X_WIDTH), row),
                   *round_specs),
        out_shape=(jax.ShapeDtypeStruct((n * X_HEADS, HEAD_DIM), F32),
                   jax.ShapeDtypeStruct((n * X_HEADS, HEAD_DIM), F32),
                   jax.ShapeDtypeStruct((n, X_WIDTH), BF16), jax.ShapeDtypeStruct((n, X_WIDTH), BF16),
                   *round_shapes),
        compiler_params=_params("arbitrary"),
        name="memkv",
    )(mem2d, g_mem, wk, wv, g_k_x, *round_weights)


def _proj_kernel(n_round, x_ref, g_ref, w_ref, gq_ref, gk_ref, gx_ref, *refs):
    b_ref, u_ref, q_ref, k_ref, v_ref, kb_ref, vb_ref, qx_ref = refs[n_round:n_round + 8]
    _round_slabs(refs[:n_round], refs[n_round + 8:])
    for r0 in range(0, x_ref.shape[0], ROW_CHUNK):
        r1 = r0 + ROW_CHUNK
        rows = slice(r0, r1)
        h = _rms(x_ref[rows, :], g_ref[...]).astype(BF16)

        def seg(lo, width):
            return jnp.dot(h, w_ref[:, lo:lo + width], preferred_element_type=F32)

        b_ref[rows, :] = seg(OFF_B, CONV_WIDTH)
        u_ref[rows, :] = seg(OFF_C, CONV_WIDTH) * seg(OFF_H, CONV_WIDTH)

        def head_norm(z, g, n_heads, out):
            for hh in range(n_heads):
                sl = slice(hh * HEAD_DIM, (hh + 1) * HEAD_DIM)
                out[rows, sl] = _rms(z[:, sl], g).astype(out.dtype)

        head_norm(seg(OFF_Q, SWA_WIDTH), gq_ref[...], SWA_Q_HEADS, q_ref)
        head_norm(seg(OFF_QX, X_WIDTH), gx_ref[...], X_HEADS, qx_ref)
        zk = seg(OFF_K, SWA_KV_WIDTH)
        zv = seg(OFF_V, SWA_KV_WIDTH)
        vb_ref[rows, :] = zv.astype(BF16)
        for hh in range(SWA_KV_HEADS):
            sl = slice(hh * HEAD_DIM, (hh + 1) * HEAD_DIM)
            head_rows = pl.ds(r0 * SWA_KV_HEADS + hh, r1 - r0, stride=SWA_KV_HEADS)
            k = _rms(zk[:, sl], gk_ref[...])
            k_ref[head_rows, :] = k
            v_ref[head_rows, :] = zv[:, sl]
            kb_ref[rows, sl] = k.astype(BF16)


def _projections(x2d, g_mix, w_in, g_q, g_k, g_qx, q_dtype, round_weights=()):
    n = x2d.shape[0]
    tile = min(ROW_TILE, n)
    row = lambda i: (i, 0)
    outs = ((1, CONV_WIDTH, F32), (1, CONV_WIDTH, F32), (1, SWA_WIDTH, q_dtype),
            (SWA_KV_HEADS, HEAD_DIM, F32), (SWA_KV_HEADS, HEAD_DIM, F32),
            (1, SWA_KV_WIDTH, BF16), (1, SWA_KV_WIDTH, BF16), (1, X_WIDTH, q_dtype))
    round_specs, round_shapes = _round_specs(round_weights, n // tile)
    return pl.pallas_call(
        functools.partial(_proj_kernel, len(round_weights)),
        grid=(n // tile,),
        in_specs=[pl.BlockSpec((tile, D_MODEL), row), _resident((1, D_MODEL)), _resident((D_MODEL, IN_WIDTH)),
                  _resident((1, HEAD_DIM)), _resident((1, HEAD_DIM)), _resident((1, HEAD_DIM)), *round_specs],
        out_specs=(*(pl.BlockSpec((tile * r, w), row) for r, w, _ in outs), *round_specs),
        out_shape=(*(jax.ShapeDtypeStruct((n * r, w), dt) for r, w, dt in outs), *round_shapes),
        compiler_params=_params("arbitrary"),
        name="proj",
    )(x2d, g_mix, w_in, g_q, g_k, g_qx, *round_weights)


def _softmax_parts(s, sink=None):
    m = jnp.max(s, axis=-1, keepdims=True)
    if sink is not None:
        m = jnp.maximum(m, sink)
    p = jnp.exp2(s - m)
    den = jnp.sum(p, axis=-1, keepdims=True)
    if sink is not None:
        den = den + jnp.exp2(sink - m)
    return p, 1.0 / den


def _mix_out_kernel(sink_ref, q_ref, kc_ref, kp_ref, vc_ref, vp_ref, qx_ref, mk_ref, mv_ref,
                    b_ref, uc_ref, up_ref, cw_ref, bias_ref, band_ref, x_ref, w_ref, g_ref, x1_ref, h_ref):
    has_prev = pl.program_id(1) > 0
    rows = MIX_BLOCKS * WINDOW
    blocks = [slice(j * WINDOW, (j + 1) * WINDOW) for j in range(MIX_BLOCKS)]

    def project(y_cols, col0, acc):
        y = jnp.concatenate(y_cols, axis=1).astype(BF16)
        return acc + jnp.dot(y, w_ref[col0:col0 + y.shape[1], :], preferred_element_type=F32)

    u = uc_ref[...]
    prev = jnp.where(has_prev, up_ref[...], 0.0)
    ext = jnp.concatenate([prev, u], axis=0)
    cw = cw_ref[...]
    conv = cw[0:1] * ext[SUBLANES - 2:SUBLANES - 2 + rows]
    conv = conv + cw[1:2] * ext[SUBLANES - 1:SUBLANES - 1 + rows]
    conv = conv + cw[2:3] * u
    x1 = project([b_ref[...] * conv], 0, x_ref[...])

    for h in range(SWA_KV_HEADS):
        ksl = slice(h * HEAD_DIM, (h + 1) * HEAD_DIM)
        outs = [[None] * MIX_BLOCKS for _ in range(SWA_GROUP)]
        for j, rsl in enumerate(blocks):
            min_band = jnp.where(has_prev, 0.5, 1.5) if j == 0 else 0.5
            if j == 0:
                k_all = jnp.concatenate([kp_ref[:, ksl], kc_ref[rsl, ksl]], axis=0)
                v_all = jnp.concatenate([vp_ref[:, ksl], vc_ref[rsl, ksl]], axis=0)
            else:
                k_all = kc_ref[(j - 1) * WINDOW:(j + 1) * WINDOW, ksl]
                v_all = vc_ref[(j - 1) * WINDOW:(j + 1) * WINDOW, ksl]
            for g in range(SWA_GROUP):
                hh = h * SWA_GROUP + g
                q = q_ref[rsl, hh * HEAD_DIM:(hh + 1) * HEAD_DIM]
                s = lax.dot_general(q, k_all, NT_DIMS, preferred_element_type=F32) * SCALE_LOG2
                s = jnp.where(band_ref[...] > min_band, s + bias_ref[h, g * WINDOW:(g + 1) * WINDOW, :], NEG)
                p, inv = _softmax_parts(s, sink_ref[hh] * LOG2E)
                outs[g][j] = (jnp.dot(p.astype(BF16), v_all, preferred_element_type=F32) * inv).astype(BF16)
        x1 = project([jnp.concatenate(o, axis=0) for o in outs], CONV_WIDTH + h * SWA_GROUP * HEAD_DIM, x1)

    outs = [[None] * MIX_BLOCKS for _ in range(X_HEADS)]
    for j, rsl in enumerate(blocks):
        for hx in range(X_HEADS):
            sl = slice(hx * HEAD_DIM, (hx + 1) * HEAD_DIM)
            s = lax.dot_general(qx_ref[rsl, sl], mk_ref[:, sl], NT_DIMS, preferred_element_type=F32) * SCALE_LOG2
            p, inv = _softmax_parts(s)
            outs[hx][j] = (jnp.dot(p.astype(BF16), mv_ref[:, sl], preferred_element_type=F32) * inv).astype(BF16)
    x1 = project([jnp.concatenate(o, axis=0) for o in outs], CONV_WIDTH + SWA_WIDTH, x1)

    x1_ref[...] = x1
    h_ref[...] = _rms(x1, g_ref[...]).astype(BF16)


def _mix_out_prompt(sinks, q, kb, vb, qx, mkb, mvb, b, u, conv_w, bias_p, x2d, w_out, g_mlp, bsz, seq):
    rows = MIX_BLOCKS * WINDOW
    steps = seq // rows
    cur = lambda bi, i: (bi * steps + i, 0)
    prv = lambda bi, i: (jnp.maximum((bi * steps + i) * MIX_BLOCKS - 1, 0), 0)
    prv8 = lambda bi, i: (jnp.maximum((bi * steps + i) * (rows // SUBLANES) - 1, 0), 0)
    per_b = lambda bi, i: (bi, 0)
    dist = WINDOW + np.arange(WINDOW)[:, None] - np.arange(2 * WINDOW)[None, :]
    band = np.where((dist >= 0) & (dist < WINDOW), np.where(np.arange(2 * WINDOW)[None, :] < WINDOW, 1.0, 2.0), 0.0)
    return pl.pallas_call(
        _mix_out_kernel,
        grid=(bsz, steps),
        in_specs=[pl.BlockSpec(memory_space=pltpu.SMEM),
                  pl.BlockSpec((rows, SWA_WIDTH), cur),
                  pl.BlockSpec((rows, SWA_KV_WIDTH), cur), pl.BlockSpec((WINDOW, SWA_KV_WIDTH), prv),
                  pl.BlockSpec((rows, SWA_KV_WIDTH), cur), pl.BlockSpec((WINDOW, SWA_KV_WIDTH), prv),
                  pl.BlockSpec((rows, X_WIDTH), cur),
                  pl.BlockSpec((MEM_TOKENS, X_WIDTH), per_b), pl.BlockSpec((MEM_TOKENS, X_WIDTH), per_b),
                  pl.BlockSpec((rows, CONV_WIDTH), cur), pl.BlockSpec((rows, CONV_WIDTH), cur),
                  pl.BlockSpec((SUBLANES, CONV_WIDTH), prv8),
                  _resident((CONV_K, CONV_WIDTH)),
                  _resident((SWA_KV_HEADS, SWA_GROUP * WINDOW, 2 * WINDOW)),
                  _resident((WINDOW, 2 * WINDOW)),
                  pl.BlockSpec((rows, D_MODEL), cur),
                  _resident((D_MODEL, D_MODEL)), _resident((1, D_MODEL))],
        out_specs=(pl.BlockSpec((rows, D_MODEL), cur), pl.BlockSpec((rows, D_MODEL), cur)),
        out_shape=(jax.ShapeDtypeStruct((bsz * seq, D_MODEL), F32), jax.ShapeDtypeStruct((bsz * seq, D_MODEL), BF16)),
        compiler_params=_params("arbitrary", "arbitrary"),
        name="mix_out",
    )(sinks, q, kb, kb, vb, vb, qx, mkb, mvb, b, u, u, conv_w, bias_p, jnp.asarray(band, F32), x2d, w_out, g_mlp)


def _mix_sample_kernel(t_len, sink_ref, q_ref, kn_ref, vn_ref, ck_ref, cv_ref, qx_ref, cmk_ref, cmv_ref,
                       b_ref, u_ref, cc_ref, cw_ref, bias_ref, x_ref, w_ref, g_ref,
                       x1_ref, h_ref, sk_ref, sv_ref, o_ref):
    nb = cc_ref.shape[0]
    n_cache = SWA_KV_HEADS * WINDOW
    n_new = SWA_KV_HEADS * t_len
    n_mem = X_HEADS * MEM_TOKENS
    log_t = int(math.log2(t_len))

    rows = SWA_Q_HEADS * t_len
    r = lax.broadcasted_iota(jnp.int32, (nb * rows, SAMPLE_KEY_ROWS), 0) & (rows - 1)
    c = lax.broadcasted_iota(jnp.int32, (nb * rows, SAMPLE_KEY_ROWS), 1)
    key_pos = jnp.where(c < n_cache, c >> 1, WINDOW + ((c - n_cache) >> 1))
    dist = (r & (t_len - 1)) + WINDOW - key_pos
    valid = (dist >= 0) & (dist < WINDOW) & ((c & (SWA_KV_HEADS - 1)) == (r >> int(math.log2(SWA_GROUP * t_len))))
    bias = jnp.concatenate([bias_ref[...]] * nb, axis=0)
    sink_col = jnp.concatenate([jnp.full((t_len, 1), sink_ref[hh], F32) for hh in range(SWA_Q_HEADS)] * nb, axis=0)
    xrows = X_HEADS * t_len
    xr = lax.broadcasted_iota(jnp.int32, (nb * xrows, n_mem), 0) & (xrows - 1)
    xc = lax.broadcasted_iota(jnp.int32, (nb * xrows, n_mem), 1)
    x_valid = (xc & (X_HEADS - 1)) == (xr >> log_t)
    cw = cw_ref[...]
    trow = lax.broadcasted_iota(jnp.int32, (t_len, CONV_WIDTH), 0)
    zeros_pad = jnp.zeros((SAMPLE_KEY_ROWS - n_cache - n_new, HEAD_DIM), F32)

    s_list, sx_list = [], []
    for bi in range(nb):
        rsl = slice(bi * t_len, (bi + 1) * t_len)
        c0, n0, m0 = bi * n_cache, bi * n_new, bi * n_mem

        for dst, cache, new in ((sk_ref, ck_ref, kn_ref), (sv_ref, cv_ref, vn_ref)):
            dst[c0:c0 + n_cache - n_new, :] = cache[c0 + n_new:c0 + n_cache, :]
            dst[c0 + n_cache - n_new:c0 + n_cache, :] = new[n0:n0 + n_new, :]

        u = u_ref[rsl, :]
        cc = cc_ref[bi]
        cc1 = jnp.broadcast_to(cc[1:2], u.shape)
        cc0 = jnp.broadcast_to(cc[0:1], u.shape)
        u_m1 = jnp.where(trow >= 1, pltpu.roll(u, 1, 0), cc1)
        u_m2 = jnp.where(trow >= 2, pltpu.roll(u, 2, 0), jnp.where(trow == 1, cc1, cc0))
        conv = cw[0:1] * u_m2
        conv = conv + cw[1:2] * u_m1
        conv = conv + cw[2:3] * u
        o_ref[rsl, 0:CONV_WIDTH] = (b_ref[rsl, :] * conv).astype(o_ref.dtype)

        qb = q_ref[rsl, :]
        q_rows = jnp.concatenate([qb[:, hh * HEAD_DIM:(hh + 1) * HEAD_DIM] for hh in range(SWA_Q_HEADS)], axis=0)
        k_all = jnp.concatenate([ck_ref[c0:c0 + n_cache, :], kn_ref[n0:n0 + n_new, :], zeros_pad], axis=0)
        s_list.append(lax.dot_general(q_rows.astype(BF16), k_all.astype(BF16), NT_DIMS, preferred_element_type=F32))
        qxb = qx_ref[rsl, :]
        qx_rows = jnp.concatenate([qxb[:, hx * HEAD_DIM:(hx + 1) * HEAD_DIM] for hx in range(X_HEADS)], axis=0)
        mk = cmk_ref[m0:m0 + n_mem, :].astype(BF16)
        sx_list.append(lax.dot_general(qx_rows.astype(BF16), mk, NT_DIMS, preferred_element_type=F32))

    s = jnp.concatenate(s_list, axis=0) * SCALE_LOG2
    w, inv = _softmax_parts(jnp.where(valid, s + bias, NEG), sink_col * LOG2E)
    w = w.astype(BF16)
    sx = jnp.concatenate(sx_list, axis=0) * SCALE_LOG2
    wx, invx = _softmax_parts(jnp.where(x_valid, sx, NEG))
    wx = wx.astype(BF16)

    for bi in range(nb):
        rsl = slice(bi * t_len, (bi + 1) * t_len)
        c0, n0, m0 = bi * n_cache, bi * n_new, bi * n_mem
        v_all = jnp.concatenate([cv_ref[c0:c0 + n_cache, :], vn_ref[n0:n0 + n_new, :], zeros_pad], axis=0)
        srows = slice(bi * rows, (bi + 1) * rows)
        o = jnp.dot(w[srows], v_all.astype(BF16), preferred_element_type=F32) * inv[srows]
        for hh in range(SWA_Q_HEADS):
            col = CONV_WIDTH + hh * HEAD_DIM
            o_ref[rsl, col:col + HEAD_DIM] = o[hh * t_len:(hh + 1) * t_len].astype(o_ref.dtype)
        mv = cmv_ref[m0:m0 + n_mem, :].astype(BF16)
        xsl = slice(bi * xrows, (bi + 1) * xrows)
        ox = jnp.dot(wx[xsl], mv, preferred_element_type=F32) * invx[xsl]
        for hx in range(X_HEADS):
            col = CONV_WIDTH + SWA_WIDTH + hx * HEAD_DIM
            o_ref[rsl, col:col + HEAD_DIM] = ox[hx * t_len:(hx + 1) * t_len].astype(o_ref.dtype)

    x1 = x_ref[...] + jnp.dot(o_ref[...].astype(BF16), w_ref[...], preferred_element_type=F32)
    x1_ref[...] = x1
    h_ref[...] = _rms(x1, g_ref[...]).astype(BF16)


def _mix_out_sample(sinks, q, k_rows, v_rows, cache_k, cache_v, qx, cache_mk, cache_mv, b, u, cache_conv, conv_w,
                    bias_s, x2d, w_out, g_mlp, bsz, t_len):
    assert t_len == SUBLANES and bsz % SAMPLE_BATCH_TILE == 0 and SWA_KV_HEADS == 2
    nb = SAMPLE_BATCH_TILE
    n_cache, n_new, n_mem = SWA_KV_HEADS * WINDOW, SWA_KV_HEADS * t_len, X_HEADS * MEM_TOKENS
    row = lambda i: (i, 0)
    rows_of = lambda n, w: pl.BlockSpec((nb * n, w), row)
    return pl.pallas_call(
        functools.partial(_mix_sample_kernel, t_len),
        grid=(bsz // nb,),
        in_specs=[pl.BlockSpec(memory_space=pltpu.SMEM),
                  rows_of(t_len, SWA_WIDTH),
                  rows_of(n_new, HEAD_DIM), rows_of(n_new, HEAD_DIM),
                  rows_of(n_cache, HEAD_DIM), rows_of(n_cache, HEAD_DIM),
                  rows_of(t_len, X_WIDTH),
                  rows_of(n_mem, HEAD_DIM), rows_of(n_mem, HEAD_DIM),
                  rows_of(t_len, CONV_WIDTH), rows_of(t_len, CONV_WIDTH),
                  pl.BlockSpec((nb, CONV_K - 1, CONV_WIDTH), lambda i: (i, 0, 0)),
                  _resident((CONV_K, CONV_WIDTH)),
                  _resident((SWA_Q_HEADS * t_len, SAMPLE_KEY_ROWS)),
                  rows_of(t_len, D_MODEL), _resident((D_MODEL, D_MODEL)), _resident((1, D_MODEL))],
        out_specs=(rows_of(t_len, D_MODEL), rows_of(t_len, D_MODEL),
                   rows_of(n_cache, HEAD_DIM), rows_of(n_cache, HEAD_DIM)),
        out_shape=(jax.ShapeDtypeStruct((bsz * t_len, D_MODEL), F32),
                   jax.ShapeDtypeStruct((bsz * t_len, D_MODEL), BF16),
                   jax.ShapeDtypeStruct((bsz * n_cache, HEAD_DIM), F32),
                   jax.ShapeDtypeStruct((bsz * n_cache, HEAD_DIM), F32)),
        scratch_shapes=[pltpu.VMEM((nb * t_len, D_MODEL), F32)],
        compiler_params=_params("arbitrary"),
        name="mix_out_sample",
    )(sinks, q, k_rows, v_rows, cache_k, cache_v, qx, cache_mk, cache_mv, b, u, cache_conv, conv_w, bias_s,
      x2d, w_out, g_mlp)


def _mlp_kernel(x1_ref, h_ref, wu_ref, wd_ref, o_ref):
    @pl.when(pl.program_id(1) == 0)
    def _():
        o_ref[...] = x1_ref[...]

    a = jnp.maximum(jnp.dot(h_ref[...], wu_ref[...], preferred_element_type=F32), 0.0)
    o_ref[...] += jnp.dot((a * a).astype(BF16), wd_ref[...], preferred_element_type=F32)


def _mlp_cast_kernel(x1_hbm, h_ref, wu_ref, wd_ref, o_ref, wub_ref, wdb_ref, sem):
    j = pl.program_id(1)
    residual_copy = pltpu.make_async_copy(x1_hbm, o_ref, sem)

    @pl.when(j == 0)
    def _():
        residual_copy.start()

    wu = wu_ref[...].astype(BF16)
    wub_ref[...] = wu
    a = jnp.maximum(jnp.dot(h_ref[...], wu, preferred_element_type=F32), 0.0)
    a = (a * a).astype(BF16)

    @pl.when(j == 0)
    def _():
        residual_copy.wait()

    for c0 in range(0, D_MODEL, MLP_CAST_FF_TILE):
        cols = slice(c0, c0 + MLP_CAST_FF_TILE)
        wd = wd_ref[:, cols].astype(BF16)
        wdb_ref[:, cols] = wd
        o_ref[:, cols] += jnp.dot(a, wd, preferred_element_type=F32)


def _mlp(x1, h, w_up, w_down):
    n = x1.shape[0]
    tile = min(MLP_ROW_TILE, n)
    rows = pl.BlockSpec((tile, D_MODEL), lambda i, j: (i, 0))
    return pl.pallas_call(
        _mlp_kernel,
        grid=(n // tile, D_FF // MLP_FF_TILE),
        in_specs=[rows, rows,
                  pl.BlockSpec((D_MODEL, MLP_FF_TILE), lambda i, j: (0, j)),
                  pl.BlockSpec((MLP_FF_TILE, D_MODEL), lambda i, j: (j, 0))],
        out_specs=rows,
        out_shape=jax.ShapeDtypeStruct((n, D_MODEL), F32),
        compiler_params=_params("arbitrary", "arbitrary", vmem_limit=VMEM_LIMIT_MLP_V7X),
        name="mlp",
    )(x1, h, w_up, w_down)


def _mlp_cast(x1, h, w_up, w_down):
    n = x1.shape[0]
    rows = pl.BlockSpec((n, D_MODEL), lambda i, j: (0, 0))
    up_spec = pl.BlockSpec((D_MODEL, MLP_CAST_FF_TILE), lambda i, j: (0, j))
    down_spec = pl.BlockSpec((MLP_CAST_FF_TILE, D_MODEL), lambda i, j: (j, 0))
    return pl.pallas_call(
        _mlp_cast_kernel,
        grid=(1, D_FF // MLP_CAST_FF_TILE),
        in_specs=[pl.BlockSpec(memory_space=pl.ANY), rows, up_spec, down_spec],
        out_specs=(rows, up_spec, down_spec),
        out_shape=(jax.ShapeDtypeStruct((n, D_MODEL), F32),
                   jax.ShapeDtypeStruct(w_up.shape, BF16), jax.ShapeDtypeStruct(w_down.shape, BF16)),
        scratch_shapes=[pltpu.SemaphoreType.DMA(())],
        compiler_params=_params("arbitrary", "arbitrary"),
        name="mlp_cast",
    )(x1, h, w_up, w_down)


def kernel(x_prompt, x_sample, mem_prompt, cache_conv, cache_swa_k, cache_swa_v, cache_mem_k, cache_mem_v,
           rel_bias_table, g_mix, w_in, conv_w, g_q_swa, g_k_swa, sinks, g_q_x, g_k_x, g_mem,
           w_mem_k, w_mem_v, w_out, g_mlp, w_up, w_down):
    depth = w_in.shape[0]
    bsz, seq, _ = x_prompt.shape
    dbsz, t_len, _ = x_sample.shape
    xp = x_prompt.reshape(bsz * seq, D_MODEL)
    xs = x_sample.reshape(dbsz * t_len, D_MODEL)
    mem2d = mem_prompt.reshape(bsz * MEM_TOKENS, D_MODEL)
    bias_p, bias_s = _bias_tables(rel_bias_table, t_len)

    outs = [[] for _ in range(8)]
    for l in range(depth):
        vec = lambda a: a[l].reshape(1, -1)
        mk, mv, mkb, mvb, wi = _memory_kv(mem2d, vec(g_mem), w_mem_k[l], w_mem_v[l], vec(g_k_x),
                                          round_weights=(w_in[l],))
        proj = functools.partial(_projections, g_mix=vec(g_mix), w_in=wi, g_q=vec(g_q_swa), g_k=vec(g_k_swa),
                                 g_qx=vec(g_q_x))
        pb, pu, pq, pk, pv, pkb, pvb, pqx, wo = proj(xp, q_dtype=BF16, round_weights=(w_out[l],))

        head_rows = lambda a: a.reshape(-1, HEAD_DIM)
        b, u, q, k, v, _, _, qx = proj(xs, q_dtype=F32)
        x1, h, sk, sv = _mix_out_sample(
            sinks[l], q, k, v, head_rows(cache_swa_k[l]), head_rows(cache_swa_v[l]),
            qx, head_rows(cache_mem_k[l]), head_rows(cache_mem_v[l]),
            b, u, cache_conv[l], conv_w[l], bias_s, xs, wo, vec(g_mlp), dbsz, t_len)
        xs, wu, wd = _mlp_cast(x1, h, w_up[l], w_down[l])
        outs[5].append(u.reshape(dbsz, t_len, CONV_WIDTH)[:, t_len - (CONV_K - 1):])
        outs[6].append(sk.reshape(dbsz, WINDOW, SWA_KV_HEADS, HEAD_DIM))
        outs[7].append(sv.reshape(dbsz, WINDOW, SWA_KV_HEADS, HEAD_DIM))

        x1, h = _mix_out_prompt(sinks[l], pq, pkb, pvb, pqx, mkb, mvb, pb, pu, conv_w[l], bias_p, xp, wo,
                                vec(g_mlp), bsz, seq)
        xp = _mlp(x1, h, wu, wd)
        outs[0].append(pu.reshape(bsz, seq, CONV_WIDTH)[:, seq - (CONV_K - 1):])
        last_window = lambda a: a.reshape(bsz, seq, SWA_KV_HEADS, HEAD_DIM)[:, seq - WINDOW:]
        outs[1].append(last_window(pk))
        outs[2].append(last_window(pv))
        outs[3].append(mk.reshape(bsz, MEM_TOKENS, X_HEADS, HEAD_DIM))
        outs[4].append(mv.reshape(bsz, MEM_TOKENS, X_HEADS, HEAD_DIM))

    return (xp.reshape(bsz, seq, D_MODEL), xs.reshape(dbsz, t_len, D_MODEL)) + tuple(jnp.stack(o) for o in outs)
```

```python
import functools
import math

import numpy as np
import jax
import jax.numpy as jnp
from jax import lax
from jax.experimental import pallas as pl
from jax.experimental.pallas import tpu as pltpu

D_MODEL = 2048
HEAD_DIM = 128
SWA_Q_HEADS = 8
SWA_KV_HEADS = 2
SWA_GROUP = SWA_Q_HEADS // SWA_KV_HEADS
SWA_WIDTH = SWA_Q_HEADS * HEAD_DIM
SWA_KV_WIDTH = SWA_KV_HEADS * HEAD_DIM
X_HEADS = 4
X_WIDTH = X_HEADS * HEAD_DIM
CONV_WIDTH = D_MODEL - SWA_WIDTH - X_WIDTH
CONV_K = 3
WINDOW = 128
NUM_BUCKETS = 32
MAX_DISTANCE = WINDOW
MEM_TOKENS = 256
D_FF = 4 * D_MODEL
EPS = 1e-6
NEG = -1e30
SCALE = HEAD_DIM ** -0.5
LOG2E = math.log2(math.e)
SCALE_LOG2 = SCALE * LOG2E

OFF_B = 0
OFF_C = CONV_WIDTH
OFF_H = 2 * CONV_WIDTH
OFF_Q = 3 * CONV_WIDTH
OFF_K = OFF_Q + SWA_WIDTH
OFF_V = OFF_K + SWA_KV_WIDTH
OFF_QX = OFF_V + SWA_KV_WIDTH
IN_WIDTH = OFF_QX + X_WIDTH

VMEM_LIMIT_V7X = 56 * 1024 * 1024
VMEM_LIMIT_MLP_V7X = 62 * 1024 * 1024
SUBLANES = 8

ROW_TILE = 512
ROW_CHUNK = 256
MLP_ROW_TILE = 512
MLP_FF_TILE = 2048
MLP_CAST_FF_TILE = 512
SAMPLE_BATCH_TILE = 8
MIX_BLOCKS = 4

BF16 = jnp.bfloat16
F32 = jnp.float32
NT_DIMS = (((1,), (1,)), ((), ()))


def _params(*sem, vmem_limit=VMEM_LIMIT_V7X):
    return pltpu.CompilerParams(dimension_semantics=sem, vmem_limit_bytes=vmem_limit)


def _resident(shape):
    nd = len(shape)
    return pl.BlockSpec(shape, lambda *_: (0,) * nd, pipeline_mode=pl.Buffered(1))


def _rms(x, g):
    ms = jnp.mean(x * x, axis=-1, keepdims=True)
    return x * lax.rsqrt(ms + EPS) * g


def _rel_bucket_np(dist):
    n = np.maximum(dist, 0)
    max_exact = NUM_BUCKETS // 2
    nf = np.maximum(n, 1).astype(np.float32)
    large = max_exact + (np.log(nf / np.float32(max_exact)) / np.float32(math.log(MAX_DISTANCE / max_exact))
                         * np.float32(NUM_BUCKETS - max_exact)).astype(np.int32)
    large = np.minimum(large, NUM_BUCKETS - 1)
    return np.where(n < max_exact, n, large).astype(np.int32)


SAMPLE_KEY_ROWS = 3 * WINDOW


def _sample_key_positions(t_len):
    c = np.arange(SAMPLE_KEY_ROWS)
    n_cache = SWA_KV_HEADS * WINDOW
    assert n_cache + SWA_KV_HEADS * t_len <= SAMPLE_KEY_ROWS
    return np.where(c < n_cache, c // SWA_KV_HEADS, WINDOW + (c - n_cache) // SWA_KV_HEADS)


def _bias_kernel(tab_ref, bp_ref, bs_ref, op_ref, os_ref):
    bp = bp_ref[...]
    bs = bs_ref[...]
    for hh in range(SWA_Q_HEADS):
        accp = jnp.zeros(bp.shape, F32)
        accs = jnp.zeros(bs.shape, F32)
        for k in range(NUM_BUCKETS):
            t = tab_ref[k * SWA_Q_HEADS + hh] * LOG2E
            accp = jnp.where(bp == k, t, accp)
            accs = jnp.where(bs == k, t, accs)
        h, g = divmod(hh, SWA_GROUP)
        op_ref[h, g * WINDOW:(g + 1) * WINDOW, :] = accp
        os_ref[hh * SUBLANES:(hh + 1) * SUBLANES, :] = accs


def _bias_tables(table, t_len):
    qi = np.arange(WINDOW)[:, None]
    kj = np.arange(2 * WINDOW)[None, :]
    bkt_p = _rel_bucket_np(WINDOW + qi - kj)
    key_pos = _sample_key_positions(t_len)
    bkt_s = _rel_bucket_np(np.arange(t_len)[:, None] + WINDOW - key_pos[None, :])
    return pl.pallas_call(
        _bias_kernel,
        out_shape=(jax.ShapeDtypeStruct((SWA_KV_HEADS, SWA_GROUP * WINDOW, 2 * WINDOW), F32),
                   jax.ShapeDtypeStruct((SWA_Q_HEADS * t_len, SAMPLE_KEY_ROWS), F32)),
        in_specs=[pl.BlockSpec(memory_space=pltpu.SMEM),
                  pl.BlockSpec(memory_space=pltpu.VMEM),
                  pl.BlockSpec(memory_space=pltpu.VMEM)],
        out_specs=(pl.BlockSpec(memory_space=pltpu.VMEM), pl.BlockSpec(memory_space=pltpu.VMEM)),
        name="bias",
    )(table.reshape(-1), jnp.asarray(bkt_p), jnp.asarray(bkt_s))


def _round_specs(weights, steps):
    specs = [pl.BlockSpec((w.shape[0] // steps, w.shape[1]), lambda i: (i, 0)) for w in weights]
    shapes = [jax.ShapeDtypeStruct(w.shape, BF16) for w in weights]
    return specs, shapes


def _round_slabs(srcs, dsts):
    for src, dst in zip(srcs, dsts):
        dst[...] = src[...].astype(BF16)


def _memkv_kernel(n_round, m_ref, g_ref, wk_ref, wv_ref, gk_ref, *refs):
    mk_ref, mv_ref, mkb_ref, mvb_ref = refs[n_round:n_round + 4]
    _round_slabs(refs[:n_round], refs[n_round + 4:])
    h = _rms(m_ref[...], g_ref[...]).astype(BF16)
    zk = jnp.dot(h, wk_ref[...].astype(BF16), preferred_element_type=F32)
    zv = jnp.dot(h, wv_ref[...].astype(BF16), preferred_element_type=F32)
    gk = gk_ref[...]
    tokens = m_ref.shape[0]
    for hx in range(X_HEADS):
        sl = slice(hx * HEAD_DIM, (hx + 1) * HEAD_DIM)
        head_rows = pl.ds(hx, tokens, stride=X_HEADS)
        mk = _rms(zk[:, sl], gk)
        mk_ref[head_rows, :] = mk
        mv_ref[head_rows, :] = zv[:, sl]
        mkb_ref[:, sl] = mk.astype(BF16)
    mvb_ref[...] = zv.astype(BF16)


def _memory_kv(mem2d, g_mem, wk, wv, g_k_x, round_weights=()):
    n = mem2d.shape[0]
    tile = MEM_TOKENS
    row = lambda i: (i, 0)
    head_rows = pl.BlockSpec((tile * X_HEADS, HEAD_DIM), row)
    round_specs, round_shapes = _round_specs(round_weights, n // tile)
    return pl.pallas_call(
        functools.partial(_memkv_kernel, len(round_weights)),
        grid=(n // tile,),
        in_specs=[pl.BlockSpec((tile, D_MODEL), row), _resident((1, D_MODEL)),
                  _resident((D_MODEL, X_WIDTH)), _resident((D_MODEL, X_WIDTH)), _resident((1, HEAD_DIM)),
                  *round_specs],
        out_specs=(head_rows, head_rows, pl.BlockSpec((tile, X_WIDTH), row), pl.BlockSpec((tile, X_WIDTH), row),
                   *round_specs),
        out_shape=(jax.ShapeDtypeStruct((n * X_HEADS, HEAD_DIM), F32),
                   jax.ShapeDtypeStruct((n * X_HEADS, HEAD_DIM), F32),
                   jax.ShapeDtypeStruct((n, X_WIDTH), BF16), jax.ShapeDtypeStruct((n, X_WIDTH), BF16),
                   *round_shapes),
        compiler_params=_params("arbitrary"),
        name="memkv",
    )(mem2d, g_mem, wk, wv, g_k_x, *round_weights)


def _proj_kernel(n_round, x_ref, g_ref, w_ref, gq_ref, gk_ref, gx_ref, *refs):
    b_ref, u_ref, q_ref, k_ref, v_ref, kb_ref, vb_ref, qx_ref = refs[n_round:n_round + 8]
    _round_slabs(refs[:n_round], refs[n_round + 8:])
    for r0 in range(0, x_ref.shape[0], ROW_CHUNK):
        r1 = r0 + ROW_CHUNK
        rows = slice(r0, r1)
        h = _rms(x_ref[rows, :], g_ref[...]).astype(BF16)

        def seg(lo, width):
            return jnp.dot(h, w_ref[:, lo:lo + width], preferred_element_type=F32)

        b_ref[rows, :] = seg(OFF_B, CONV_WIDTH)
        u_ref[rows, :] = seg(OFF_C, CONV_WIDTH) * seg(OFF_H, CONV_WIDTH)

        def head_norm(z, g, n_heads, out):
            for hh in range(n_heads):
                sl = slice(hh * HEAD_DIM, (hh + 1) * HEAD_DIM)
                out[rows, sl] = _rms(z[:, sl], g).astype(out.dtype)

        head_norm(seg(OFF_Q, SWA_WIDTH), gq_ref[...], SWA_Q_HEADS, q_ref)
        head_norm(seg(OFF_QX, X_WIDTH), gx_ref[...], X_HEADS, qx_ref)
        zk = seg(OFF_K, SWA_KV_WIDTH)
        zv = seg(OFF_V, SWA_KV_WIDTH)
        vb_ref[rows, :] = zv.astype(BF16)
        for hh in range(SWA_KV_HEADS):
            sl = slice(hh * HEAD_DIM, (hh + 1) * HEAD_DIM)
            head_rows = pl.ds(r0 * SWA_KV_HEADS + hh, r1 - r0, stride=SWA_KV_HEADS)
            k = _rms(zk[:, sl], gk_ref[...])
            k_ref[head_rows, :] = k
            v_ref[head_rows, :] = zv[:, sl]
            kb_ref[rows, sl] = k.astype(BF16)


def _projections(x2d, g_mix, w_in, g_q, g_k, g_qx, q_dtype, round_weights=()):
    n = x2d.shape[0]
    tile = min(ROW_TILE, n)
    row = lambda i: (i, 0)
    outs = ((1, CONV_WIDTH, F32), (1, CONV_WIDTH, F32), (1, SWA_WIDTH, q_dtype),
            (SWA_KV_HEADS, HEAD_DIM, F32), (SWA_KV_HEADS, HEAD_DIM, F32),
            (1, SWA_KV_WIDTH, BF16), (1, SWA_KV_WIDTH, BF16), (1, X_WIDTH, q_dtype))
    round_specs, round_shapes = _round_specs(round_weights, n // tile)
    return pl.pallas_call(
        functools.partial(_proj_kernel, len(round_weights)),
        grid=(n // tile,),
        in_specs=[pl.BlockSpec((tile, D_MODEL), row), _resident((1, D_MODEL)), _resident((D_MODEL, IN_WIDTH)),
                  _resident((1, HEAD_DIM)), _resident((1, HEAD_DIM)), _resident((1, HEAD_DIM)), *round_specs],
        out_specs=(*(pl.BlockSpec((tile * r, w), row) for r, w, _ in outs), *round_specs),
        out_shape=(*(jax.ShapeDtypeStruct((n * r, w), dt) for r, w, dt in outs), *round_shapes),
        compiler_params=_params("arbitrary"),
        name="proj",
    )(x2d, g_mix, w_in, g_q, g_k, g_qx, *round_weights)


def _softmax_parts(s, sink=None):
    m = jnp.max(s, axis=-1, keepdims=True)
    if sink is not None:
        m = jnp.maximum(m, sink)
    p = jnp.exp2(s - m)
    den = jnp.sum(p, axis=-1, keepdims=True)
    if sink is not None:
        den = den + jnp.exp2(sink - m)
    return p, 1.0 / den


def _mix_out_kernel(sink_ref, q_ref, kc_ref, kp_ref, vc_ref, vp_ref, qx_ref, mk_ref, mv_ref,
                    b_ref, uc_ref, up_ref, cw_ref, bias_ref, band_ref, x_ref, w_ref, g_ref, x1_ref, h_ref):
    has_prev = pl.program_id(1) > 0
    rows = MIX_BLOCKS * WINDOW
    blocks = [slice(j * WINDOW, (j + 1) * WINDOW) for j in range(MIX_BLOCKS)]

    u = uc_ref[...]
    prev = jnp.where(has_prev, up_ref[...], 0.0)
    ext = jnp.concatenate([prev, u], axis=0)
    cw = cw_ref[...]
    conv = cw[0:1] * ext[SUBLANES - 2:SUBLANES - 2 + rows]
    conv = conv + cw[1:2] * ext[SUBLANES - 1:SUBLANES - 1 + rows]
    conv = conv + cw[2:3] * u
    y_cols = [(b_ref[...] * conv).astype(BF16)]

    for h in range(SWA_KV_HEADS):
        ksl = slice(h * HEAD_DIM, (h + 1) * HEAD_DIM)
        outs = [[None] * MIX_BLOCKS for _ in range(SWA_GROUP)]
        for j, rsl in enumerate(blocks):
            min_band = jnp.where(has_prev, 0.5, 1.5) if j == 0 else 0.5
            if j == 0:
                k_all = jnp.concatenate([kp_ref[:, ksl], kc_ref[rsl, ksl]], axis=0)
                v_all = jnp.concatenate([vp_ref[:, ksl], vc_ref[rsl, ksl]], axis=0)
            else:
                k_all = kc_ref[(j - 1) * WINDOW:(j + 1) * WINDOW, ksl]
                v_all = vc_ref[(j - 1) * WINDOW:(j + 1) * WINDOW, ksl]
            for g in range(SWA_GROUP):
                hh = h * SWA_GROUP + g
                q = q_ref[rsl, hh * HEAD_DIM:(hh + 1) * HEAD_DIM]
                s = lax.dot_general(q, k_all, NT_DIMS, preferred_element_type=F32) * SCALE_LOG2
                s = jnp.where(band_ref[...] > min_band, s + bias_ref[h, g * WINDOW:(g + 1) * WINDOW, :], NEG)
                p, inv = _softmax_parts(s, sink_ref[hh] * LOG2E)
                outs[g][j] = (jnp.dot(p.astype(BF16), v_all, preferred_element_type=F32) * inv).astype(BF16)
        y_cols += [jnp.concatenate(o, axis=0) for o in outs]

    outs = [[None] * MIX_BLOCKS for _ in range(X_HEADS)]
    for j, rsl in enumerate(blocks):
        for hx in range(X_HEADS):
            sl = slice(hx * HEAD_DIM, (hx + 1) * HEAD_DIM)
            s = lax.dot_general(qx_ref[rsl, sl], mk_ref[:, sl], NT_DIMS, preferred_element_type=F32) * SCALE_LOG2
            p, inv = _softmax_parts(s)
            outs[hx][j] = (jnp.dot(p.astype(BF16), mv_ref[:, sl], preferred_element_type=F32) * inv).astype(BF16)
    y_cols += [jnp.concatenate(o, axis=0) for o in outs]

    x1 = x_ref[...] + jnp.dot(jnp.concatenate(y_cols, axis=1), w_ref[...], preferred_element_type=F32)
    x1_ref[...] = x1
    h_ref[...] = _rms(x1, g_ref[...]).astype(BF16)


def _mix_out_prompt(sinks, q, kb, vb, qx, mkb, mvb, b, u, conv_w, bias_p, x2d, w_out, g_mlp, bsz, seq):
    rows = MIX_BLOCKS * WINDOW
    steps = seq // rows
    cur = lambda bi, i: (bi * steps + i, 0)
    prv = lambda bi, i: (jnp.maximum((bi * steps + i) * MIX_BLOCKS - 1, 0), 0)
    prv8 = lambda bi, i: (jnp.maximum((bi * steps + i) * (rows // SUBLANES) - 1, 0), 0)
    per_b = lambda bi, i: (bi, 0)
    dist = WINDOW + np.arange(WINDOW)[:, None] - np.arange(2 * WINDOW)[None, :]
    band = np.where((dist >= 0) & (dist < WINDOW), np.where(np.arange(2 * WINDOW)[None, :] < WINDOW, 1.0, 2.0), 0.0)
    return pl.pallas_call(
        _mix_out_kernel,
        grid=(bsz, steps),
        in_specs=[pl.BlockSpec(memory_space=pltpu.SMEM),
                  pl.BlockSpec((rows, SWA_WIDTH), cur),
                  pl.BlockSpec((rows, SWA_KV_WIDTH), cur), pl.BlockSpec((WINDOW, SWA_KV_WIDTH), prv),
                  pl.BlockSpec((rows, SWA_KV_WIDTH), cur), pl.BlockSpec((WINDOW, SWA_KV_WIDTH), prv),
                  pl.BlockSpec((rows, X_WIDTH), cur),
                  pl.BlockSpec((MEM_TOKENS, X_WIDTH), per_b), pl.BlockSpec((MEM_TOKENS, X_WIDTH), per_b),
                  pl.BlockSpec((rows, CONV_WIDTH), cur), pl.BlockSpec((rows, CONV_WIDTH), cur),
                  pl.BlockSpec((SUBLANES, CONV_WIDTH), prv8),
                  _resident((CONV_K, CONV_WIDTH)),
                  _resident((SWA_KV_HEADS, SWA_GROUP * WINDOW, 2 * WINDOW)),
                  _resident((WINDOW, 2 * WINDOW)),
                  pl.BlockSpec((rows, D_MODEL), cur),
                  _resident((D_MODEL, D_MODEL)), _resident((1, D_MODEL))],
        out_specs=(pl.BlockSpec((rows, D_MODEL), cur), pl.BlockSpec((rows, D_MODEL), cur)),
        out_shape=(jax.ShapeDtypeStruct((bsz * seq, D_MODEL), F32), jax.ShapeDtypeStruct((bsz * seq, D_MODEL), BF16)),
        compiler_params=_params("arbitrary", "arbitrary"),
        name="mix_out",
    )(sinks, q, kb, kb, vb, vb, qx, mkb, mvb, b, u, u, conv_w, bias_p, jnp.asarray(band, F32), x2d, w_out, g_mlp)


def _mix_sample_kernel(t_len, sink_ref, q_ref, kn_ref, vn_ref, ck_ref, cv_ref, qx_ref, cmk_ref, cmv_ref,
                       b_ref, u_ref, cc_ref, cw_ref, bias_ref, x_ref, w_ref, g_ref,
                       x1_ref, h_ref, sk_ref, sv_ref, o_ref):
    nb = cc_ref.shape[0]
    n_cache = SWA_KV_HEADS * WINDOW
    n_new = SWA_KV_HEADS * t_len
    n_mem = X_HEADS * MEM_TOKENS
    log_t = int(math.log2(t_len))

    rows = SWA_Q_HEADS * t_len
    r = lax.broadcasted_iota(jnp.int32, (nb * rows, SAMPLE_KEY_ROWS), 0) & (rows - 1)
    c = lax.broadcasted_iota(jnp.int32, (nb * rows, SAMPLE_KEY_ROWS), 1)
    key_pos = jnp.where(c < n_cache, c >> 1, WINDOW + ((c - n_cache) >> 1))
    dist = (r & (t_len - 1)) + WINDOW - key_pos
    valid = (dist >= 0) & (dist < WINDOW) & ((c & (SWA_KV_HEADS - 1)) == (r >> int(math.log2(SWA_GROUP * t_len))))
    bias = jnp.concatenate([bias_ref[...]] * nb, axis=0)
    sink_col = jnp.concatenate([jnp.full((t_len, 1), sink_ref[hh], F32) for hh in range(SWA_Q_HEADS)] * nb, axis=0)
    xrows = X_HEADS * t_len
    xr = lax.broadcasted_iota(jnp.int32, (nb * xrows, n_mem), 0) & (xrows - 1)
    xc = lax.broadcasted_iota(jnp.int32, (nb * xrows, n_mem), 1)
    x_valid = (xc & (X_HEADS - 1)) == (xr >> log_t)
    cw = cw_ref[...]
    trow = lax.broadcasted_iota(jnp.int32, (t_len, CONV_WIDTH), 0)
    zeros_pad = jnp.zeros((SAMPLE_KEY_ROWS - n_cache - n_new, HEAD_DIM), F32)

    s_list, sx_list = [], []
    for bi in range(nb):
        rsl = slice(bi * t_len, (bi + 1) * t_len)
        c0, n0, m0 = bi * n_cache, bi * n_new, bi * n_mem

        for dst, cache, new in ((sk_ref, ck_ref, kn_ref), (sv_ref, cv_ref, vn_ref)):
            dst[c0:c0 + n_cache - n_new, :] = cache[c0 + n_new:c0 + n_cache, :]
            dst[c0 + n_cache - n_new:c0 + n_cache, :] = new[n0:n0 + n_new, :]

        u = u_ref[rsl, :]
        cc = cc_ref[bi]
        cc1 = jnp.broadcast_to(cc[1:2], u.shape)
        cc0 = jnp.broadcast_to(cc[0:1], u.shape)
        u_m1 = jnp.where(trow >= 1, pltpu.roll(u, 1, 0), cc1)
        u_m2 = jnp.where(trow >= 2, pltpu.roll(u, 2, 0), jnp.where(trow == 1, cc1, cc0))
        conv = cw[0:1] * u_m2
        conv = conv + cw[1:2] * u_m1
        conv = conv + cw[2:3] * u
        o_ref[rsl, 0:CONV_WIDTH] = (b_ref[rsl, :] * conv).astype(o_ref.dtype)

        qb = q_ref[rsl, :]
        q_rows = jnp.concatenate([qb[:, hh * HEAD_DIM:(hh + 1) * HEAD_DIM] for hh in range(SWA_Q_HEADS)], axis=0)
        k_all = jnp.concatenate([ck_ref[c0:c0 + n_cache, :], kn_ref[n0:n0 + n_new, :], zeros_pad], axis=0)
        s_list.append(lax.dot_general(q_rows.astype(BF16), k_all.astype(BF16), NT_DIMS, preferred_element_type=F32))
        qxb = qx_ref[rsl, :]
        qx_rows = jnp.concatenate([qxb[:, hx * HEAD_DIM:(hx + 1) * HEAD_DIM] for hx in range(X_HEADS)], axis=0)
        mk = cmk_ref[m0:m0 + n_mem, :].astype(BF16)
        sx_list.append(lax.dot_general(qx_rows.astype(BF16), mk, NT_DIMS, preferred_element_type=F32))

    s = jnp.concatenate(s_list, axis=0) * SCALE_LOG2
    w, inv = _softmax_parts(jnp.where(valid, s + bias, NEG), sink_col * LOG2E)
    w = w.astype(BF16)
    sx = jnp.concatenate(sx_list, axis=0) * SCALE_LOG2
    wx, invx = _softmax_parts(jnp.where(x_valid, sx, NEG))
    wx = wx.astype(BF16)

    for bi in range(nb):
        rsl = slice(bi * t_len, (bi + 1) * t_len)
        c0, n0, m0 = bi * n_cache, bi * n_new, bi * n_mem
        v_all = jnp.concatenate([cv_ref[c0:c0 + n_cache, :], vn_ref[n0:n0 + n_new, :], zeros_pad], axis=0)
        srows = slice(bi * rows, (bi + 1) * rows)
        o = jnp.dot(w[srows], v_all.astype(BF16), preferred_element_type=F32) * inv[srows]
        for hh in range(SWA_Q_HEADS):
            col = CONV_WIDTH + hh * HEAD_DIM
            o_ref[rsl, col:col + HEAD_DIM] = o[hh * t_len:(hh + 1) * t_len].astype(o_ref.dtype)
        mv = cmv_ref[m0:m0 + n_mem, :].astype(BF16)
        xsl = slice(bi * xrows, (bi + 1) * xrows)
        ox = jnp.dot(wx[xsl], mv, preferred_element_type=F32) * invx[xsl]
        for hx in range(X_HEADS):
            col = CONV_WIDTH + SWA_WIDTH + hx * HEAD_DIM
            o_ref[rsl, col:col + HEAD_DIM] = ox[hx * t_len:(hx + 1) * t_len].astype(o_ref.dtype)

    x1 = x_ref[...] + jnp.dot(o_ref[...].astype(BF16), w_ref[...], preferred_element_type=F32)
    x1_ref[...] = x1
    h_ref[...] = _rms(x1, g_ref[...]).astype(BF16)


def _mix_out_sample(sinks, q, k_rows, v_rows, cache_k, cache_v, qx, cache_mk, cache_mv, b, u, cache_conv, conv_w,
                    bias_s, x2d, w_out, g_mlp, bsz, t_len):
    assert t_len == SUBLANES and bsz % SAMPLE_BATCH_TILE == 0 and SWA_KV_HEADS == 2
    nb = SAMPLE_BATCH_TILE
    n_cache, n_new, n_mem = SWA_KV_HEADS * WINDOW, SWA_KV_HEADS * t_len, X_HEADS * MEM_TOKENS
    row = lambda i: (i, 0)
    rows_of = lambda n, w: pl.BlockSpec((nb * n, w), row)
    return pl.pallas_call(
        functools.partial(_mix_sample_kernel, t_len),
        grid=(bsz // nb,),
        in_specs=[pl.BlockSpec(memory_space=pltpu.SMEM),
                  rows_of(t_len, SWA_WIDTH),
                  rows_of(n_new, HEAD_DIM), rows_of(n_new, HEAD_DIM),
                  rows_of(n_cache, HEAD_DIM), rows_of(n_cache, HEAD_DIM),
                  rows_of(t_len, X_WIDTH),
                  rows_of(n_mem, HEAD_DIM), rows_of(n_mem, HEAD_DIM),
                  rows_of(t_len, CONV_WIDTH), rows_of(t_len, CONV_WIDTH),
                  pl.BlockSpec((nb, CONV_K - 1, CONV_WIDTH), lambda i: (i, 0, 0)),
                  _resident((CONV_K, CONV_WIDTH)),
                  _resident((SWA_Q_HEADS * t_len, SAMPLE_KEY_ROWS)),
                  rows_of(t_len, D_MODEL), _resident((D_MODEL, D_MODEL)), _resident((1, D_MODEL))],
        out_specs=(rows_of(t_len, D_MODEL), rows_of(t_len, D_MODEL),
                   rows_of(n_cache, HEAD_DIM), rows_of(n_cache, HEAD_DIM)),
        out_shape=(jax.ShapeDtypeStruct((bsz * t_len, D_MODEL), F32),
                   jax.ShapeDtypeStruct((bsz * t_len, D_MODEL), BF16),
                   jax.ShapeDtypeStruct((bsz * n_cache, HEAD_DIM), F32),
                   jax.ShapeDtypeStruct((bsz * n_cache, HEAD_DIM), F32)),
        scratch_shapes=[pltpu.VMEM((nb * t_len, D_MODEL), F32)],
        compiler_params=_params("arbitrary"),
        name="mix_out_sample",
    )(sinks, q, k_rows, v_rows, cache_k, cache_v, qx, cache_mk, cache_mv, b, u, cache_conv, conv_w, bias_s,
      x2d, w_out, g_mlp)


def _mlp_kernel(x1_ref, h_ref, wu_ref, wd_ref, o_ref):
    @pl.when(pl.program_id(1) == 0)
    def _():
        o_ref[...] = x1_ref[...]

    a = jnp.maximum(jnp.dot(h_ref[...], wu_ref[...], preferred_element_type=F32), 0.0)
    o_ref[...] += jnp.dot((a * a).astype(BF16), wd_ref[...], preferred_element_type=F32)


def _mlp_cast_kernel(x1_hbm, h_ref, wu_ref, wd_ref, o_ref, wub_ref, wdb_ref, sem):
    j = pl.program_id(1)
    residual_copy = pltpu.make_async_copy(x1_hbm, o_ref, sem)

    @pl.when(j == 0)
    def _():
        residual_copy.start()

    wu = wu_ref[...].astype(BF16)
    wub_ref[...] = wu
    a = jnp.maximum(jnp.dot(h_ref[...], wu, preferred_element_type=F32), 0.0)
    a = (a * a).astype(BF16)

    @pl.when(j == 0)
    def _():
        residual_copy.wait()

    for c0 in range(0, D_MODEL, MLP_CAST_FF_TILE):
        cols = slice(c0, c0 + MLP_CAST_FF_TILE)
        wd = wd_ref[:, cols].astype(BF16)
        wdb_ref[:, cols] = wd
        o_ref[:, cols] += jnp.dot(a, wd, preferred_element_type=F32)


def _mlp(x1, h, w_up, w_down):
    n = x1.shape[0]
    tile = min(MLP_ROW_TILE, n)
    rows = pl.BlockSpec((tile, D_MODEL), lambda i, j: (i, 0))
    return pl.pallas_call(
        _mlp_kernel,
        grid=(n // tile, D_FF // MLP_FF_TILE),
        in_specs=[rows, rows,
                  pl.BlockSpec((D_MODEL, MLP_FF_TILE), lambda i, j: (0, j)),
                  pl.BlockSpec((MLP_FF_TILE, D_MODEL), lambda i, j: (j, 0))],
        out_specs=rows,
        out_shape=jax.ShapeDtypeStruct((n, D_MODEL), F32),
        compiler_params=_params("arbitrary", "arbitrary", vmem_limit=VMEM_LIMIT_MLP_V7X),
        name="mlp",
    )(x1, h, w_up, w_down)


def _mlp_cast(x1, h, w_up, w_down):
    n = x1.shape[0]
    rows = pl.BlockSpec((n, D_MODEL), lambda i, j: (0, 0))
    up_spec = pl.BlockSpec((D_MODEL, MLP_CAST_FF_TILE), lambda i, j: (0, j))
    down_spec = pl.BlockSpec((MLP_CAST_FF_TILE, D_MODEL), lambda i, j: (j, 0))
    return pl.pallas_call(
        _mlp_cast_kernel,
        grid=(1, D_FF // MLP_CAST_FF_TILE),
        in_specs=[pl.BlockSpec(memory_space=pl.ANY), rows, up_spec, down_spec],
        out_specs=(rows, up_spec, down_spec),
        out_shape=(jax.ShapeDtypeStruct((n, D_MODEL), F32),
                   jax.ShapeDtypeStruct(w_up.shape, BF16), jax.ShapeDtypeStruct(w_down.shape, BF16)),
        scratch_shapes=[pltpu.SemaphoreType.DMA(())],
        compiler_params=_params("arbitrary", "arbitrary"),
        name="mlp_cast",
    )(x1, h, w_up, w_down)


def kernel(x_prompt, x_sample, mem_prompt, cache_conv, cache_swa_k, cache_swa_v, cache_mem_k, cache_mem_v,
           rel_bias_table, g_mix, w_in, conv_w, g_q_swa, g_k_swa, sinks, g_q_x, g_k_x, g_mem,
           w_mem_k, w_mem_v, w_out, g_mlp, w_up, w_down):
    depth = w_in.shape[0]
    bsz, seq, _ = x_prompt.shape
    dbsz, t_len, _ = x_sample.shape
    xp = x_prompt.reshape(bsz * seq, D_MODEL)
    xs = x_sample.reshape(dbsz * t_len, D_MODEL)
    mem2d = mem_prompt.reshape(bsz * MEM_TOKENS, D_MODEL)
    bias_p, bias_s = _bias_tables(rel_bias_table, t_len)

    outs = [[] for _ in range(8)]
    for l in range(depth):
        vec = lambda a: a[l].reshape(1, -1)
        mk, mv, mkb, mvb, wi = _memory_kv(mem2d, vec(g_mem), w_mem_k[l], w_mem_v[l], vec(g_k_x),
                                          round_weights=(w_in[l],))
        proj = functools.partial(_projections, g_mix=vec(g_mix), w_in=wi, g_q=vec(g_q_swa), g_k=vec(g_k_swa),
                                 g_qx=vec(g_q_x))
        pb, pu, pq, pk, pv, pkb, pvb, pqx, wo = proj(xp, q_dtype=BF16, round_weights=(w_out[l],))

        head_rows = lambda a: a.reshape(-1, HEAD_DIM)
        b, u, q, k, v, _, _, qx = proj(xs, q_dtype=F32)
        x1, h, sk, sv = _mix_out_sample(
            sinks[l], q, k, v, head_rows(cache_swa_k[l]), head_rows(cache_swa_v[l]),
            qx, head_rows(cache_mem_k[l]), head_rows(cache_mem_v[l]),
            b, u, cache_conv[l], conv_w[l], bias_s, xs, wo, vec(g_mlp), dbsz, t_len)
        xs, wu, wd = _mlp_cast(x1, h, w_up[l], w_down[l])
        outs[5].append(u.reshape(dbsz, t_len, CONV_WIDTH)[:, t_len - (CONV_K - 1):])
        outs[6].append(sk.reshape(dbsz, WINDOW, SWA_KV_HEADS, HEAD_DIM))
        outs[7].append(sv.reshape(dbsz, WINDOW, SWA_KV_HEADS, HEAD_DIM))

        x1, h = _mix_out_prompt(sinks[l], pq, pkb, pvb, pqx, mkb, mvb, pb, pu, conv_w[l], bias_p, xp, wo,
                                vec(g_mlp), bsz, seq)
        xp = _mlp(x1, h, wu, wd)
        outs[0].append(pu.reshape(bsz, seq, CONV_WIDTH)[:, seq - (CONV_K - 1):])
        last_window = lambda a: a.reshape(bsz, seq, SWA_KV_HEADS, HEAD_DIM)[:, seq - WINDOW:]
        outs[1].append(last_window(pk))
        outs[2].append(last_window(pv))
        outs[3].append(mk.reshape(bsz, MEM_TOKENS, X_HEADS, HEAD_DIM))
        outs[4].append(mv.reshape(bsz, MEM_TOKENS, X_HEADS, HEAD_DIM))

    return (xp.reshape(bsz, seq, D_MODEL), xs.reshape(dbsz, t_len, D_MODEL)) + tuple(jnp.stack(o) for o in outs)
```

```python
import functools
import math

import numpy as np
import jax
import jax.numpy as jnp
from jax import lax
from jax.experimental import pallas as pl
from jax.experimental.pallas import tpu as pltpu

D_MODEL = 2048
HEAD_DIM = 128
SWA_Q_HEADS = 8
SWA_KV_HEADS = 2
SWA_GROUP = SWA_Q_HEADS // SWA_KV_HEADS
SWA_WIDTH = SWA_Q_HEADS * HEAD_DIM
SWA_KV_WIDTH = SWA_KV_HEADS * HEAD_DIM
X_HEADS = 4
X_WIDTH = X_HEADS * HEAD_DIM
CONV_WIDTH = D_MODEL - SWA_WIDTH - X_WIDTH
CONV_K = 3
WINDOW = 128
NUM_BUCKETS = 32
MAX_DISTANCE = WINDOW
MEM_TOKENS = 256
D_FF = 4 * D_MODEL
EPS = 1e-6
NEG = -1e30
SCALE = HEAD_DIM ** -0.5
LOG2E = math.log2(math.e)
SCALE_LOG2 = SCALE * LOG2E

OFF_B = 0
OFF_C = CONV_WIDTH
OFF_H = 2 * CONV_WIDTH
OFF_Q = 3 * CONV_WIDTH
OFF_K = OFF_Q + SWA_WIDTH
OFF_V = OFF_K + SWA_KV_WIDTH
OFF_QX = OFF_V + SWA_KV_WIDTH
IN_WIDTH = OFF_QX + X_WIDTH

VMEM_LIMIT_V7X = 56 * 1024 * 1024
VMEM_LIMIT_MLP_V7X = 62 * 1024 * 1024
SUBLANES = 8

ROW_TILE = 512
ROW_CHUNK = 256
MLP_ROW_TILE = 512
MLP_FF_TILE = 2048
MLP_CAST_FF_TILE = 512
SAMPLE_BATCH_TILE = 8
MIX_BLOCKS = 4

BF16 = jnp.bfloat16
F32 = jnp.float32
NT_DIMS = (((1,), (1,)), ((), ()))


def _params(*sem, vmem_limit=VMEM_LIMIT_V7X):
    return pltpu.CompilerParams(dimension_semantics=sem, vmem_limit_bytes=vmem_limit)


def _resident(shape):
    nd = len(shape)
    return pl.BlockSpec(shape, lambda *_: (0,) * nd, pipeline_mode=pl.Buffered(1))


def _rms(x, g):
    ms = jnp.mean(x * x, axis=-1, keepdims=True)
    return x * lax.rsqrt(ms + EPS) * g


def _rel_bucket_np(dist):
    n = np.maximum(dist, 0)
    max_exact = NUM_BUCKETS // 2
    nf = np.maximum(n, 1).astype(np.float32)
    large = max_exact + (np.log(nf / np.float32(max_exact)) / np.float32(math.log(MAX_DISTANCE / max_exact))
                         * np.float32(NUM_BUCKETS - max_exact)).astype(np.int32)
    large = np.minimum(large, NUM_BUCKETS - 1)
    return np.where(n < max_exact, n, large).astype(np.int32)


SAMPLE_KEY_ROWS = 3 * WINDOW


def _sample_key_positions(t_len):
    c = np.arange(SAMPLE_KEY_ROWS)
    n_cache = SWA_KV_HEADS * WINDOW
    assert n_cache + SWA_KV_HEADS * t_len <= SAMPLE_KEY_ROWS
    return np.where(c < n_cache, c // SWA_KV_HEADS, WINDOW + (c - n_cache) // SWA_KV_HEADS)


def _bias_kernel(tab_ref, bp_ref, bs_ref, op_ref, os_ref):
    bp = bp_ref[...]
    bs = bs_ref[...]
    for hh in range(SWA_Q_HEADS):
        accp = jnp.zeros(bp.shape, F32)
        accs = jnp.zeros(bs.shape, F32)
        for k in range(NUM_BUCKETS):
            t = tab_ref[k * SWA_Q_HEADS + hh] * LOG2E
            accp = jnp.where(bp == k, t, accp)
            accs = jnp.where(bs == k, t, accs)
        h, g = divmod(hh, SWA_GROUP)
        op_ref[h, g * WINDOW:(g + 1) * WINDOW, :] = accp
        os_ref[hh * SUBLANES:(hh + 1) * SUBLANES, :] = accs


def _bias_tables(table, t_len):
    qi = np.arange(WINDOW)[:, None]
    kj = np.arange(2 * WINDOW)[None, :]
    bkt_p = _rel_bucket_np(WINDOW + qi - kj)
    key_pos = _sample_key_positions(t_len)
    bkt_s = _rel_bucket_np(np.arange(t_len)[:, None] + WINDOW - key_pos[None, :])
    return pl.pallas_call(
        _bias_kernel,
        out_shape=(jax.ShapeDtypeStruct((SWA_KV_HEADS, SWA_GROUP * WINDOW, 2 * WINDOW), F32),
                   jax.ShapeDtypeStruct((SWA_Q_HEADS * t_len, SAMPLE_KEY_ROWS), F32)),
        in_specs=[pl.BlockSpec(memory_space=pltpu.SMEM),
                  pl.BlockSpec(memory_space=pltpu.VMEM),
                  pl.BlockSpec(memory_space=pltpu.VMEM)],
        out_specs=(pl.BlockSpec(memory_space=pltpu.VMEM), pl.BlockSpec(memory_space=pltpu.VMEM)),
        name="bias",
    )(table.reshape(-1), jnp.asarray(bkt_p), jnp.asarray(bkt_s))


def _round_specs(weights, steps):
    specs = [pl.BlockSpec((w.shape[0] // steps, w.shape[1]), lambda i: (i, 0)) for w in weights]
    shapes = [jax.ShapeDtypeStruct(w.shape, BF16) for w in weights]
    return specs, shapes


def _round_slabs(srcs, dsts):
    for src, dst in zip(srcs, dsts):
        dst[...] = src[...].astype(BF16)


def _memkv_kernel(n_round, m_ref, g_ref, wk_ref, wv_ref, gk_ref, *refs):
    mk_ref, mv_ref, mkb_ref, mvb_ref = refs[n_round:n_round + 4]
    _round_slabs(refs[:n_round], refs[n_round + 4:])
    h = _rms(m_ref[...], g_ref[...]).astype(BF16)
    zk = jnp.dot(h, wk_ref[...].astype(BF16), preferred_element_type=F32)
    zv = jnp.dot(h, wv_ref[...].astype(BF16), preferred_element_type=F32)
    gk = gk_ref[...]
    tokens = m_ref.shape[0]
    for hx in range(X_HEADS):
        sl = slice(hx * HEAD_DIM, (hx + 1) * HEAD_DIM)
        head_rows = pl.ds(hx, tokens, stride=X_HEADS)
        mk = _rms(zk[:, sl], gk)
        mk_ref[head_rows, :] = mk
        mv_ref[head_rows, :] = zv[:, sl]
        mkb_ref[:, sl] = mk.astype(BF16)
    mvb_ref[...] = zv.astype(BF16)


def _memory_kv(mem2d, g_mem, wk, wv, g_k_x, round_weights=()):
    n = mem2d.shape[0]
    tile = MEM_TOKENS
    row = lambda i: (i, 0)
    head_rows = pl.BlockSpec((tile * X_HEADS, HEAD_DIM), row)
    round_specs, round_shapes = _round_specs(round_weights, n // tile)
    return pl.pallas_call(
        functools.partial(_memkv_kernel, len(round_weights)),
        grid=(n // tile,),
        in_specs=[pl.BlockSpec((tile, D_MODEL), row), _resident((1, D_MODEL)),
                  _resident((D_MODEL, X_WIDTH)), _resident((D_MODEL, X_WIDTH)), _resident((1, HEAD_DIM)),
                  *round_specs],
        out_specs=(head_rows, head_rows, pl.BlockSpec((tile, X_WIDTH), row), pl.BlockSpec((tile, X_WIDTH), row),
                   *round_specs),
        out_shape=(jax.ShapeDtypeStruct((n * X_HEADS, HEAD_DIM), F32),
                   jax.ShapeDtypeStruct((n * X_HEADS, HEAD_DIM), F32),
                   jax.ShapeDtypeStruct((n, X_WIDTH), BF16), jax.ShapeDtypeStruct((n, X_WIDTH), BF16),
                   *round_shapes),
        compiler_params=_params("arbitrary"),
        name="memkv",
    )(mem2d, g_mem, wk, wv, g_k_x, *round_weights)


def _proj_kernel(n_round, x_ref, g_ref, w_ref, gq_ref, gk_ref, gx_ref, *refs):
    b_ref, u_ref, q_ref, k_ref, v_ref, kb_ref, vb_ref, qx_ref = refs[n_round:n_round + 8]
    _round_slabs(refs[:n_round], refs[n_round + 8:])
    for r0 in range(0, x_ref.shape[0], ROW_CHUNK):
        r1 = r0 + ROW_CHUNK
        rows = slice(r0, r1)
        h = _rms(x_ref[rows, :], g_ref[...]).astype(BF16)

        def seg(lo, width):
            return jnp.dot(h, w_ref[:, lo:lo + width], preferred_element_type=F32)

        b_ref[rows, :] = seg(OFF_B, CONV_WIDTH)
        u_ref[rows, :] = seg(OFF_C, CONV_WIDTH) * seg(OFF_H, CONV_WIDTH)

        def head_norm(z, g, n_heads, out):
            for hh in range(n_heads):
                sl = slice(hh * HEAD_DIM, (hh + 1) * HEAD_DIM)
                out[rows, sl] = _rms(z[:, sl], g).astype(out.dtype)

        head_norm(seg(OFF_Q, SWA_WIDTH), gq_ref[...], SWA_Q_HEADS, q_ref)
        head_norm(seg(OFF_QX, X_WIDTH), gx_ref[...], X_HEADS, qx_ref)
        zk = seg(OFF_K, SWA_KV_WIDTH)
        zv = seg(OFF_V, SWA_KV_WIDTH)
        vb_ref[rows, :] = zv.astype(BF16)
        for hh in range(SWA_KV_HEADS):
            sl = slice(hh * HEAD_DIM, (hh + 1) * HEAD_DIM)
            head_rows = pl.ds(r0 * SWA_KV_HEADS + hh, r1 - r0, stride=SWA_KV_HEADS)
            k = _rms(zk[:, sl], gk_ref[...])
            k_ref[head_rows, :] = k
            v_ref[head_rows, :] = zv[:, sl]
            kb_ref[rows, sl] = k.astype(BF16)


def _projections(x2d, g_mix, w_in, g_q, g_k, g_qx, q_dtype, round_weights=()):
    n = x2d.shape[0]
    tile = min(ROW_TILE, n)
    row = lambda i: (i, 0)
    outs = ((1, CONV_WIDTH, F32), (1, CONV_WIDTH, F32), (1, SWA_WIDTH, q_dtype),
            (SWA_KV_HEADS, HEAD_DIM, F32), (SWA_KV_HEADS, HEAD_DIM, F32),
            (1, SWA_KV_WIDTH, BF16), (1, SWA_KV_WIDTH, BF16), (1, X_WIDTH, q_dtype))
    round_specs, round_shapes = _round_specs(round_weights, n // tile)
    return pl.pallas_call(
        functools.partial(_proj_kernel, len(round_weights)),
        grid=(n // tile,),
        in_specs=[pl.BlockSpec((tile, D_MODEL), row), _resident((1, D_MODEL)), _resident((D_MODEL, IN_WIDTH)),
                  _resident((1, HEAD_DIM)), _resident((1, HEAD_DIM)), _resident((1, HEAD_DIM)), *round_specs],
        out_specs=(*(pl.BlockSpec((tile * r, w), row) for r, w, _ in outs), *round_specs),
        out_shape=(*(jax.ShapeDtypeStruct((n * r, w), dt) for r, w, dt in outs), *round_shapes),
        compiler_params=_params("arbitrary"),
        name="proj",
    )(x2d, g_mix, w_in, g_q, g_k, g_qx, *round_weights)


def _softmax_parts(s, sink=None):
    m = jnp.max(s, axis=-1, keepdims=True)
    if sink is not None:
        m = jnp.maximum(m, sink)
    p = jnp.exp2(s - m)
    den = jnp.sum(p, axis=-1, keepdims=True)
    if sink is not None:
        den = den + jnp.exp2(sink - m)
    return p, 1.0 / den


def _mix_out_kernel(sink_ref, q_ref, kc_ref, kp_ref, vc_ref, vp_ref, qx_ref, mk_ref, mv_ref,
                    b_ref, uc_ref, up_ref, cw_ref, bias_ref, band_ref, x_ref, w_ref, g_ref, x1_ref, h_ref):
    has_prev = pl.program_id(1) > 0
    rows = MIX_BLOCKS * WINDOW
    blocks = [slice(j * WINDOW, (j + 1) * WINDOW) for j in range(MIX_BLOCKS)]

    u = uc_ref[...]
    prev = jnp.where(has_prev, up_ref[...], 0.0)
    ext = jnp.concatenate([prev, u], axis=0)
    cw = cw_ref[...]
    conv = cw[0:1] * ext[SUBLANES - 2:SUBLANES - 2 + rows]
    conv = conv + cw[1:2] * ext[SUBLANES - 1:SUBLANES - 1 + rows]
    conv = conv + cw[2:3] * u
    y_cols = [(b_ref[...] * conv).astype(BF16)]

    ones_cols = jnp.ones((2 * WINDOW, HEAD_DIM), BF16)

    def weighted_values(p, v_ones, sink=None, m=None):
        ov = jnp.dot(p.astype(BF16), v_ones, preferred_element_type=F32)
        den = ov[:, HEAD_DIM:]
        if sink is not None:
            den = den + jnp.exp2(sink - m)
        return (ov[:, :HEAD_DIM] * (1.0 / den)).astype(BF16)

    for h in range(SWA_KV_HEADS):
        ksl = slice(h * HEAD_DIM, (h + 1) * HEAD_DIM)
        outs = [[None] * MIX_BLOCKS for _ in range(SWA_GROUP)]
        for j, rsl in enumerate(blocks):
            min_band = jnp.where(has_prev, 0.5, 1.5) if j == 0 else 0.5
            if j == 0:
                k_all = jnp.concatenate([kp_ref[:, ksl], kc_ref[rsl, ksl]], axis=0)
                v_all = jnp.concatenate([vp_ref[:, ksl], vc_ref[rsl, ksl]], axis=0)
            else:
                k_all = kc_ref[(j - 1) * WINDOW:(j + 1) * WINDOW, ksl]
                v_all = vc_ref[(j - 1) * WINDOW:(j + 1) * WINDOW, ksl]
            v_ones = jnp.concatenate([v_all, ones_cols], axis=1)
            for g in range(SWA_GROUP):
                hh = h * SWA_GROUP + g
                q = q_ref[rsl, hh * HEAD_DIM:(hh + 1) * HEAD_DIM]
                s = lax.dot_general(q, k_all, NT_DIMS, preferred_element_type=F32) * SCALE_LOG2
                s = jnp.where(band_ref[...] > min_band, s + bias_ref[h, g * WINDOW:(g + 1) * WINDOW, :], NEG)
                sink = sink_ref[hh] * LOG2E
                m = jnp.maximum(jnp.max(s, axis=-1, keepdims=True), sink)
                outs[g][j] = weighted_values(jnp.exp2(s - m), v_ones, sink, m)
        y_cols += [jnp.concatenate(o, axis=0) for o in outs]

    outs = [[None] * MIX_BLOCKS for _ in range(X_HEADS)]
    for hx in range(X_HEADS):
        sl = slice(hx * HEAD_DIM, (hx + 1) * HEAD_DIM)
        mv_ones = jnp.concatenate([mv_ref[:, sl], ones_cols], axis=1)
        for j, rsl in enumerate(blocks):
            s = lax.dot_general(qx_ref[rsl, sl], mk_ref[:, sl], NT_DIMS, preferred_element_type=F32) * SCALE_LOG2
            m = jnp.max(s, axis=-1, keepdims=True)
            outs[hx][j] = weighted_values(jnp.exp2(s - m), mv_ones)
    y_cols += [jnp.concatenate(o, axis=0) for o in outs]

    x1 = x_ref[...] + jnp.dot(jnp.concatenate(y_cols, axis=1), w_ref[...], preferred_element_type=F32)
    x1_ref[...] = x1
    h_ref[...] = _rms(x1, g_ref[...]).astype(BF16)


def _mix_out_prompt(sinks, q, kb, vb, qx, mkb, mvb, b, u, conv_w, bias_p, x2d, w_out, g_mlp, bsz, seq):
    rows = MIX_BLOCKS * WINDOW
    steps = seq // rows
    cur = lambda bi, i: (bi * steps + i, 0)
    prv = lambda bi, i: (jnp.maximum((bi * steps + i) * MIX_BLOCKS - 1, 0), 0)
    prv8 = lambda bi, i: (jnp.maximum((bi * steps + i) * (rows // SUBLANES) - 1, 0), 0)
    per_b = lambda bi, i: (bi, 0)
    dist = WINDOW + np.arange(WINDOW)[:, None] - np.arange(2 * WINDOW)[None, :]
    band = np.where((dist >= 0) & (dist < WINDOW), np.where(np.arange(2 * WINDOW)[None, :] < WINDOW, 1.0, 2.0), 0.0)
    return pl.pallas_call(
        _mix_out_kernel,
        grid=(bsz, steps),
        in_specs=[pl.BlockSpec(memory_space=pltpu.SMEM),
                  pl.BlockSpec((rows, SWA_WIDTH), cur),
                  pl.BlockSpec((rows, SWA_KV_WIDTH), cur), pl.BlockSpec((WINDOW, SWA_KV_WIDTH), prv),
                  pl.BlockSpec((rows, SWA_KV_WIDTH), cur), pl.BlockSpec((WINDOW, SWA_KV_WIDTH), prv),
                  pl.BlockSpec((rows, X_WIDTH), cur),
                  pl.BlockSpec((MEM_TOKENS, X_WIDTH), per_b), pl.BlockSpec((MEM_TOKENS, X_WIDTH), per_b),
                  pl.BlockSpec((rows, CONV_WIDTH), cur), pl.BlockSpec((rows, CONV_WIDTH), cur),
                  pl.BlockSpec((SUBLANES, CONV_WIDTH), prv8),
                  _resident((CONV_K, CONV_WIDTH)),
                  _resident((SWA_KV_HEADS, SWA_GROUP * WINDOW, 2 * WINDOW)),
                  _resident((WINDOW, 2 * WINDOW)),
                  pl.BlockSpec((rows, D_MODEL), cur),
                  _resident((D_MODEL, D_MODEL)), _resident((1, D_MODEL))],
        out_specs=(pl.BlockSpec((rows, D_MODEL), cur), pl.BlockSpec((rows, D_MODEL), cur)),
        out_shape=(jax.ShapeDtypeStruct((bsz * seq, D_MODEL), F32), jax.ShapeDtypeStruct((bsz * seq, D_MODEL), BF16)),
        compiler_params=_params("arbitrary", "arbitrary"),
        name="mix_out",
    )(sinks, q, kb, kb, vb, vb, qx, mkb, mvb, b, u, u, conv_w, bias_p, jnp.asarray(band, F32), x2d, w_out, g_mlp)


def _mix_sample_kernel(t_len, sink_ref, q_ref, kn_ref, vn_ref, ck_ref, cv_ref, qx_ref, cmk_ref, cmv_ref,
                       b_ref, u_ref, cc_ref, cw_ref, bias_ref, x_ref, w_ref, g_ref,
                       x1_ref, h_ref, sk_ref, sv_ref, o_ref):
    nb = cc_ref.shape[0]
    n_cache = SWA_KV_HEADS * WINDOW
    n_new = SWA_KV_HEADS * t_len
    n_mem = X_HEADS * MEM_TOKENS
    log_t = int(math.log2(t_len))

    rows = SWA_Q_HEADS * t_len
    r = lax.broadcasted_iota(jnp.int32, (nb * rows, SAMPLE_KEY_ROWS), 0) & (rows - 1)
    c = lax.broadcasted_iota(jnp.int32, (nb * rows, SAMPLE_KEY_ROWS), 1)
    key_pos = jnp.where(c < n_cache, c >> 1, WINDOW + ((c - n_cache) >> 1))
    dist = (r & (t_len - 1)) + WINDOW - key_pos
    valid = (dist >= 0) & (dist < WINDOW) & ((c & (SWA_KV_HEADS - 1)) == (r >> int(math.log2(SWA_GROUP * t_len))))
    bias = jnp.concatenate([bias_ref[...]] * nb, axis=0)
    sink_col = jnp.concatenate([jnp.full((t_len, 1), sink_ref[hh], F32) for hh in range(SWA_Q_HEADS)] * nb, axis=0)
    xrows = X_HEADS * t_len
    xr = lax.broadcasted_iota(jnp.int32, (nb * xrows, n_mem), 0) & (xrows - 1)
    xc = lax.broadcasted_iota(jnp.int32, (nb * xrows, n_mem), 1)
    x_valid = (xc & (X_HEADS - 1)) == (xr >> log_t)
    cw = cw_ref[...]
    trow = lax.broadcasted_iota(jnp.int32, (t_len, CONV_WIDTH), 0)
    zeros_pad = jnp.zeros((SAMPLE_KEY_ROWS - n_cache - n_new, HEAD_DIM), F32)

    s_list, sx_list = [], []
    for bi in range(nb):
        rsl = slice(bi * t_len, (bi + 1) * t_len)
        c0, n0, m0 = bi * n_cache, bi * n_new, bi * n_mem

        for dst, cache, new in ((sk_ref, ck_ref, kn_ref), (sv_ref, cv_ref, vn_ref)):
            dst[c0:c0 + n_cache - n_new, :] = cache[c0 + n_new:c0 + n_cache, :]
            dst[c0 + n_cache - n_new:c0 + n_cache, :] = new[n0:n0 + n_new, :]

        u = u_ref[rsl, :]
        cc = cc_ref[bi]
        cc1 = jnp.broadcast_to(cc[1:2], u.shape)
        cc0 = jnp.broadcast_to(cc[0:1], u.shape)
        u_m1 = jnp.where(trow >= 1, pltpu.roll(u, 1, 0), cc1)
        u_m2 = jnp.where(trow >= 2, pltpu.roll(u, 2, 0), jnp.where(trow == 1, cc1, cc0))
        conv = cw[0:1] * u_m2
        conv = conv + cw[1:2] * u_m1
        conv = conv + cw[2:3] * u
        o_ref[rsl, 0:CONV_WIDTH] = (b_ref[rsl, :] * conv).astype(o_ref.dtype)

        qb = q_ref[rsl, :]
        q_rows = jnp.concatenate([qb[:, hh * HEAD_DIM:(hh + 1) * HEAD_DIM] for hh in range(SWA_Q_HEADS)], axis=0)
        k_all = jnp.concatenate([ck_ref[c0:c0 + n_cache, :], kn_ref[n0:n0 + n_new, :], zeros_pad], axis=0)
        s_list.append(lax.dot_general(q_rows.astype(BF16), k_all.astype(BF16), NT_DIMS, preferred_element_type=F32))
        qxb = qx_ref[rsl, :]
        qx_rows = jnp.concatenate([qxb[:, hx * HEAD_DIM:(hx + 1) * HEAD_DIM] for hx in range(X_HEADS)], axis=0)
        mk = cmk_ref[m0:m0 + n_mem, :].astype(BF16)
        sx_list.append(lax.dot_general(qx_rows.astype(BF16), mk, NT_DIMS, preferred_element_type=F32))

    s = jnp.concatenate(s_list, axis=0) * SCALE_LOG2
    w, inv = _softmax_parts(jnp.where(valid, s + bias, NEG), sink_col * LOG2E)
    w = w.astype(BF16)
    sx = jnp.concatenate(sx_list, axis=0) * SCALE_LOG2
    wx, invx = _softmax_parts(jnp.where(x_valid, sx, NEG))
    wx = wx.astype(BF16)

    for bi in range(nb):
        rsl = slice(bi * t_len, (bi + 1) * t_len)
        c0, n0, m0 = bi * n_cache, bi * n_new, bi * n_mem
        v_all = jnp.concatenate([cv_ref[c0:c0 + n_cache, :], vn_ref[n0:n0 + n_new, :], zeros_pad], axis=0)
        srows = slice(bi * rows, (bi + 1) * rows)
        o = jnp.dot(w[srows], v_all.astype(BF16), preferred_element_type=F32) * inv[srows]
        for hh in range(SWA_Q_HEADS):
            col = CONV_WIDTH + hh * HEAD_DIM
            o_ref[rsl, col:col + HEAD_DIM] = o[hh * t_len:(hh + 1) * t_len].astype(o_ref.dtype)
        mv = cmv_ref[m0:m0 + n_mem, :].astype(BF16)
        xsl = slice(bi * xrows, (bi + 1) * xrows)
        ox = jnp.dot(wx[xsl], mv, preferred_element_type=F32) * invx[xsl]
        for hx in range(X_HEADS):
            col = CONV_WIDTH + SWA_WIDTH + hx * HEAD_DIM
            o_ref[rsl, col:col + HEAD_DIM] = ox[hx * t_len:(hx + 1) * t_len].astype(o_ref.dtype)

    x1 = x_ref[...] + jnp.dot(o_ref[...].astype(BF16), w_ref[...], preferred_element_type=F32)
    x1_ref[...] = x1
    h_ref[...] = _rms(x1, g_ref[...]).astype(BF16)


def _mix_out_sample(sinks, q, k_rows, v_rows, cache_k, cache_v, qx, cache_mk, cache_mv, b, u, cache_conv, conv_w,
                    bias_s, x2d, w_out, g_mlp, bsz, t_len):
    assert t_len == SUBLANES and bsz % SAMPLE_BATCH_TILE == 0 and SWA_KV_HEADS == 2
    nb = SAMPLE_BATCH_TILE
    n_cache, n_new, n_mem = SWA_KV_HEADS * WINDOW, SWA_KV_HEADS * t_len, X_HEADS * MEM_TOKENS
    row = lambda i: (i, 0)
    rows_of = lambda n, w: pl.BlockSpec((nb * n, w), row)
    return pl.pallas_call(
        functools.partial(_mix_sample_kernel, t_len),
        grid=(bsz // nb,),
        in_specs=[pl.BlockSpec(memory_space=pltpu.SMEM),
                  rows_of(t_len, SWA_WIDTH),
                  rows_of(n_new, HEAD_DIM), rows_of(n_new, HEAD_DIM),
                  rows_of(n_cache, HEAD_DIM), rows_of(n_cache, HEAD_DIM),
                  rows_of(t_len, X_WIDTH),
                  rows_of(n_mem, HEAD_DIM), rows_of(n_mem, HEAD_DIM),
                  rows_of(t_len, CONV_WIDTH), rows_of(t_len, CONV_WIDTH),
                  pl.BlockSpec((nb, CONV_K - 1, CONV_WIDTH), lambda i: (i, 0, 0)),
                  _resident((CONV_K, CONV_WIDTH)),
                  _resident((SWA_Q_HEADS * t_len, SAMPLE_KEY_ROWS)),
                  rows_of(t_len, D_MODEL), _resident((D_MODEL, D_MODEL)), _resident((1, D_MODEL))],
        out_specs=(rows_of(t_len, D_MODEL), rows_of(t_len, D_MODEL),
                   rows_of(n_cache, HEAD_DIM), rows_of(n_cache, HEAD_DIM)),
        out_shape=(jax.ShapeDtypeStruct((bsz * t_len, D_MODEL), F32),
                   jax.ShapeDtypeStruct((bsz * t_len, D_MODEL), BF16),
                   jax.ShapeDtypeStruct((bsz * n_cache, HEAD_DIM), F32),
                   jax.ShapeDtypeStruct((bsz * n_cache, HEAD_DIM), F32)),
        scratch_shapes=[pltpu.VMEM((nb * t_len, D_MODEL), F32)],
        compiler_params=_params("arbitrary"),
        name="mix_out_sample",
    )(sinks, q, k_rows, v_rows, cache_k, cache_v, qx, cache_mk, cache_mv, b, u, cache_conv, conv_w, bias_s,
      x2d, w_out, g_mlp)


def _mlp_kernel(x1_ref, h_ref, wu_ref, wd_ref, o_ref):
    @pl.when(pl.program_id(1) == 0)
    def _():
        o_ref[...] = x1_ref[...]

    a = jnp.maximum(jnp.dot(h_ref[...], wu_ref[...], preferred_element_type=F32), 0.0)
    o_ref[...] += jnp.dot((a * a).astype(BF16), wd_ref[...], preferred_element_type=F32)


def _mlp_cast_kernel(x1_hbm, h_ref, wu_ref, wd_ref, o_ref, wub_ref, wdb_ref, sem):
    j = pl.program_id(1)
    residual_copy = pltpu.make_async_copy(x1_hbm, o_ref, sem)

    @pl.when(j == 0)
    def _():
        residual_copy.start()

    wu = wu_ref[...].astype(BF16)
    wub_ref[...] = wu
    a = jnp.maximum(jnp.dot(h_ref[...], wu, preferred_element_type=F32), 0.0)
    a = (a * a).astype(BF16)

    @pl.when(j == 0)
    def _():
        residual_copy.wait()

    for c0 in range(0, D_MODEL, MLP_CAST_FF_TILE):
        cols = slice(c0, c0 + MLP_CAST_FF_TILE)
        wd = wd_ref[:, cols].astype(BF16)
        wdb_ref[:, cols] = wd
        o_ref[:, cols] += jnp.dot(a, wd, preferred_element_type=F32)


def _mlp(x1, h, w_up, w_down):
    n = x1.shape[0]
    tile = min(MLP_ROW_TILE, n)
    rows = pl.BlockSpec((tile, D_MODEL), lambda i, j: (i, 0))
    return pl.pallas_call(
        _mlp_kernel,
        grid=(n // tile, D_FF // MLP_FF_TILE),
        in_specs=[rows, rows,
                  pl.BlockSpec((D_MODEL, MLP_FF_TILE), lambda i, j: (0, j)),
                  pl.BlockSpec((MLP_FF_TILE, D_MODEL), lambda i, j: (j, 0))],
        out_specs=rows,
        out_shape=jax.ShapeDtypeStruct((n, D_MODEL), F32),
        compiler_params=_params("arbitrary", "arbitrary", vmem_limit=VMEM_LIMIT_MLP_V7X),
        name="mlp",
    )(x1, h, w_up, w_down)


def _mlp_cast(x1, h, w_up, w_down):
    n = x1.shape[0]
    rows = pl.BlockSpec((n, D_MODEL), lambda i, j: (0, 0))
    up_spec = pl.BlockSpec((D_MODEL, MLP_CAST_FF_TILE), lambda i, j: (0, j))
    down_spec = pl.BlockSpec((MLP_CAST_FF_TILE, D_MODEL), lambda i, j: (j, 0))
    return pl.pallas_call(
        _mlp_cast_kernel,
        grid=(1, D_FF // MLP_CAST_FF_TILE),
        in_specs=[pl.BlockSpec(memory_space=pl.ANY), rows, up_spec, down_spec],
        out_specs=(rows, up_spec, down_spec),
        out_shape=(jax.ShapeDtypeStruct((n, D_MODEL), F32),
                   jax.ShapeDtypeStruct(w_up.shape, BF16), jax.ShapeDtypeStruct(w_down.shape, BF16)),
        scratch_shapes=[pltpu.SemaphoreType.DMA(())],
        compiler_params=_params("arbitrary", "arbitrary"),
        name="mlp_cast",
    )(x1, h, w_up, w_down)


def kernel(x_prompt, x_sample, mem_prompt, cache_conv, cache_swa_k, cache_swa_v, cache_mem_k, cache_mem_v,
           rel_bias_table, g_mix, w_in, conv_w, g_q_swa, g_k_swa, sinks, g_q_x, g_k_x, g_mem,
           w_mem_k, w_mem_v, w_out, g_mlp, w_up, w_down):
    depth = w_in.shape[0]
    bsz, seq, _ = x_prompt.shape
    dbsz, t_len, _ = x_sample.shape
    xp = x_prompt.reshape(bsz * seq, D_MODEL)
    xs = x_sample.reshape(dbsz * t_len, D_MODEL)
    mem2d = mem_prompt.reshape(bsz * MEM_TOKENS, D_MODEL)
    bias_p, bias_s = _bias_tables(rel_bias_table, t_len)

    outs = [[] for _ in range(8)]
    for l in range(depth):
        vec = lambda a: a[l].reshape(1, -1)
        mk, mv, mkb, mvb, wi = _memory_kv(mem2d, vec(g_mem), w_mem_k[l], w_mem_v[l], vec(g_k_x),
                                          round_weights=(w_in[l],))
        proj = functools.partial(_projections, g_mix=vec(g_mix), w_in=wi, g_q=vec(g_q_swa), g_k=vec(g_k_swa),
                                 g_qx=vec(g_q_x))
        pb, pu, pq, pk, pv, pkb, pvb, pqx, wo = proj(xp, q_dtype=BF16, round_weights=(w_out[l],))

        head_rows = lambda a: a.reshape(-1, HEAD_DIM)
        b, u, q, k, v, _, _, qx = proj(xs, q_dtype=F32)
        x1, h, sk, sv = _mix_out_sample(
            sinks[l], q, k, v, head_rows(cache_swa_k[l]), head_rows(cache_swa_v[l]),
            qx, head_rows(cache_mem_k[l]), head_rows(cache_mem_v[l]),
            b, u, cache_conv[l], conv_w[l], bias_s, xs, wo, vec(g_mlp), dbsz, t_len)
        xs, wu, wd = _mlp_cast(x1, h, w_up[l], w_down[l])
        outs[5].append(u.reshape(dbsz, t_len, CONV_WIDTH)[:, t_len - (CONV_K - 1):])
        outs[6].append(sk.reshape(dbsz, WINDOW, SWA_KV_HEADS, HEAD_DIM))
        outs[7].append(sv.reshape(dbsz, WINDOW, SWA_KV_HEADS, HEAD_DIM))

        x1, h = _mix_out_prompt(sinks[l], pq, pkb, pvb, pqx, mkb, mvb, pb, pu, conv_w[l], bias_p, xp, wo,
                                vec(g_mlp), bsz, seq)
        xp = _mlp(x1, h, wu, wd)
        outs[0].append(pu.reshape(bsz, seq, CONV_WIDTH)[:, seq - (CONV_K - 1):])
        last_window = lambda a: a.reshape(bsz, seq, SWA_KV_HEADS, HEAD_DIM)[:, seq - WINDOW:]
        outs[1].append(last_window(pk))
        outs[2].append(last_window(pv))
        outs[3].append(mk.reshape(bsz, MEM_TOKENS, X_HEADS, HEAD_DIM))
        outs[4].append(mv.reshape(bsz, MEM_TOKENS, X_HEADS, HEAD_DIM))

    return (xp.reshape(bsz, seq, D_MODEL), xs.reshape(dbsz, t_len, D_MODEL)) + tuple(jnp.stack(o) for o in outs)
```

```python
import functools
import math

import numpy as np
import jax
import jax.numpy as jnp
from jax import lax
from jax.experimental import pallas as pl
from jax.experimental.pallas import tpu as pltpu

D_MODEL = 2048
HEAD_DIM = 128
SWA_Q_HEADS = 8
SWA_KV_HEADS = 2
SWA_GROUP = SWA_Q_HEADS // SWA_KV_HEADS
SWA_WIDTH = SWA_Q_HEADS * HEAD_DIM
SWA_KV_WIDTH = SWA_KV_HEADS * HEAD_DIM
X_HEADS = 4
X_WIDTH = X_HEADS * HEAD_DIM
CONV_WIDTH = D_MODEL - SWA_WIDTH - X_WIDTH
CONV_K = 3
WINDOW = 128
NUM_BUCKETS = 32
MAX_DISTANCE = WINDOW
MEM_TOKENS = 256
D_FF = 4 * D_MODEL
EPS = 1e-6
NEG = -1e30
SCALE = HEAD_DIM ** -0.5
LOG2E = math.log2(math.e)
SCALE_LOG2 = SCALE * LOG2E

OFF_B = 0
OFF_C = CONV_WIDTH
OFF_H = 2 * CONV_WIDTH
OFF_Q = 3 * CONV_WIDTH
OFF_K = OFF_Q + SWA_WIDTH
OFF_V = OFF_K + SWA_KV_WIDTH
OFF_QX = OFF_V + SWA_KV_WIDTH
IN_WIDTH = OFF_QX + X_WIDTH

VMEM_LIMIT_V7X = 56 * 1024 * 1024
VMEM_LIMIT_MLP_V7X = 62 * 1024 * 1024
SUBLANES = 8

ROW_TILE = 512
ROW_CHUNK = 256
MLP_ROW_TILE = 512
MLP_FF_TILE = 2048
MLP_CAST_FF_TILE = 512
SAMPLE_BATCH_TILE = 8
MIX_BLOCKS = 4

BF16 = jnp.bfloat16
F32 = jnp.float32
NT_DIMS = (((1,), (1,)), ((), ()))


def _params(*sem, vmem_limit=VMEM_LIMIT_V7X):
    return pltpu.CompilerParams(dimension_semantics=sem, vmem_limit_bytes=vmem_limit)


def _resident(shape):
    nd = len(shape)
    return pl.BlockSpec(shape, lambda *_: (0,) * nd, pipeline_mode=pl.Buffered(1))


def _rms(x, g):
    ms = jnp.mean(x * x, axis=-1, keepdims=True)
    return x * lax.rsqrt(ms + EPS) * g


def _rel_bucket_np(dist):
    n = np.maximum(dist, 0)
    max_exact = NUM_BUCKETS // 2
    nf = np.maximum(n, 1).astype(np.float32)
    large = max_exact + (np.log(nf / np.float32(max_exact)) / np.float32(math.log(MAX_DISTANCE / max_exact))
                         * np.float32(NUM_BUCKETS - max_exact)).astype(np.int32)
    large = np.minimum(large, NUM_BUCKETS - 1)
    return np.where(n < max_exact, n, large).astype(np.int32)


SAMPLE_KEY_ROWS = 3 * WINDOW


def _sample_key_positions(t_len):
    c = np.arange(SAMPLE_KEY_ROWS)
    n_cache = SWA_KV_HEADS * WINDOW
    assert n_cache + SWA_KV_HEADS * t_len <= SAMPLE_KEY_ROWS
    return np.where(c < n_cache, c // SWA_KV_HEADS, WINDOW + (c - n_cache) // SWA_KV_HEADS)


def _bias_kernel(tab_ref, bp_ref, bs_ref, op_ref, os_ref):
    bp = bp_ref[...]
    bs = bs_ref[...]
    for hh in range(SWA_Q_HEADS):
        accp = jnp.zeros(bp.shape, F32)
        accs = jnp.zeros(bs.shape, F32)
        for k in range(NUM_BUCKETS):
            t = tab_ref[k * SWA_Q_HEADS + hh] * LOG2E
            accp = jnp.where(bp == k, t, accp)
            accs = jnp.where(bs == k, t, accs)
        h, g = divmod(hh, SWA_GROUP)
        op_ref[h, g * WINDOW:(g + 1) * WINDOW, :] = accp
        os_ref[hh * SUBLANES:(hh + 1) * SUBLANES, :] = accs


def _bias_tables(table, t_len):
    qi = np.arange(WINDOW)[:, None]
    kj = np.arange(2 * WINDOW)[None, :]
    bkt_p = _rel_bucket_np(WINDOW + qi - kj)
    key_pos = _sample_key_positions(t_len)
    bkt_s = _rel_bucket_np(np.arange(t_len)[:, None] + WINDOW - key_pos[None, :])
    return pl.pallas_call(
        _bias_kernel,
        out_shape=(jax.ShapeDtypeStruct((SWA_KV_HEADS, SWA_GROUP * WINDOW, 2 * WINDOW), F32),
                   jax.ShapeDtypeStruct((SWA_Q_HEADS * t_len, SAMPLE_KEY_ROWS), F32)),
        in_specs=[pl.BlockSpec(memory_space=pltpu.SMEM),
                  pl.BlockSpec(memory_space=pltpu.VMEM),
                  pl.BlockSpec(memory_space=pltpu.VMEM)],
        out_specs=(pl.BlockSpec(memory_space=pltpu.VMEM), pl.BlockSpec(memory_space=pltpu.VMEM)),
        name="bias",
    )(table.reshape(-1), jnp.asarray(bkt_p), jnp.asarray(bkt_s))


def _round_specs(weights, steps):
    specs = [pl.BlockSpec((w.shape[0] // steps, w.shape[1]), lambda i: (i, 0)) for w in weights]
    shapes = [jax.ShapeDtypeStruct(w.shape, BF16) for w in weights]
    return specs, shapes


def _round_slabs(srcs, dsts):
    for src, dst in zip(srcs, dsts):
        dst[...] = src[...].astype(BF16)


def _memkv_kernel(n_round, m_ref, g_ref, wk_ref, wv_ref, gk_ref, *refs):
    mk_ref, mv_ref, mkb_ref, mvb_ref = refs[n_round:n_round + 4]
    _round_slabs(refs[:n_round], refs[n_round + 4:])
    h = _rms(m_ref[...], g_ref[...]).astype(BF16)
    zk = jnp.dot(h, wk_ref[...].astype(BF16), preferred_element_type=F32)
    zv = jnp.dot(h, wv_ref[...].astype(BF16), preferred_element_type=F32)
    gk = gk_ref[...]
    tokens = m_ref.shape[0]
    for hx in range(X_HEADS):
        sl = slice(hx * HEAD_DIM, (hx + 1) * HEAD_DIM)
        head_rows = pl.ds(hx, tokens, stride=X_HEADS)
        mk = _rms(zk[:, sl], gk)
        mk_ref[head_rows, :] = mk
        mv_ref[head_rows, :] = zv[:, sl]
        mkb_ref[:, sl] = mk.astype(BF16)
    mvb_ref[...] = zv.astype(BF16)


def _memory_kv(mem2d, g_mem, wk, wv, g_k_x, round_weights=()):
    n = mem2d.shape[0]
    tile = MEM_TOKENS
    row = lambda i: (i, 0)
    head_rows = pl.BlockSpec((tile * X_HEADS, HEAD_DIM), row)
    round_specs, round_shapes = _round_specs(round_weights, n // tile)
    return pl.pallas_call(
        functools.partial(_memkv_kernel, len(round_weights)),
        grid=(n // tile,),
        in_specs=[pl.BlockSpec((tile, D_MODEL), row), _resident((1, D_MODEL)),
                  _resident((D_MODEL, X_WIDTH)), _resident((D_MODEL, X_WIDTH)), _resident((1, HEAD_DIM)),
                  *round_specs],
        out_specs=(head_rows, head_rows, pl.BlockSpec((tile, X_WIDTH), row), pl.BlockSpec((tile, X_WIDTH), row),
                   *round_specs),
        out_shape=(jax.ShapeDtypeStruct((n * X_HEADS, HEAD_DIM), F32),
                   jax.ShapeDtypeStruct((n * X_HEADS, HEAD_DIM), F32),
                   jax.ShapeDtypeStruct((n, X_WIDTH), BF16), jax.ShapeDtypeStruct((n, X_WIDTH), BF16),
                   *round_shapes),
        compiler_params=_params("arbitrary"),
        name="memkv",
    )(mem2d, g_mem, wk, wv, g_k_x, *round_weights)


def _proj_kernel(n_round, x_ref, g_ref, w_ref, gq_ref, gk_ref, gx_ref, *refs):
    b_ref, u_ref, q_ref, k_ref, v_ref, kb_ref, vb_ref, qx_ref = refs[n_round:n_round + 8]
    _round_slabs(refs[:n_round], refs[n_round + 8:])
    for r0 in range(0, x_ref.shape[0], ROW_CHUNK):
        r1 = r0 + ROW_CHUNK
        rows = slice(r0, r1)
        h = _rms(x_ref[rows, :], g_ref[...]).astype(BF16)

        def seg(lo, width):
            return jnp.dot(h, w_ref[:, lo:lo + width], preferred_element_type=F32)

        b_ref[rows, :] = seg(OFF_B, CONV_WIDTH)
        u_ref[rows, :] = seg(OFF_C, CONV_WIDTH) * seg(OFF_H, CONV_WIDTH)

        def head_norm(z, g, n_heads, out):
            for hh in range(n_heads):
                sl = slice(hh * HEAD_DIM, (hh + 1) * HEAD_DIM)
                out[rows, sl] = _rms(z[:, sl], g).astype(out.dtype)

        head_norm(seg(OFF_Q, SWA_WIDTH), gq_ref[...], SWA_Q_HEADS, q_ref)
        head_norm(seg(OFF_QX, X_WIDTH), gx_ref[...], X_HEADS, qx_ref)
        zk = seg(OFF_K, SWA_KV_WIDTH)
        zv = seg(OFF_V, SWA_KV_WIDTH)
        vb_ref[rows, :] = zv.astype(BF16)
        for hh in range(SWA_KV_HEADS):
            sl = slice(hh * HEAD_DIM, (hh + 1) * HEAD_DIM)
            head_rows = pl.ds(r0 * SWA_KV_HEADS + hh, r1 - r0, stride=SWA_KV_HEADS)
            k = _rms(zk[:, sl], gk_ref[...])
            k_ref[head_rows, :] = k
            v_ref[head_rows, :] = zv[:, sl]
            kb_ref[rows, sl] = k.astype(BF16)


def _projections(x2d, g_mix, w_in, g_q, g_k, g_qx, q_dtype, round_weights=()):
    n = x2d.shape[0]
    tile = min(ROW_TILE, n)
    row = lambda i: (i, 0)
    outs = ((1, CONV_WIDTH, F32), (1, CONV_WIDTH, F32), (1, SWA_WIDTH, q_dtype),
            (SWA_KV_HEADS, HEAD_DIM, F32), (SWA_KV_HEADS, HEAD_DIM, F32),
            (1, SWA_KV_WIDTH, BF16), (1, SWA_KV_WIDTH, BF16), (1, X_WIDTH, q_dtype))
    round_specs, round_shapes = _round_specs(round_weights, n // tile)
    return pl.pallas_call(
        functools.partial(_proj_kernel, len(round_weights)),
        grid=(n // tile,),
        in_specs=[pl.BlockSpec((tile, D_MODEL), row), _resident((1, D_MODEL)), _resident((D_MODEL, IN_WIDTH)),
                  _resident((1, HEAD_DIM)), _resident((1, HEAD_DIM)), _resident((1, HEAD_DIM)), *round_specs],
        out_specs=(*(pl.BlockSpec((tile * r, w), row) for r, w, _ in outs), *round_specs),
        out_shape=(*(jax.ShapeDtypeStruct((n * r, w), dt) for r, w, dt in outs), *round_shapes),
        compiler_params=_params("arbitrary"),
        name="proj",
    )(x2d, g_mix, w_in, g_q, g_k, g_qx, *round_weights)


def _mix_out_kernel(sink_ref, q_ref, kc_ref, kp_ref, vc_ref, vp_ref, qx_ref, mk_ref, mv_ref,
                    b_ref, uc_ref, up_ref, cw_ref, bias_ref, band_ref, x_ref, w_ref, g_ref, x1_ref, h_ref):
    has_prev = pl.program_id(1) > 0
    rows = MIX_BLOCKS * WINDOW
    blocks = [slice(j * WINDOW, (j + 1) * WINDOW) for j in range(MIX_BLOCKS)]

    u = uc_ref[...]
    prev = jnp.where(has_prev, up_ref[...], 0.0)
    ext = jnp.concatenate([prev, u], axis=0)
    cw = cw_ref[...]
    conv = cw[0:1] * ext[SUBLANES - 2:SUBLANES - 2 + rows]
    conv = conv + cw[1:2] * ext[SUBLANES - 1:SUBLANES - 1 + rows]
    conv = conv + cw[2:3] * u
    y_cols = [(b_ref[...] * conv).astype(BF16)]

    ones_cols = jnp.ones((2 * WINDOW, HEAD_DIM), BF16)

    def weighted_values(p, v_ones, sink=None, m=None):
        ov = jnp.dot(p.astype(BF16), v_ones, preferred_element_type=F32)
        den = ov[:, HEAD_DIM:]
        if sink is not None:
            den = den + jnp.exp2(sink - m)
        return (ov[:, :HEAD_DIM] * (1.0 / den)).astype(BF16)

    for h in range(SWA_KV_HEADS):
        ksl = slice(h * HEAD_DIM, (h + 1) * HEAD_DIM)
        outs = [[None] * MIX_BLOCKS for _ in range(SWA_GROUP)]
        for j, rsl in enumerate(blocks):
            min_band = jnp.where(has_prev, 0.5, 1.5) if j == 0 else 0.5
            if j == 0:
                k_all = jnp.concatenate([kp_ref[:, ksl], kc_ref[rsl, ksl]], axis=0)
                v_all = jnp.concatenate([vp_ref[:, ksl], vc_ref[rsl, ksl]], axis=0)
            else:
                k_all = kc_ref[(j - 1) * WINDOW:(j + 1) * WINDOW, ksl]
                v_all = vc_ref[(j - 1) * WINDOW:(j + 1) * WINDOW, ksl]
            v_ones = jnp.concatenate([v_all, ones_cols], axis=1)
            for g in range(SWA_GROUP):
                hh = h * SWA_GROUP + g
                q = q_ref[rsl, hh * HEAD_DIM:(hh + 1) * HEAD_DIM]
                s = lax.dot_general(q, k_all, NT_DIMS, preferred_element_type=F32) * SCALE_LOG2
                s = jnp.where(band_ref[...] > min_band, s + bias_ref[h, g * WINDOW:(g + 1) * WINDOW, :], NEG)
                sink = sink_ref[hh] * LOG2E
                m = jnp.maximum(jnp.max(s, axis=-1, keepdims=True), sink)
                outs[g][j] = weighted_values(jnp.exp2(s - m), v_ones, sink, m)
        y_cols += [jnp.concatenate(o, axis=0) for o in outs]

    outs = [[None] * MIX_BLOCKS for _ in range(X_HEADS)]
    for hx in range(X_HEADS):
        sl = slice(hx * HEAD_DIM, (hx + 1) * HEAD_DIM)
        mv_ones = jnp.concatenate([mv_ref[:, sl], ones_cols], axis=1)
        for j, rsl in enumerate(blocks):
            s = lax.dot_general(qx_ref[rsl, sl], mk_ref[:, sl], NT_DIMS, preferred_element_type=F32) * SCALE_LOG2
            m = jnp.max(s, axis=-1, keepdims=True)
            outs[hx][j] = weighted_values(jnp.exp2(s - m), mv_ones)
    y_cols += [jnp.concatenate(o, axis=0) for o in outs]

    x1 = x_ref[...] + jnp.dot(jnp.concatenate(y_cols, axis=1), w_ref[...], preferred_element_type=F32)
    x1_ref[...] = x1
    h_ref[...] = _rms(x1, g_ref[...]).astype(BF16)


def _mix_out_prompt(sinks, q, kb, vb, qx, mkb, mvb, b, u, conv_w, bias_p, x2d, w_out, g_mlp, bsz, seq):
    rows = MIX_BLOCKS * WINDOW
    steps = seq // rows
    cur = lambda bi, i: (bi * steps + i, 0)
    prv = lambda bi, i: (jnp.maximum((bi * steps + i) * MIX_BLOCKS - 1, 0), 0)
    prv8 = lambda bi, i: (jnp.maximum((bi * steps + i) * (rows // SUBLANES) - 1, 0), 0)
    per_b = lambda bi, i: (bi, 0)
    dist = WINDOW + np.arange(WINDOW)[:, None] - np.arange(2 * WINDOW)[None, :]
    band = np.where((dist >= 0) & (dist < WINDOW), np.where(np.arange(2 * WINDOW)[None, :] < WINDOW, 1.0, 2.0), 0.0)
    return pl.pallas_call(
        _mix_out_kernel,
        grid=(bsz, steps),
        in_specs=[pl.BlockSpec(memory_space=pltpu.SMEM),
                  pl.BlockSpec((rows, SWA_WIDTH), cur),
                  pl.BlockSpec((rows, SWA_KV_WIDTH), cur), pl.BlockSpec((WINDOW, SWA_KV_WIDTH), prv),
                  pl.BlockSpec((rows, SWA_KV_WIDTH), cur), pl.BlockSpec((WINDOW, SWA_KV_WIDTH), prv),
                  pl.BlockSpec((rows, X_WIDTH), cur),
                  pl.BlockSpec((MEM_TOKENS, X_WIDTH), per_b), pl.BlockSpec((MEM_TOKENS, X_WIDTH), per_b),
                  pl.BlockSpec((rows, CONV_WIDTH), cur), pl.BlockSpec((rows, CONV_WIDTH), cur),
                  pl.BlockSpec((SUBLANES, CONV_WIDTH), prv8),
                  _resident((CONV_K, CONV_WIDTH)),
                  _resident((SWA_KV_HEADS, SWA_GROUP * WINDOW, 2 * WINDOW)),
                  _resident((WINDOW, 2 * WINDOW)),
                  pl.BlockSpec((rows, D_MODEL), cur),
                  _resident((D_MODEL, D_MODEL)), _resident((1, D_MODEL))],
        out_specs=(pl.BlockSpec((rows, D_MODEL), cur), pl.BlockSpec((rows, D_MODEL), cur)),
        out_shape=(jax.ShapeDtypeStruct((bsz * seq, D_MODEL), F32), jax.ShapeDtypeStruct((bsz * seq, D_MODEL), BF16)),
        compiler_params=_params("arbitrary", "arbitrary"),
        name="mix_out",
    )(sinks, q, kb, kb, vb, vb, qx, mkb, mvb, b, u, u, conv_w, bias_p, jnp.asarray(band, F32), x2d, w_out, g_mlp)


def _mix_sample_kernel(t_len, sink_ref, q_ref, kn_ref, vn_ref, ck_ref, cv_ref, qx_ref, cmk_ref, cmv_ref,
                       b_ref, u_ref, cc_ref, cw_ref, bias_ref, x_ref, w_ref, g_ref,
                       x1_ref, h_ref, sk_ref, sv_ref, o_ref):
    nb = cc_ref.shape[0]
    n_cache = SWA_KV_HEADS * WINDOW
    n_new = SWA_KV_HEADS * t_len
    n_mem = X_HEADS * MEM_TOKENS
    log_t = int(math.log2(t_len))

    rows = SWA_Q_HEADS * t_len
    r = lax.broadcasted_iota(jnp.int32, (nb * rows, SAMPLE_KEY_ROWS), 0) & (rows - 1)
    c = lax.broadcasted_iota(jnp.int32, (nb * rows, SAMPLE_KEY_ROWS), 1)
    key_pos = jnp.where(c < n_cache, c >> 1, WINDOW + ((c - n_cache) >> 1))
    dist = (r & (t_len - 1)) + WINDOW - key_pos
    valid = (dist >= 0) & (dist < WINDOW) & ((c & (SWA_KV_HEADS - 1)) == (r >> int(math.log2(SWA_GROUP * t_len))))
    bias = jnp.concatenate([bias_ref[...]] * nb, axis=0)
    sink_col = jnp.concatenate([jnp.full((t_len, 1), sink_ref[hh], F32) for hh in range(SWA_Q_HEADS)] * nb, axis=0)
    xrows = X_HEADS * t_len
    xr = lax.broadcasted_iota(jnp.int32, (nb * xrows, n_mem), 0) & (xrows - 1)
    xc = lax.broadcasted_iota(jnp.int32, (nb * xrows, n_mem), 1)
    x_valid = (xc & (X_HEADS - 1)) == (xr >> log_t)
    cw = cw_ref[...]
    trow = lax.broadcasted_iota(jnp.int32, (t_len, CONV_WIDTH), 0)
    zeros_pad = jnp.zeros((SAMPLE_KEY_ROWS - n_cache - n_new, HEAD_DIM), F32)

    s_list, sx_list = [], []
    for bi in range(nb):
        rsl = slice(bi * t_len, (bi + 1) * t_len)
        c0, n0, m0 = bi * n_cache, bi * n_new, bi * n_mem

        for dst, cache, new in ((sk_ref, ck_ref, kn_ref), (sv_ref, cv_ref, vn_ref)):
            dst[c0:c0 + n_cache - n_new, :] = cache[c0 + n_new:c0 + n_cache, :]
            dst[c0 + n_cache - n_new:c0 + n_cache, :] = new[n0:n0 + n_new, :]

        u = u_ref[rsl, :]
        cc = cc_ref[bi]
        cc1 = jnp.broadcast_to(cc[1:2], u.shape)
        cc0 = jnp.broadcast_to(cc[0:1], u.shape)
        u_m1 = jnp.where(trow >= 1, pltpu.roll(u, 1, 0), cc1)
        u_m2 = jnp.where(trow >= 2, pltpu.roll(u, 2, 0), jnp.where(trow == 1, cc1, cc0))
        conv = cw[0:1] * u_m2
        conv = conv + cw[1:2] * u_m1
        conv = conv + cw[2:3] * u
        o_ref[rsl, 0:CONV_WIDTH] = (b_ref[rsl, :] * conv).astype(o_ref.dtype)

        qb = q_ref[rsl, :]
        q_rows = jnp.concatenate([qb[:, hh * HEAD_DIM:(hh + 1) * HEAD_DIM] for hh in range(SWA_Q_HEADS)], axis=0)
        k_all = jnp.concatenate([ck_ref[c0:c0 + n_cache, :], kn_ref[n0:n0 + n_new, :], zeros_pad], axis=0)
        s_list.append(lax.dot_general(q_rows.astype(BF16), k_all.astype(BF16), NT_DIMS, preferred_element_type=F32))
        qxb = qx_ref[rsl, :]
        qx_rows = jnp.concatenate([qxb[:, hx * HEAD_DIM:(hx + 1) * HEAD_DIM] for hx in range(X_HEADS)], axis=0)
        mk = cmk_ref[m0:m0 + n_mem, :].astype(BF16)
        sx_list.append(lax.dot_general(qx_rows.astype(BF16), mk, NT_DIMS, preferred_element_type=F32))

    s = jnp.concatenate(s_list, axis=0) * SCALE_LOG2
    s = jnp.where(valid, s + bias, NEG)
    sink2 = sink_col * LOG2E
    m = jnp.maximum(jnp.max(s, axis=-1, keepdims=True), sink2)
    w = jnp.exp2(s - m).astype(BF16)
    sink_term = jnp.exp2(sink2 - m)
    sx = jnp.where(x_valid, jnp.concatenate(sx_list, axis=0) * SCALE_LOG2, NEG)
    wx = jnp.exp2(sx - jnp.max(sx, axis=-1, keepdims=True)).astype(BF16)
    ones_keys = jnp.ones((SAMPLE_KEY_ROWS, HEAD_DIM), BF16)
    ones_mem = jnp.ones((n_mem, HEAD_DIM), BF16)

    for bi in range(nb):
        rsl = slice(bi * t_len, (bi + 1) * t_len)
        c0, n0, m0 = bi * n_cache, bi * n_new, bi * n_mem
        v_all = jnp.concatenate([cv_ref[c0:c0 + n_cache, :], vn_ref[n0:n0 + n_new, :], zeros_pad], axis=0)
        srows = slice(bi * rows, (bi + 1) * rows)
        ov = jnp.dot(w[srows], jnp.concatenate([v_all.astype(BF16), ones_keys], axis=1), preferred_element_type=F32)
        o = ov[:, :HEAD_DIM] * (1.0 / (ov[:, HEAD_DIM:] + sink_term[srows]))
        for hh in range(SWA_Q_HEADS):
            col = CONV_WIDTH + hh * HEAD_DIM
            o_ref[rsl, col:col + HEAD_DIM] = o[hh * t_len:(hh + 1) * t_len].astype(o_ref.dtype)
        mv = jnp.concatenate([cmv_ref[m0:m0 + n_mem, :].astype(BF16), ones_mem], axis=1)
        xsl = slice(bi * xrows, (bi + 1) * xrows)
        ovx = jnp.dot(wx[xsl], mv, preferred_element_type=F32)
        ox = ovx[:, :HEAD_DIM] * (1.0 / ovx[:, HEAD_DIM:])
        for hx in range(X_HEADS):
            col = CONV_WIDTH + SWA_WIDTH + hx * HEAD_DIM
            o_ref[rsl, col:col + HEAD_DIM] = ox[hx * t_len:(hx + 1) * t_len].astype(o_ref.dtype)

    x1 = x_ref[...] + jnp.dot(o_ref[...].astype(BF16), w_ref[...], preferred_element_type=F32)
    x1_ref[...] = x1
    h_ref[...] = _rms(x1, g_ref[...]).astype(BF16)


def _mix_out_sample(sinks, q, k_rows, v_rows, cache_k, cache_v, qx, cache_mk, cache_mv, b, u, cache_conv, conv_w,
                    bias_s, x2d, w_out, g_mlp, bsz, t_len):
    assert t_len == SUBLANES and bsz % SAMPLE_BATCH_TILE == 0 and SWA_KV_HEADS == 2
    nb = SAMPLE_BATCH_TILE
    n_cache, n_new, n_mem = SWA_KV_HEADS * WINDOW, SWA_KV_HEADS * t_len, X_HEADS * MEM_TOKENS
    row = lambda i: (i, 0)
    rows_of = lambda n, w: pl.BlockSpec((nb * n, w), row)
    return pl.pallas_call(
        functools.partial(_mix_sample_kernel, t_len),
        grid=(bsz // nb,),
        in_specs=[pl.BlockSpec(memory_space=pltpu.SMEM),
                  rows_of(t_len, SWA_WIDTH),
                  rows_of(n_new, HEAD_DIM), rows_of(n_new, HEAD_DIM),
                  rows_of(n_cache, HEAD_DIM), rows_of(n_cache, HEAD_DIM),
                  rows_of(t_len, X_WIDTH),
                  rows_of(n_mem, HEAD_DIM), rows_of(n_mem, HEAD_DIM),
                  rows_of(t_len, CONV_WIDTH), rows_of(t_len, CONV_WIDTH),
                  pl.BlockSpec((nb, CONV_K - 1, CONV_WIDTH), lambda i: (i, 0, 0)),
                  _resident((CONV_K, CONV_WIDTH)),
                  _resident((SWA_Q_HEADS * t_len, SAMPLE_KEY_ROWS)),
                  rows_of(t_len, D_MODEL), _resident((D_MODEL, D_MODEL)), _resident((1, D_MODEL))],
        out_specs=(rows_of(t_len, D_MODEL), rows_of(t_len, D_MODEL),
                   rows_of(n_cache, HEAD_DIM), rows_of(n_cache, HEAD_DIM)),
        out_shape=(jax.ShapeDtypeStruct((bsz * t_len, D_MODEL), F32),
                   jax.ShapeDtypeStruct((bsz * t_len, D_MODEL), BF16),
                   jax.ShapeDtypeStruct((bsz * n_cache, HEAD_DIM), F32),
                   jax.ShapeDtypeStruct((bsz * n_cache, HEAD_DIM), F32)),
        scratch_shapes=[pltpu.VMEM((nb * t_len, D_MODEL), F32)],
        compiler_params=_params("arbitrary"),
        name="mix_out_sample",
    )(sinks, q, k_rows, v_rows, cache_k, cache_v, qx, cache_mk, cache_mv, b, u, cache_conv, conv_w, bias_s,
      x2d, w_out, g_mlp)


def _mlp_kernel(x1_ref, h_ref, wu_ref, wd_ref, o_ref):
    @pl.when(pl.program_id(1) == 0)
    def _():
        o_ref[...] = x1_ref[...]

    a = jnp.maximum(jnp.dot(h_ref[...], wu_ref[...], preferred_element_type=F32), 0.0)
    o_ref[...] += jnp.dot((a * a).astype(BF16), wd_ref[...], preferred_element_type=F32)


def _mlp_cast_kernel(x1_hbm, h_ref, wu_ref, wd_ref, o_ref, wub_ref, wdb_ref, sem):
    j = pl.program_id(1)
    residual_copy = pltpu.make_async_copy(x1_hbm, o_ref, sem)

    @pl.when(j == 0)
    def _():
        residual_copy.start()

    wu = wu_ref[...].astype(BF16)
    wub_ref[...] = wu
    a = jnp.maximum(jnp.dot(h_ref[...], wu, preferred_element_type=F32), 0.0)
    a = (a * a).astype(BF16)

    @pl.when(j == 0)
    def _():
        residual_copy.wait()

    for c0 in range(0, D_MODEL, MLP_CAST_FF_TILE):
        cols = slice(c0, c0 + MLP_CAST_FF_TILE)
        wd = wd_ref[:, cols].astype(BF16)
        wdb_ref[:, cols] = wd
        o_ref[:, cols] += jnp.dot(a, wd, preferred_element_type=F32)


def _mlp(x1, h, w_up, w_down):
    n = x1.shape[0]
    tile = min(MLP_ROW_TILE, n)
    rows = pl.BlockSpec((tile, D_MODEL), lambda i, j: (i, 0))
    return pl.pallas_call(
        _mlp_kernel,
        grid=(n // tile, D_FF // MLP_FF_TILE),
        in_specs=[rows, rows,
                  pl.BlockSpec((D_MODEL, MLP_FF_TILE), lambda i, j: (0, j)),
                  pl.BlockSpec((MLP_FF_TILE, D_MODEL), lambda i, j: (j, 0))],
        out_specs=rows,
        out_shape=jax.ShapeDtypeStruct((n, D_MODEL), F32),
        compiler_params=_params("arbitrary", "arbitrary", vmem_limit=VMEM_LIMIT_MLP_V7X),
        name="mlp",
    )(x1, h, w_up, w_down)


def _mlp_cast(x1, h, w_up, w_down):
    n = x1.shape[0]
    rows = pl.BlockSpec((n, D_MODEL), lambda i, j: (0, 0))
    up_spec = pl.BlockSpec((D_MODEL, MLP_CAST_FF_TILE), lambda i, j: (0, j))
    down_spec = pl.BlockSpec((MLP_CAST_FF_TILE, D_MODEL), lambda i, j: (j, 0))
    return pl.pallas_call(
        _mlp_cast_kernel,
        grid=(1, D_FF // MLP_CAST_FF_TILE),
        in_specs=[pl.BlockSpec(memory_space=pl.ANY), rows, up_spec, down_spec],
        out_specs=(rows, up_spec, down_spec),
        out_shape=(jax.ShapeDtypeStruct((n, D_MODEL), F32),
                   jax.ShapeDtypeStruct(w_up.shape, BF16), jax.ShapeDtypeStruct(w_down.shape, BF16)),
        scratch_shapes=[pltpu.SemaphoreType.DMA(())],
        compiler_params=_params("arbitrary", "arbitrary"),
        name="mlp_cast",
    )(x1, h, w_up, w_down)


def kernel(x_prompt, x_sample, mem_prompt, cache_conv, cache_swa_k, cache_swa_v, cache_mem_k, cache_mem_v,
           rel_bias_table, g_mix, w_in, conv_w, g_q_swa, g_k_swa, sinks, g_q_x, g_k_x, g_mem,
           w_mem_k, w_mem_v, w_out, g_mlp, w_up, w_down):
    depth = w_in.shape[0]
    bsz, seq, _ = x_prompt.shape
    dbsz, t_len, _ = x_sample.shape
    xp = x_prompt.reshape(bsz * seq, D_MODEL)
    xs = x_sample.reshape(dbsz * t_len, D_MODEL)
    mem2d = mem_prompt.reshape(bsz * MEM_TOKENS, D_MODEL)
    bias_p, bias_s = _bias_tables(rel_bias_table, t_len)

    outs = [[] for _ in range(8)]
    for l in range(depth):
        vec = lambda a: a[l].reshape(1, -1)
        mk, mv, mkb, mvb, wi = _memory_kv(mem2d, vec(g_mem), w_mem_k[l], w_mem_v[l], vec(g_k_x),
                                          round_weights=(w_in[l],))
        proj = functools.partial(_projections, g_mix=vec(g_mix), w_in=wi, g_q=vec(g_q_swa), g_k=vec(g_k_swa),
                                 g_qx=vec(g_q_x))
        pb, pu, pq, pk, pv, pkb, pvb, pqx, wo = proj(xp, q_dtype=BF16, round_weights=(w_out[l],))

        head_rows = lambda a: a.reshape(-1, HEAD_DIM)
        b, u, q, k, v, _, _, qx = proj(xs, q_dtype=F32)
        x1, h, sk, sv = _mix_out_sample(
            sinks[l], q, k, v, head_rows(cache_swa_k[l]), head_rows(cache_swa_v[l]),
            qx, head_rows(cache_mem_k[l]), head_rows(cache_mem_v[l]),
            b, u, cache_conv[l], conv_w[l], bias_s, xs, wo, vec(g_mlp), dbsz, t_len)
        xs, wu, wd = _mlp_cast(x1, h, w_up[l], w_down[l])
        outs[5].append(u.reshape(dbsz, t_len, CONV_WIDTH)[:, t_len - (CONV_K - 1):])
        outs[6].append(sk.reshape(dbsz, WINDOW, SWA_KV_HEADS, HEAD_DIM))
        outs[7].append(sv.reshape(dbsz, WINDOW, SWA_KV_HEADS, HEAD_DIM))

        x1, h = _mix_out_prompt(sinks[l], pq, pkb, pvb, pqx, mkb, mvb, pb, pu, conv_w[l], bias_p, xp, wo,
                                vec(g_mlp), bsz, seq)
        xp = _mlp(x1, h, wu, wd)
        outs[0].append(pu.reshape(bsz, seq, CONV_WIDTH)[:, seq - (CONV_K - 1):])
        last_window = lambda a: a.reshape(bsz, seq, SWA_KV_HEADS, HEAD_DIM)[:, seq - WINDOW:]
        outs[1].append(last_window(pk))
        outs[2].append(last_window(pv))
        outs[3].append(mk.reshape(bsz, MEM_TOKENS, X_HEADS, HEAD_DIM))
        outs[4].append(mv.reshape(bsz, MEM_TOKENS, X_HEADS, HEAD_DIM))

    return (xp.reshape(bsz, seq, D_MODEL), xs.reshape(dbsz, t_len, D_MODEL)) + tuple(jnp.stack(o) for o in outs)
```
